```python
import math
import jax, jax.numpy as jnp
from jax import lax
import numpy as np

D_MODEL = 2048
BATCH = 1
SEQ = 16384
DEPTH = 4
DEC_BATCH = 2
DEC_SEQ = 4096
PAST_LEN = 128

PLE_DIM = 256
GRID_W = 64
HEAD_DIM = 128
EPS = 1e-6
Q_BLOCK = 128
NA_HEADS = 6
NA_WIN_ROWS = 8
NA_WIN_COLS = 16
MLA_HEADS = 5
MLA_Q_RANK = 512
MLA_KV_RANK = 256
MLA_NOPE_DIM = 128
MLA_ROPE_DIM = 64
MLA_V_DIM = 128
MLA_ROPE_THETA = 10000.0
DIFF_HEADS = 5
DIFF_QK_DIM = 64
DIFF_V_DIM = 128
ROPE_THETA = 500000.0
ROPE_PART_DIM = DIFF_QK_DIM // 4
D_FF = 5632
CONV_W = 3

NA_W = NA_HEADS * HEAD_DIM
MLA_QK_DIM = MLA_NOPE_DIM + MLA_ROPE_DIM
DIFF_QK_W = DIFF_HEADS * 2 * DIFF_QK_DIM
D_MIX = NA_W + MLA_HEADS * MLA_V_DIM + DIFF_HEADS * DIFF_V_DIM
_SIZES = (NA_W, NA_W, NA_W, MLA_Q_RANK, MLA_KV_RANK, MLA_ROPE_DIM,
          DIFF_QK_W, DIFF_QK_W, DIFF_HEADS * DIFF_V_DIM)
IN_COLS = sum(_SIZES)
SPLIT_POINTS = tuple(int(v) for v in np.cumsum(_SIZES)[:-1])

kernel_name = "hybrid_na_mla_diff_encoder"


def rms_norm(x, g):
    xf = x.astype(jnp.float32)
    y = xf * lax.rsqrt(jnp.mean(xf * xf, axis=-1, keepdims=True) + EPS)
    return (y * g.astype(jnp.float32)).astype(x.dtype)


def rope_angles(T, dim, theta):
    inv = 1.0 / (theta ** (jnp.arange(0, dim, 2, dtype=jnp.float32) / dim))
    ang = jnp.arange(T, dtype=jnp.float32)[:, None] * inv[None, :]
    return jnp.cos(ang), jnp.sin(ang)


def apply_rope(x, cos, sin):
    x1, x2 = jnp.split(x, 2, axis=-1)
    c = cos.astype(x.dtype)
    s = sin.astype(x.dtype)
    return jnp.concatenate([x1 * c - x2 * s, x1 * s + x2 * c], axis=-1)


def to_heads(u, n_heads):
    B, T, _ = u.shape
    return u.reshape(B, T, n_heads, -1).transpose(0, 2, 1, 3)


def from_heads(u):
    B, H, T, d = u.shape
    return u.transpose(0, 2, 1, 3).reshape(B, T, H * d)


def neighbourhood_attention(q, k, v, rpb):
    B, H, T, dh = q.shape
    rows = T // GRID_W
    kh = min(NA_WIN_ROWS, rows)
    kw = NA_WIN_COLS
    qg = q.reshape(B, H, rows, GRID_W, dh)
    kg = k.reshape(B, H, rows, GRID_W, dh)
    vg = v.reshape(B, H, rows, GRID_W, dh)
    cols = jnp.arange(GRID_W)
    col_start = jnp.clip(cols - kw // 2, 0, GRID_W - kw)
    col_idx = col_start[:, None] + jnp.arange(kw)[None, :]
    col_bias_idx = col_idx - cols[:, None] + (NA_WIN_COLS - 1)
    scale = dh ** -0.5

    def one_row(r):
        rs = jnp.clip(r - kh // 2, 0, rows - kh)
        k_rows = lax.dynamic_slice_in_dim(kg, rs, kh, axis=2)
        v_rows = lax.dynamic_slice_in_dim(vg, rs, kh, axis=2)
        k_win = k_rows[:, :, :, col_idx]
        v_win = v_rows[:, :, :, col_idx]
        q_r = lax.dynamic_index_in_dim(qg, r, axis=2, keepdims=False)
        s = jnp.einsum('bhcd,bhicjd->bhcij', q_r, k_win).astype(jnp.float32) * scale
        row_bias_idx = rs + jnp.arange(kh) - r + (NA_WIN_ROWS - 1)
        bias = rpb[:, row_bias_idx[None, :, None], col_bias_idx[:, None, :]]
        s = s + bias.astype(jnp.float32)[None]
        p = jax.nn.softmax(s.reshape(B, H, GRID_W, kh * kw), axis=-1)
        p = p.reshape(B, H, GRID_W, kh, kw).astype(v.dtype)
        return jnp.einsum('bhcij,bhicjd->bhcd', p, v_win)

    o = lax.map(one_row, jnp.arange(rows))
    return o.transpose(1, 2, 0, 3, 4).reshape(B, H, T, dh)


def dense_attention(q, k, v):
    B, H, T, dq = q.shape
    nb = T // Q_BLOCK
    qb = q.reshape(B, H, nb, Q_BLOCK, dq).transpose(2, 0, 1, 3, 4)
    scale = dq ** -0.5

    def one(qblk):
        s = jnp.einsum('bhqd,bhkd->bhqk', qblk, k).astype(jnp.float32) * scale
        p = jax.nn.softmax(s, axis=-1).astype(v.dtype)
        return jnp.einsum('bhqk,bhkd->bhqd', p, v)

    o = lax.map(one, qb)
    return o.transpose(1, 2, 0, 3, 4).reshape(B, H, T, v.shape[-1])


def differential_attention(q, k, v, lam):
    B, H, _, T, dq = q.shape
    nb = T // Q_BLOCK
    qb = q.reshape(B, H, 2, nb, Q_BLOCK, dq).transpose(3, 0, 1, 2, 4, 5)
    scale = dq ** -0.5

    def one(qblk):
        s = jnp.einsum('bhcqd,bhckd->bhcqk', qblk, k).astype(jnp.float32) * scale
        p = jax.nn.softmax(s, axis=-1)
        a = (p[:, :, 0] - lam * p[:, :, 1]).astype(v.dtype)
        return jnp.einsum('bhqk,bhkd->bhqd', a, v)

    o = lax.map(one, qb)
    return o.transpose(1, 2, 0, 3, 4).reshape(B, H, T, v.shape[-1])


def dwconv_centred(u, w, b):
    up = jnp.pad(u, ((0, 0), (1, 1), (0, 0)))
    return up[:, :-2] * w[0] + up[:, 1:-1] * w[1] + up[:, 2:] * w[2] + b


def encoder_layer(x, p_l, layer_idx, norm_mix, w_in, na_q_norm, na_k_norm, na_rpb,
                  mla_q_a_norm, mla_w_q_b, mla_kv_a_norm, mla_w_kv_b,
                  mla_q_nope_norm, mla_q_pe_norm, mla_k_nope_norm, mla_k_pe_norm,
                  diff_q_norm, diff_k_norm, diff_lambda_q1, diff_lambda_k1,
                  diff_lambda_q2, diff_lambda_k2, diff_subln, w_out,
                  norm_ffn, w_gate, w_up, conv_w, conv_b, w_down,
                  ple_norm, w_ple_gate, w_ple_proj):
    B, T, _ = x.shape
    h = rms_norm(x, norm_mix)
    proj = jnp.einsum('btd,dc->btc', h, w_in)
    (na_q, na_k, na_v, mla_cq, mla_ckv, mla_kpe,
     df_q, df_k, df_v) = jnp.split(proj, SPLIT_POINTS, axis=-1)

    qa = rms_norm(to_heads(na_q, NA_HEADS), na_q_norm)
    ka = rms_norm(to_heads(na_k, NA_HEADS), na_k_norm)
    va = to_heads(na_v, NA_HEADS)
    o_na = neighbourhood_attention(qa, ka, va, na_rpb)

    cos_m, sin_m = rope_angles(T, MLA_ROPE_DIM, MLA_ROPE_THETA)
    c_q = rms_norm(mla_cq, mla_q_a_norm)
    qm = to_heads(jnp.einsum('btr,rc->btc', c_q, mla_w_q_b), MLA_HEADS)
    q_nope = rms_norm(qm[..., :MLA_NOPE_DIM], mla_q_nope_norm)
    q_pe = apply_rope(rms_norm(qm[..., MLA_NOPE_DIM:], mla_q_pe_norm), cos_m, sin_m)
    c_kv = rms_norm(mla_ckv, mla_kv_a_norm)
    kv = to_heads(jnp.einsum('btr,rc->btc', c_kv, mla_w_kv_b), MLA_HEADS)
    k_nope = rms_norm(kv[..., :MLA_NOPE_DIM], mla_k_nope_norm)
    vm = kv[..., MLA_NOPE_DIM:]
    k_pe = apply_rope(rms_norm(mla_kpe, mla_k_pe_norm)[:, None], cos_m, sin_m)
    q_full = jnp.concatenate([q_nope, q_pe], axis=-1)
    k_full = jnp.concatenate([k_nope, jnp.broadcast_to(k_pe, (B, MLA_HEADS, T, MLA_ROPE_DIM))], axis=-1)
    o_mla = dense_attention(q_full, k_full, vm)

    cos_d, sin_d = rope_angles(T, ROPE_PART_DIM, ROPE_THETA)
    def diff_qk(u, g):
        u = u.reshape(B, T, DIFF_HEADS, 2, DIFF_QK_DIM).transpose(0, 2, 3, 1, 4)
        u = rms_norm(u, g)
        return jnp.concatenate([apply_rope(u[..., :ROPE_PART_DIM], cos_d, sin_d),
                                u[..., ROPE_PART_DIM:]], axis=-1)
    qd = diff_qk(df_q, diff_q_norm)
    kd = diff_qk(df_k, diff_k_norm)
    vd = to_heads(df_v, DIFF_HEADS)
    lam_init = 0.8 - 0.6 * math.exp(-0.3 * layer_idx)
    lam = (jnp.exp(jnp.sum(diff_lambda_q1.astype(jnp.float32) * diff_lambda_k1.astype(jnp.float32)))
           - jnp.exp(jnp.sum(diff_lambda_q2.astype(jnp.float32) * diff_lambda_k2.astype(jnp.float32)))
           + lam_init)
    o_df = differential_attention(qd, kd, vd, lam)
    o_df = rms_norm(o_df, diff_subln) * (1.0 - lam_init)

    mixed = jnp.concatenate([from_heads(o_na), from_heads(o_mla), from_heads(o_df)], axis=-1)
    x = x + jnp.einsum('btc,cd->btd', mixed, w_out)

    h = rms_norm(x, norm_ffn)
    g = dwconv_centred(jnp.einsum('btd,df->btf', h, w_gate), conv_w, conv_b)
    u = jnp.einsum('btd,df->btf', h, w_up)
    x = x + jnp.einsum('btf,fd->btd', jax.nn.silu(g) * u, w_down)

    gate = jax.nn.sigmoid(jnp.einsum('btd,de->bte', rms_norm(x, ple_norm), w_ple_gate))
    x = x + gate * jnp.einsum('btp,pd->btd', p_l, w_ple_proj)
    return x


def setup_inputs(seed: int = 0) -> dict:
    key = jax.random.key(seed)
    ks = iter(jax.random.split(key, 48))
    L = DEPTH

    def nrm(shape, scale):
        return scale * jax.random.normal(next(ks), shape, jnp.float32)

    def gain(shape):
        return 1.0 + nrm(shape, 0.02)

    return {
        "x_prompt": nrm((BATCH, SEQ, D_MODEL), 1.0),
        "x_sample": nrm((DEC_BATCH, DEC_SEQ, D_MODEL), 1.0),
        "p_prompt": nrm((DEPTH, BATCH, SEQ, PLE_DIM), 1.0),
        "p_sample": nrm((DEPTH, DEC_BATCH, DEC_SEQ, PLE_DIM), 1.0),
        "norm_mix": gain((L, D_MODEL)),
        "w_in": nrm((L, D_MODEL, IN_COLS), D_MODEL ** -0.5),
        "na_q_norm": gain((L, HEAD_DIM)),
        "na_k_norm": gain((L, HEAD_DIM)),
        "na_rpb": nrm((L, NA_HEADS, 2 * NA_WIN_ROWS - 1, 2 * NA_WIN_COLS - 1), 0.1),
        "mla_q_a_norm": gain((L, MLA_Q_RANK)),
        "mla_w_q_b": nrm((L, MLA_Q_RANK, MLA_HEADS * MLA_QK_DIM), MLA_Q_RANK ** -0.5),
        "mla_kv_a_norm": gain((L, MLA_KV_RANK)),
        "mla_w_kv_b": nrm((L, MLA_KV_RANK, MLA_HEADS * (MLA_NOPE_DIM + MLA_V_DIM)), MLA_KV_RANK ** -0.5),
        "mla_q_nope_norm": gain((L, MLA_NOPE_DIM)),
        "mla_q_pe_norm": gain((L, MLA_ROPE_DIM)),
        "mla_k_nope_norm": gain((L, MLA_NOPE_DIM)),
        "mla_k_pe_norm": gain((L, MLA_ROPE_DIM)),
        "diff_q_norm": gain((L, DIFF_QK_DIM)),
        "diff_k_norm": gain((L, DIFF_QK_DIM)),
        "diff_lambda_q1": nrm((L, DIFF_QK_DIM), 0.1),
        "diff_lambda_k1": nrm((L, DIFF_QK_DIM), 0.1),
        "diff_lambda_q2": nrm((L, DIFF_QK_DIM), 0.1),
        "diff_lambda_k2": nrm((L, DIFF_QK_DIM), 0.1),
        "diff_subln": gain((L, DIFF_V_DIM)),
        "w_out": nrm((L, D_MIX, D_MODEL), D_MIX ** -0.5),
        "norm_ffn": gain((L, D_MODEL)),
        "w_gate": nrm((L, D_MODEL, D_FF), D_MODEL ** -0.5),
        "w_up": nrm((L, D_MODEL, D_FF), D_MODEL ** -0.5),
        "conv_w": nrm((L, CONV_W, D_FF), CONV_W ** -0.5),
        "conv_b": nrm((L, D_FF), 0.01),
        "w_down": nrm((L, D_FF, D_MODEL), D_FF ** -0.5),
        "ple_norm": gain((L, D_MODEL)),
        "w_ple_gate": nrm((L, D_MODEL, D_MODEL), D_MODEL ** -0.5),
        "w_ple_proj": nrm((L, PLE_DIM, D_MODEL), PLE_DIM ** -0.5),
    }


def reference(x_prompt, x_sample, p_prompt, p_sample, norm_mix, w_in, na_q_norm, na_k_norm, na_rpb,
              mla_q_a_norm, mla_w_q_b, mla_kv_a_norm, mla_w_kv_b,
              mla_q_nope_norm, mla_q_pe_norm, mla_k_nope_norm, mla_k_pe_norm,
              diff_q_norm, diff_k_norm, diff_lambda_q1, diff_lambda_k1,
              diff_lambda_q2, diff_lambda_k2, diff_subln, w_out,
              norm_ffn, w_gate, w_up, conv_w, conv_b, w_down,
              ple_norm, w_ple_gate, w_ple_proj):
    y_prompt = x_prompt
    y_sample = x_sample
    for i in range(DEPTH):
        lw = (norm_mix[i], w_in[i], na_q_norm[i], na_k_norm[i], na_rpb[i],
              mla_q_a_norm[i], mla_w_q_b[i], mla_kv_a_norm[i], mla_w_kv_b[i],
              mla_q_nope_norm[i], mla_q_pe_norm[i], mla_k_nope_norm[i], mla_k_pe_norm[i],
              diff_q_norm[i], diff_k_norm[i], diff_lambda_q1[i], diff_lambda_k1[i],
              diff_lambda_q2[i], diff_lambda_k2[i], diff_subln[i], w_out[i],
              norm_ffn[i], w_gate[i], w_up[i], conv_w[i], conv_b[i], w_down[i],
              ple_norm[i], w_ple_gate[i], w_ple_proj[i])
        y_prompt = encoder_layer(y_prompt, p_prompt[i], i, *lw)
        y_sample = encoder_layer(y_sample, p_sample[i], i, *lw)
    return (y_prompt, y_sample)
```

```python
import functools
import math

import jax
import jax.numpy as jnp
from jax import lax
from jax.experimental import pallas as pl
from jax.experimental.pallas import tpu as pltpu

F32 = jnp.float32
BF16 = jnp.bfloat16

EPS = 1e-6
LOG2E = 1.4426950408889634
NEG_BIG = -1e30

D_MODEL = 2048
PLE_DIM = 256
GRID_W = 64
HEAD_DIM = 128
NA_HEADS = 6
NA_WIN_ROWS = 8
NA_WIN_COLS = 16
MLA_HEADS = 5
MLA_Q_RANK = 512
MLA_KV_RANK = 256
MLA_NOPE_DIM = 128
MLA_ROPE_DIM = 64
MLA_V_DIM = 128
MLA_ROPE_THETA = 10000.0
DIFF_HEADS = 5
DIFF_QK_DIM = 64
DIFF_V_DIM = 128
ROPE_THETA = 500000.0
ROPE_PART_DIM = DIFF_QK_DIM // 4
D_FF = 5632

NA_W = NA_HEADS * HEAD_DIM
MLA_QK_DIM = MLA_NOPE_DIM + MLA_ROPE_DIM
MLA_SLOT = 256
DIFF_W = DIFF_HEADS * 2 * DIFF_QK_DIM
DIFF_VW = DIFF_HEADS * DIFF_V_DIM
MLA_VW = MLA_HEADS * MLA_V_DIM

C_NAQ = 0
C_NAK = C_NAQ + NA_W
C_NAV = C_NAK + NA_W
C_CQ = C_NAV + NA_W
C_CKV = C_CQ + MLA_Q_RANK
C_DQ = C_CKV + MLA_KV_RANK
C_DK = C_DQ + DIFF_W
C_DV = C_DK + DIFF_W
C_KPE = C_DV + DIFF_VW
IN_COLS_PAD = C_KPE + 128

LANES = 128
VMEM_LIMIT = 48 * 1024 * 1024


def _cparams(sem):
    return pltpu.CompilerParams(dimension_semantics=sem, vmem_limit_bytes=VMEM_LIMIT)


def _rms(x, g, n):
    ms = jnp.sum(x * x, axis=-1, keepdims=True) * (1.0 / n)
    return x * lax.rsqrt(ms + EPS) * g


def _rms_matmul_kernel(x_ref, g_ref, w_ref, o_ref, xn_ref):
    @pl.when(pl.program_id(1) == 0)
    def _():
        xn_ref[...] = _rms(x_ref[...], g_ref[...], x_ref.shape[-1]).astype(BF16)

    o_ref[...] = jnp.dot(xn_ref[...], w_ref[...], preferred_element_type=F32).astype(o_ref.dtype)


def rms_matmul(x, g, w, *, tm, tn, out_dtype):
    M, K = x.shape
    N = w.shape[1]
    return pl.pallas_call(
        _rms_matmul_kernel,
        grid=(M // tm, N // tn),
        in_specs=[pl.BlockSpec((tm, K), lambda i, j: (i, 0)),
                  pl.BlockSpec((1, K), lambda i, j: (0, 0)),
                  pl.BlockSpec((K, tn), lambda i, j: (0, j))],
        out_specs=pl.BlockSpec((tm, tn), lambda i, j: (i, j)),
        out_shape=jax.ShapeDtypeStruct((M, N), out_dtype),
        scratch_shapes=[pltpu.VMEM((tm, K), BF16)],
        compiler_params=_cparams(("parallel", "arbitrary")),
        name="rms_matmul",
    )(x, g, w)


def _prep_kernel(proj_ref, rope_m_ref, rope_d_ref, naq_g, nak_g, qa_g, wqb_ref, kva_g, wkvb_ref,
                 qn_g, qpe_g, kn_g, kpe_g, dq_g, dk_g,
                 naq_o, nak_o, nav_o, mq_o, mkT_o, mv_o, dq_o, dkT_o, dv_o):
    tm = proj_ref.shape[0]
    lane = lax.broadcasted_iota(jnp.int32, (tm, LANES), 1)
    lo = lane < 64

    na_scale = HEAD_DIM ** -0.5 * LOG2E
    for h in range(NA_HEADS):
        sl = slice(h * HEAD_DIM, (h + 1) * HEAD_DIM)
        q = _rms(proj_ref[:, C_NAQ + h * HEAD_DIM:C_NAQ + (h + 1) * HEAD_DIM], naq_g[...], HEAD_DIM)
        naq_o[:, sl] = (q * na_scale).astype(BF16)
        k = _rms(proj_ref[:, C_NAK + h * HEAD_DIM:C_NAK + (h + 1) * HEAD_DIM], nak_g[...], HEAD_DIM)
        nak_o[:, sl] = k.astype(BF16)
    nav_o[...] = proj_ref[:, C_NAV:C_NAV + NA_W].astype(BF16)

    cm, sam, sbm = rope_m_ref[:, 0:128], rope_m_ref[:, 128:256], rope_m_ref[:, 256:384]

    def rope_m(y):
        return y * cm + pltpu.roll(y, 96, 1) * sam + pltpu.roll(y, 32, 1) * sbm

    mla_scale = MLA_QK_DIM ** -0.5 * LOG2E
    cq = _rms(proj_ref[:, C_CQ:C_CQ + MLA_Q_RANK], qa_g[...], MLA_Q_RANK).astype(BF16)
    qm = jnp.dot(cq, wqb_ref[...], preferred_element_type=F32)
    for h in range(MLA_HEADS):
        b = h * MLA_SLOT
        nope = _rms(qm[:, b:b + 128], qn_g[...], MLA_NOPE_DIM)
        pe = rope_m(_rms(qm[:, b + 128:b + 256], qpe_g[...], MLA_ROPE_DIM))
        mq_o[:, b:b + 128] = (nope * mla_scale).astype(BF16)
        mq_o[:, b + 128:b + 256] = (pe * mla_scale).astype(BF16)
    ckv = _rms(proj_ref[:, C_CKV:C_CKV + MLA_KV_RANK], kva_g[...], MLA_KV_RANK).astype(BF16)
    kv = jnp.dot(ckv, wkvb_ref[...], preferred_element_type=F32)
    kpe = rope_m(_rms(proj_ref[:, C_KPE:C_KPE + 128], kpe_g[...], MLA_ROPE_DIM))
    kpe_t = kpe.T.astype(BF16)
    for h in range(MLA_HEADS):
        b = h * MLA_SLOT
        kn = _rms(kv[:, b:b + 128], kn_g[...], MLA_NOPE_DIM)
        mkT_o[b:b + 128, :] = kn.T.astype(BF16)
        mkT_o[b + 128:b + 256, :] = kpe_t
        mv_o[:, h * MLA_V_DIM:(h + 1) * MLA_V_DIM] = kv[:, b + 128:b + 256].astype(BF16)

    cd, sad, sbd = rope_d_ref[:, 0:128], rope_d_ref[:, 128:256], rope_d_ref[:, 256:384]

    def rope_d(y):
        return y * cd + pltpu.roll(y, 120, 1) * sad + pltpu.roll(y, 8, 1) * sbd

    def group_rms(x, g):
        x2 = x * x
        s_lo = jnp.sum(jnp.where(lo, x2, 0.0), axis=-1, keepdims=True)
        s_hi = jnp.sum(jnp.where(lo, 0.0, x2), axis=-1, keepdims=True)
        ms = jnp.where(lo, s_lo, s_hi) * (1.0 / DIFF_QK_DIM)
        return x * lax.rsqrt(ms + EPS) * g

    def split_components(y):
        return jnp.where(lo, y, 0.0), jnp.where(lo, pltpu.roll(y, 64, 1), 0.0)

    df_scale = DIFF_QK_DIM ** -0.5 * LOG2E
    for h in range(DIFF_HEADS):
        q = rope_d(group_rms(proj_ref[:, C_DQ + h * 128:C_DQ + (h + 1) * 128], dq_g[...])) * df_scale
        q0, q1 = split_components(q)
        dq_o[:, (2 * h) * 128:(2 * h + 1) * 128] = q0.astype(BF16)
        dq_o[:, (2 * h + 1) * 128:(2 * h + 2) * 128] = q1.astype(BF16)
        k = rope_d(group_rms(proj_ref[:, C_DK + h * 128:C_DK + (h + 1) * 128], dk_g[...]))
        k0, k1 = split_components(k)
        dkT_o[(2 * h) * 128:(2 * h + 1) * 128, :] = k0.T.astype(BF16)
        dkT_o[(2 * h + 1) * 128:(2 * h + 2) * 128, :] = k1.T.astype(BF16)
    dv_o[...] = proj_ref[:, C_DV:C_DV + DIFF_VW].astype(BF16)


def head_prep(proj, rope_m, rope_d, lw, *, seq_len, tm):
    M = proj.shape[0]
    nt = seq_len // tm
    row = lambda w: pl.BlockSpec((tm, w), lambda i: (i, 0))
    colT = lambda h: pl.BlockSpec((h, tm), lambda i: (0, i))
    full = lambda a: pl.BlockSpec(a.shape, lambda i: (0,) * a.ndim)
    rope = pl.BlockSpec((tm, 384), lambda i: (i % nt, 0))
    params = (lw["naq_g"], lw["nak_g"], lw["qa_g"], lw["wqb"], lw["kva_g"], lw["wkvb"],
              lw["qn_g"], lw["qpe_g"], lw["kn_g"], lw["kpe_g"], lw["dq_g"], lw["dk_g"])
    out_shapes = (
        jax.ShapeDtypeStruct((M, NA_W), BF16), jax.ShapeDtypeStruct((M, NA_W), BF16),
        jax.ShapeDtypeStruct((M, NA_W), BF16),
        jax.ShapeDtypeStruct((M, MLA_HEADS * MLA_SLOT), BF16),
        jax.ShapeDtypeStruct((MLA_HEADS * MLA_SLOT, M), BF16),
        jax.ShapeDtypeStruct((M, MLA_VW), BF16),
        jax.ShapeDtypeStruct((M, 2 * DIFF_HEADS * 128), BF16),
        jax.ShapeDtypeStruct((2 * DIFF_HEADS * 128, M), BF16),
        jax.ShapeDtypeStruct((M, DIFF_VW), BF16),
    )
    out_specs = (row(NA_W), row(NA_W), row(NA_W), row(MLA_HEADS * MLA_SLOT), colT(MLA_HEADS * MLA_SLOT),
                 row(MLA_VW), row(2 * DIFF_HEADS * 128), colT(2 * DIFF_HEADS * 128), row(DIFF_VW))
    return pl.pallas_call(
        _prep_kernel,
        grid=(M // tm,),
        in_specs=[row(IN_COLS_PAD), rope, rope] + [full(a) for a in params],
        out_specs=out_specs,
        out_shape=out_shapes,
        compiler_params=_cparams(("parallel",)),
        name="head_prep",
    )(proj, rope_m, rope_d, *params)


def _flash_kernel(q_ref, kT_ref, v_ref, o_ref, acc_ref, *, tk):
    tq = q_ref.shape[0]
    nk = kT_ref.shape[1] // tk
    q = q_ref[...]
    acc_ref[...] = jnp.zeros_like(acc_ref)

    def body(c, carry):
        m, l = carry
        off = pl.multiple_of(c * tk, tk)
        s = jnp.dot(q, kT_ref[:, pl.ds(off, tk)], preferred_element_type=F32)
        m_new = jnp.maximum(m, jnp.max(s, axis=-1, keepdims=True))
        alpha = jnp.exp2(m - m_new)
        p = jnp.exp2(s - m_new)
        l_new = alpha * l + jnp.sum(p, axis=-1, keepdims=True)
        pv = jnp.dot(p.astype(BF16), v_ref[pl.ds(off, tk), :], preferred_element_type=F32)
        acc_ref[...] = alpha * acc_ref[...] + pv
        return m_new, l_new

    m0 = jnp.full((tq, 1), NEG_BIG, F32)
    l0 = jnp.zeros((tq, 1), F32)
    _, l = lax.fori_loop(0, nk, body, (m0, l0))
    o_ref[...] = (acc_ref[...] / l).astype(o_ref.dtype)


def flash_attention(q, kT, v, *, batch, seq_len, n_heads, dq, v_of_head, tq, tk, out_dtype):
    M = q.shape[0]
    nq = seq_len // tq
    dv = 128
    return pl.pallas_call(
        functools.partial(_flash_kernel, tk=tk),
        grid=(batch, n_heads, nq),
        in_specs=[pl.BlockSpec((tq, dq), lambda b, h, i: (b * nq + i, h)),
                  pl.BlockSpec((dq, seq_len), lambda b, h, i: (h, b)),
                  pl.BlockSpec((seq_len, dv), lambda b, h, i: (b, v_of_head(h)))],
        out_specs=pl.BlockSpec((tq, dv), lambda b, h, i: (b * nq + i, h)),
        out_shape=jax.ShapeDtypeStruct((M, n_heads * dv), out_dtype),
        scratch_shapes=[pltpu.VMEM((tq, dv), F32)],
        compiler_params=_cparams(("parallel", "parallel", "arbitrary")),
        name="flash_attention",
    )(q, kT, v)


def _na_kernel(q_ref, k_ref, v_ref, bias_ref, o_ref, *, rows, rq):
    i = pl.program_id(2)
    win = NA_WIN_ROWS * GRID_W
    for a in range(rq):
        r = i * rq + a
        rs = jnp.clip(r - NA_WIN_ROWS // 2, 0, rows - NA_WIN_ROWS)
        u0 = rs - r + (NA_WIN_ROWS - 1)
        ks = pl.multiple_of(rs * GRID_W, GRID_W)
        q = q_ref[a * GRID_W:(a + 1) * GRID_W, :]
        k = k_ref[pl.ds(ks, win), :]
        s = lax.dot_general(q, k, (((1,), (1,)), ((), ())), preferred_element_type=F32)
        s = s + bias_ref[u0]
        m = jnp.max(s, axis=-1, keepdims=True)
        p = jnp.exp2(s - m)
        l = jnp.sum(p, axis=-1, keepdims=True)
        o = jnp.dot(p.astype(BF16), v_ref[pl.ds(ks, win), :], preferred_element_type=F32)
        o_ref[a * GRID_W:(a + 1) * GRID_W, :] = (o / l).astype(o_ref.dtype)


def na_attention(q, k, v, bias, *, batch, seq_len, rq):
    M = q.shape[0]
    rows = seq_len // GRID_W
    assert rows >= NA_WIN_ROWS and rows % rq == 0
    nblk = rows // rq
    tq = rq * GRID_W
    return pl.pallas_call(
        functools.partial(_na_kernel, rows=rows, rq=rq),
        grid=(batch, NA_HEADS, nblk),
        in_specs=[pl.BlockSpec((tq, HEAD_DIM), lambda b, h, i: (b * nblk + i, h)),
                  pl.BlockSpec((seq_len, HEAD_DIM), lambda b, h, i: (b, h)),
                  pl.BlockSpec((seq_len, HEAD_DIM), lambda b, h, i: (b, h)),
                  pl.BlockSpec((None, NA_WIN_ROWS, GRID_W, NA_WIN_ROWS * GRID_W), lambda b, h, i: (h, 0, 0, 0))],
        out_specs=pl.BlockSpec((tq, HEAD_DIM), lambda b, h, i: (b * nblk + i, h)),
        out_shape=jax.ShapeDtypeStruct((M, NA_W), BF16),
        compiler_params=_cparams(("parallel", "parallel", "arbitrary")),
        name="na_attention",
    )(q, k, v, bias)


def _out_proj_kernel(x_ref, ona_ref, omla_ref, odf_ref, subln_ref, lq1, lk1, lq2, lk2,
                     w_na, w_mla, w_df, o_ref, *, lam_init):
    lam = (jnp.exp(jnp.sum(lq1[...] * lk1[...], axis=-1, keepdims=True))
           - jnp.exp(jnp.sum(lq2[...] * lk2[...], axis=-1, keepdims=True)) + lam_init)
    heads = []
    for h in range(DIFF_HEADS):
        o1 = odf_ref[:, (2 * h) * 128:(2 * h + 1) * 128]
        o2 = odf_ref[:, (2 * h + 1) * 128:(2 * h + 2) * 128]
        o = _rms(o1 - lam * o2, subln_ref[...], DIFF_V_DIM) * (1.0 - lam_init)
        heads.append(o.astype(BF16))
    odf = jnp.concatenate(heads, axis=-1)
    y = jnp.dot(ona_ref[...], w_na[...], preferred_element_type=F32)
    y = y + jnp.dot(omla_ref[...], w_mla[...], preferred_element_type=F32)
    y = y + jnp.dot(odf, w_df[...], preferred_element_type=F32)
    o_ref[...] = x_ref[...] + y


def out_proj(x, o_na, o_mla, o_df, lw, *, lam_init, tm):
    M, D = x.shape
    row = lambda w: pl.BlockSpec((tm, w), lambda i: (i, 0))
    full = lambda a: pl.BlockSpec(a.shape, lambda i: (0,) * a.ndim)
    params = (lw["subln_g"], lw["lq1"], lw["lk1"], lw["lq2"], lw["lk2"], lw["w_out_na"], lw["w_out_mla"], lw["w_out_df"])
    return pl.pallas_call(
        functools.partial(_out_proj_kernel, lam_init=lam_init),
        grid=(M // tm,),
        in_specs=[row(D), row(NA_W), row(MLA_VW), row(2 * DIFF_HEADS * 128)] + [full(a) for a in params],
        out_specs=row(D),
        out_shape=jax.ShapeDtypeStruct((M, D), F32),
        compiler_params=_cparams(("parallel",)),
        name="out_proj",
    )(x, o_na, o_mla, o_df, *params)


HALO = 16


def _ffn_kernel(x_ref, xp_ref, xn_ref, g_ref, wg_ref, wu_ref, cw_ref, cb_ref, wd_ref, o_ref,
                xs_ref, acc_ref, *, seq_len):
    i = pl.program_id(0)
    f = pl.program_id(1)
    tm = x_ref.shape[0]

    @pl.when(f == 0)
    def _():
        d = x_ref.shape[-1]
        row0 = i * tm
        at_start = (row0 % seq_len) == 0
        at_end = ((row0 + tm) % seq_len) == 0
        prev = _rms(xp_ref[...], g_ref[...], d)
        nxt = _rms(xn_ref[...], g_ref[...], d)
        xs_ref[0:HALO, :] = jnp.where(at_start, 0.0, prev).astype(BF16)
        xs_ref[HALO:HALO + tm, :] = _rms(x_ref[...], g_ref[...], d).astype(BF16)
        xs_ref[HALO + tm:2 * HALO + tm, :] = jnp.where(at_end, 0.0, nxt).astype(BF16)
        acc_ref[...] = jnp.zeros_like(acc_ref)

    n_ext = tm + 2 * HALO
    gate = jnp.dot(xs_ref[...], wg_ref[...], preferred_element_type=F32)
    g_prev = pltpu.roll(gate, 1, 0)[HALO:HALO + tm, :]
    g_next = pltpu.roll(gate, n_ext - 1, 0)[HALO:HALO + tm, :]
    g_mid = gate[HALO:HALO + tm, :]
    g = g_prev * cw_ref[0:1, :] + g_mid * cw_ref[1:2, :] + g_next * cw_ref[2:3, :] + cb_ref[...]
    u = jnp.dot(xs_ref[HALO:HALO + tm, :], wu_ref[...], preferred_element_type=F32)
    a = (g * jax.nn.sigmoid(g) * u).astype(BF16)
    acc_ref[...] += jnp.dot(a, wd_ref[...], preferred_element_type=F32)

    @pl.when(f == pl.num_programs(1) - 1)
    def _():
        o_ref[...] = x_ref[...] + acc_ref[...]


def ffn(x, lw, *, seq_len, tm, tf):
    M, D = x.shape
    F = lw["w_gate"].shape[1]
    assert seq_len % tm == 0 and tm % HALO == 0
    hb = tm // HALO
    last = M // HALO - 1
    return pl.pallas_call(
        functools.partial(_ffn_kernel, seq_len=seq_len),
        grid=(M // tm, F // tf),
        in_specs=[pl.BlockSpec((tm, D), lambda i, f: (i, 0)),
                  pl.BlockSpec((HALO, D), lambda i, f: (jnp.maximum(i * hb - 1, 0), 0)),
                  pl.BlockSpec((HALO, D), lambda i, f: (jnp.minimum((i + 1) * hb, last), 0)),
                  pl.BlockSpec((1, D), lambda i, f: (0, 0)),
                  pl.BlockSpec((D, tf), lambda i, f: (0, f)),
                  pl.BlockSpec((D, tf), lambda i, f: (0, f)),
                  pl.BlockSpec((3, tf), lambda i, f: (0, f)),
                  pl.BlockSpec((1, tf), lambda i, f: (0, f)),
                  pl.BlockSpec((tf, D), lambda i, f: (f, 0))],
        out_specs=pl.BlockSpec((tm, D), lambda i, f: (i, 0)),
        out_shape=jax.ShapeDtypeStruct((M, D), F32),
        scratch_shapes=[pltpu.VMEM((tm + 2 * HALO, D), BF16), pltpu.VMEM((tm, D), F32)],
        compiler_params=_cparams(("parallel", "arbitrary")),
        name="ffn",
    )(x, x, x, lw["ffn_g"], lw["w_gate"], lw["w_up"], lw["conv_w"], lw["conv_b"], lw["w_down"])


def _ple_kernel(x_ref, p_ref, g_ref, wg_ref, wp_ref, o_ref):
    x = x_ref[...]
    xn = _rms(x, g_ref[...], x.shape[-1]).astype(BF16)
    gate = jax.nn.sigmoid(jnp.dot(xn, wg_ref[...], preferred_element_type=F32))
    proj = jnp.dot(p_ref[...].astype(BF16), wp_ref[...], preferred_element_type=F32)
    o_ref[...] = x + gate * proj


def ple(x, p, lw, *, tm):
    M, D = x.shape
    row = lambda w: pl.BlockSpec((tm, w), lambda i: (i, 0))
    full = lambda a: pl.BlockSpec(a.shape, lambda i: (0,) * a.ndim)
    params = (lw["ple_g"], lw["w_ple_gate"], lw["w_ple_proj"])
    return pl.pallas_call(
        _ple_kernel,
        grid=(M // tm,),
        in_specs=[row(D), row(p.shape[1])] + [full(a) for a in params],
        out_specs=row(D),
        out_shape=jax.ShapeDtypeStruct((M, D), F32),
        compiler_params=_cparams(("parallel",)),
        name="ple",
    )(x, p, *params)


def _rope_tables(seq_len):
    def angles(dim, theta):
        inv = 1.0 / (theta ** (jnp.arange(0, dim, 2, dtype=F32) / dim))
        ang = jnp.arange(seq_len, dtype=F32)[:, None] * inv[None, :]
        return jnp.cos(ang), jnp.sin(ang)

    z = lambda w: jnp.zeros((seq_len, w), F32)
    cos, sin = angles(MLA_ROPE_DIM, MLA_ROPE_THETA)
    half = MLA_ROPE_DIM // 2
    cm = jnp.concatenate([cos, cos, z(64)], axis=1)
    sam = jnp.concatenate([-sin, z(half), z(64)], axis=1)
    sbm = jnp.concatenate([z(half), sin, z(64)], axis=1)
    rope_m = jnp.concatenate([cm, sam, sbm], axis=1)

    cos, sin = angles(ROPE_PART_DIM, ROPE_THETA)
    half = ROPE_PART_DIM // 2
    rest = DIFF_QK_DIM - ROPE_PART_DIM
    c64 = jnp.concatenate([cos, cos, jnp.ones((seq_len, rest), F32)], axis=1)
    sa64 = jnp.concatenate([-sin, z(half), z(rest)], axis=1)
    sb64 = jnp.concatenate([z(half), sin, z(rest)], axis=1)
    rope_d = jnp.concatenate([c64, c64, sa64, sa64, sb64, sb64], axis=1)
    return rope_m, rope_d


def _na_bias_table(rpb):
    cols = jnp.arange(GRID_W)
    cs = jnp.clip(cols - NA_WIN_COLS // 2, 0, GRID_W - NA_WIN_COLS)
    kc = jnp.arange(GRID_W)
    valid = (kc[None, :] >= cs[:, None]) & (kc[None, :] < cs[:, None] + NA_WIN_COLS)
    cidx = jnp.clip(kc[None, :] - cols[:, None] + (NA_WIN_COLS - 1), 0, 2 * NA_WIN_COLS - 2)
    u = jnp.arange(NA_WIN_ROWS)[:, None] + jnp.arange(NA_WIN_ROWS)[None, :]
    tab = rpb[:, u[:, :, None, None], cidx[None, None, :, :]]
    tab = jnp.where(valid[None, None, None], tab * LOG2E, NEG_BIG)
    return tab.transpose(0, 1, 3, 2, 4).reshape(rpb.shape[0], NA_WIN_ROWS, GRID_W, NA_WIN_ROWS * GRID_W)


def _layer_params(i, norm_mix, w_in, na_q_norm, na_k_norm, na_rpb, mla_q_a_norm, mla_w_q_b, mla_kv_a_norm,
                  mla_w_kv_b, mla_q_nope_norm, mla_q_pe_norm, mla_k_nope_norm, mla_k_pe_norm,
                  diff_q_norm, diff_k_norm, diff_lambda_q1, diff_lambda_k1, diff_lambda_q2, diff_lambda_k2,
                  diff_subln, w_out, norm_ffn, w_gate, w_up, conv_w, conv_b, w_down,
                  ple_norm, w_ple_gate, w_ple_proj):
    r = lambda a: a[i].reshape(1, -1).astype(F32)
    pad_to = lambda a, n: jnp.pad(a, ((0, 0), (0, n - a.shape[1])))
    w = w_in[i]
    kpe0 = 3 * NA_W + MLA_Q_RANK + MLA_KV_RANK
    w_perm = jnp.concatenate([w[:, :kpe0], w[:, kpe0 + MLA_ROPE_DIM:], w[:, kpe0:kpe0 + MLA_ROPE_DIM],
                              jnp.zeros((w.shape[0], 128 - MLA_ROPE_DIM), w.dtype)], axis=1).astype(BF16)
    wqb = mla_w_q_b[i].reshape(MLA_Q_RANK, MLA_HEADS, MLA_QK_DIM)
    wqb = jnp.pad(wqb, ((0, 0), (0, 0), (0, MLA_SLOT - MLA_QK_DIM))).reshape(MLA_Q_RANK, MLA_HEADS * MLA_SLOT)
    wo = w_out[i].astype(BF16)
    return dict(
        mix_g=r(norm_mix), w_in=w_perm,
        naq_g=r(na_q_norm), nak_g=r(na_k_norm), na_bias=_na_bias_table(na_rpb[i]),
        qa_g=r(mla_q_a_norm), wqb=wqb.astype(BF16), kva_g=r(mla_kv_a_norm), wkvb=mla_w_kv_b[i].astype(BF16),
        qn_g=r(mla_q_nope_norm), qpe_g=pad_to(r(mla_q_pe_norm), 128),
        kn_g=r(mla_k_nope_norm), kpe_g=pad_to(r(mla_k_pe_norm), 128),
        dq_g=jnp.tile(r(diff_q_norm), (1, 2)), dk_g=jnp.tile(r(diff_k_norm), (1, 2)),
        lq1=r(diff_lambda_q1), lk1=r(diff_lambda_k1), lq2=r(diff_lambda_q2), lk2=r(diff_lambda_k2),
        subln_g=r(diff_subln),
        w_out_na=wo[:NA_W], w_out_mla=wo[NA_W:NA_W + MLA_VW], w_out_df=wo[NA_W + MLA_VW:],
        ffn_g=r(norm_ffn), w_gate=w_gate[i].astype(BF16), w_up=w_up[i].astype(BF16),
        conv_w=conv_w[i].astype(F32), conv_b=r(conv_b), w_down=w_down[i].astype(BF16),
        ple_g=r(ple_norm), w_ple_gate=w_ple_gate[i].astype(BF16), w_ple_proj=w_ple_proj[i].astype(BF16),
    )


def _tile(n, pref):
    t = min(pref, n)
    while n % t:
        t //= 2
    return t


def _encoder_layer(x, p_l, lw, rope_m, rope_d, *, layer_idx, batch, seq_len):
    tm = _tile(seq_len, 512)
    proj = rms_matmul(x, lw["mix_g"], lw["w_in"], tm=tm, tn=1024, out_dtype=F32)
    (na_q, na_k, na_v, m_q, m_kT, m_v, d_q, d_kT, d_v) = head_prep(
        proj, rope_m, rope_d, lw, seq_len=seq_len, tm=_tile(seq_len, 256))
    o_na = na_attention(na_q, na_k, na_v, lw["na_bias"], batch=batch, seq_len=seq_len,
                        rq=_tile(seq_len // GRID_W, 8))
    tq = _tile(seq_len, 512)
    tk = _tile(seq_len, 512)
    o_mla = flash_attention(m_q, m_kT, m_v, batch=batch, seq_len=seq_len, n_heads=MLA_HEADS, dq=MLA_SLOT,
                            v_of_head=lambda h: h, tq=tq, tk=tk, out_dtype=BF16)
    o_df = flash_attention(d_q, d_kT, d_v, batch=batch, seq_len=seq_len, n_heads=2 * DIFF_HEADS, dq=128,
                           v_of_head=lambda h: h // 2, tq=tq, tk=tk, out_dtype=F32)
    lam_init = 0.8 - 0.6 * math.exp(-0.3 * layer_idx)
    x = out_proj(x, o_na, o_mla, o_df, lw, lam_init=lam_init, tm=tm)
    x = ffn(x, lw, seq_len=seq_len, tm=tm, tf=512)
    x = ple(x, p_l, lw, tm=tm)
    return x


def kernel(x_prompt, x_sample, p_prompt, p_sample, norm_mix, w_in, na_q_norm, na_k_norm, na_rpb, mla_q_a_norm, mla_w_q_b, mla_kv_a_norm, mla_w_kv_b, mla_q_nope_norm, mla_q_pe_norm, mla_k_nope_norm, mla_k_pe_norm, diff_q_norm, diff_k_norm, diff_lambda_q1, diff_lambda_k1, diff_lambda_q2, diff_lambda_k2, diff_subln, w_out, norm_ffn, w_gate, w_up, conv_w, conv_b, w_down, ple_norm, w_ple_gate, w_ple_proj):
    weights = (norm_mix, w_in, na_q_norm, na_k_norm, na_rpb, mla_q_a_norm, mla_w_q_b, mla_kv_a_norm, mla_w_kv_b,
               mla_q_nope_norm, mla_q_pe_norm, mla_k_nope_norm, mla_k_pe_norm, diff_q_norm, diff_k_norm,
               diff_lambda_q1, diff_lambda_k1, diff_lambda_q2, diff_lambda_k2, diff_subln, w_out,
               norm_ffn, w_gate, w_up, conv_w, conv_b, w_down, ple_norm, w_ple_gate, w_ple_proj)
    depth = norm_mix.shape[0]
    groups = []
    for x, p in ((x_prompt, p_prompt), (x_sample, p_sample)):
        b, t, d = x.shape
        groups.append(dict(x=x.reshape(b * t, d), p=p.reshape(depth, b * t, p.shape[-1]), batch=b, seq_len=t,
                           rope=_rope_tables(t), shape=x.shape))
    for i in range(depth):
        lw = _layer_params(i, *weights)
        for g in groups:
            g["x"] = _encoder_layer(g["x"], g["p"][i], lw, *g["rope"], layer_idx=i,
                                    batch=g["batch"], seq_len=g["seq_len"])
    return tuple(g["x"].reshape(g["shape"]) for g in groups)
```

```python
import functools
import math

import jax
import jax.numpy as jnp
from jax import lax
from jax.experimental import pallas as pl
from jax.experimental.pallas import tpu as pltpu

F32 = jnp.float32
BF16 = jnp.bfloat16

EPS = 1e-6
LOG2E = 1.4426950408889634
NEG_BIG = -1e30

D_MODEL = 2048
PLE_DIM = 256
GRID_W = 64
HEAD_DIM = 128
NA_HEADS = 6
NA_WIN_ROWS = 8
NA_WIN_COLS = 16
MLA_HEADS = 5
MLA_Q_RANK = 512
MLA_KV_RANK = 256
MLA_NOPE_DIM = 128
MLA_ROPE_DIM = 64
MLA_V_DIM = 128
MLA_ROPE_THETA = 10000.0
DIFF_HEADS = 5
DIFF_QK_DIM = 64
DIFF_V_DIM = 128
ROPE_THETA = 500000.0
ROPE_PART_DIM = DIFF_QK_DIM // 4
D_FF = 5632

NA_W = NA_HEADS * HEAD_DIM
MLA_QK_DIM = MLA_NOPE_DIM + MLA_ROPE_DIM
MLA_SLOT = 256
DIFF_W = DIFF_HEADS * 2 * DIFF_QK_DIM
DIFF_VW = DIFF_HEADS * DIFF_V_DIM
MLA_VW = MLA_HEADS * MLA_V_DIM

C_NAQ = 0
C_NAK = C_NAQ + NA_W
C_NAV = C_NAK + NA_W
C_CQ = C_NAV + NA_W
C_CKV = C_CQ + MLA_Q_RANK
C_DQ = C_CKV + MLA_KV_RANK
C_DK = C_DQ + DIFF_W
C_DV = C_DK + DIFF_W
C_KPE = C_DV + DIFF_VW
IN_COLS_PAD = C_KPE + 128

LANES = 128
VMEM_LIMIT = 48 * 1024 * 1024


def _cparams(sem):
    return pltpu.CompilerParams(dimension_semantics=sem, vmem_limit_bytes=VMEM_LIMIT)


def _rms(x, g, n):
    ms = jnp.sum(x * x, axis=-1, keepdims=True) * (1.0 / n)
    return x * lax.rsqrt(ms + EPS) * g


def _rms_matmul_kernel(x_ref, g_ref, w_ref, o_ref, xn_ref):
    @pl.when(pl.program_id(1) == 0)
    def _():
        xn_ref[...] = _rms(x_ref[...], g_ref[...], x_ref.shape[-1]).astype(BF16)

    o_ref[...] = jnp.dot(xn_ref[...], w_ref[...], preferred_element_type=F32).astype(o_ref.dtype)


def rms_matmul(x, g, w, *, tm, tn, out_dtype):
    M, K = x.shape
    N = w.shape[1]
    return pl.pallas_call(
        _rms_matmul_kernel,
        grid=(M // tm, N // tn),
        in_specs=[pl.BlockSpec((tm, K), lambda i, j: (i, 0)),
                  pl.BlockSpec((1, K), lambda i, j: (0, 0)),
                  pl.BlockSpec((K, tn), lambda i, j: (0, j))],
        out_specs=pl.BlockSpec((tm, tn), lambda i, j: (i, j)),
        out_shape=jax.ShapeDtypeStruct((M, N), out_dtype),
        scratch_shapes=[pltpu.VMEM((tm, K), BF16)],
        compiler_params=_cparams(("parallel", "arbitrary")),
        name="rms_matmul",
    )(x, g, w)


def _prep_kernel(proj_ref, rope_m_ref, rope_d_ref, naq_g, nak_g, qa_g, wqb_ref, kva_g, wkvb_ref,
                 qn_g, qpe_g, kn_g, kpe_g, dq_g, dk_g,
                 naq_o, nak_o, nav_o, mq_o, mkT_o, mv_o, dq_o, dkT_o, dv_o):
    tm = proj_ref.shape[0]
    lane = lax.broadcasted_iota(jnp.int32, (tm, LANES), 1)
    lo = lane < 64

    na_scale = HEAD_DIM ** -0.5 * LOG2E
    for h in range(NA_HEADS):
        sl = slice(h * HEAD_DIM, (h + 1) * HEAD_DIM)
        q = _rms(proj_ref[:, C_NAQ + h * HEAD_DIM:C_NAQ + (h + 1) * HEAD_DIM], naq_g[...], HEAD_DIM)
        naq_o[:, sl] = (q * na_scale).astype(BF16)
        k = _rms(proj_ref[:, C_NAK + h * HEAD_DIM:C_NAK + (h + 1) * HEAD_DIM], nak_g[...], HEAD_DIM)
        nak_o[:, sl] = k.astype(BF16)
    nav_o[...] = proj_ref[:, C_NAV:C_NAV + NA_W].astype(BF16)

    cm, sam, sbm = rope_m_ref[:, 0:128], rope_m_ref[:, 128:256], rope_m_ref[:, 256:384]

    def rope_m(y):
        return y * cm + pltpu.roll(y, 96, 1) * sam + pltpu.roll(y, 32, 1) * sbm

    mla_scale = MLA_QK_DIM ** -0.5 * LOG2E
    cq = _rms(proj_ref[:, C_CQ:C_CQ + MLA_Q_RANK], qa_g[...], MLA_Q_RANK).astype(BF16)
    qm = jnp.dot(cq, wqb_ref[...], preferred_element_type=F32)
    for h in range(MLA_HEADS):
        b = h * MLA_SLOT
        nope = _rms(qm[:, b:b + 128], qn_g[...], MLA_NOPE_DIM)
        pe = rope_m(_rms(qm[:, b + 128:b + 256], qpe_g[...], MLA_ROPE_DIM))
        mq_o[:, b:b + 128] = (nope * mla_scale).astype(BF16)
        mq_o[:, b + 128:b + 256] = (pe * mla_scale).astype(BF16)
    ckv = _rms(proj_ref[:, C_CKV:C_CKV + MLA_KV_RANK], kva_g[...], MLA_KV_RANK).astype(BF16)
    kv = jnp.dot(ckv, wkvb_ref[...], preferred_element_type=F32)
    kpe = rope_m(_rms(proj_ref[:, C_KPE:C_KPE + 128], kpe_g[...], MLA_ROPE_DIM))
    kpe_t = kpe.T.astype(BF16)
    for h in range(MLA_HEADS):
        b = h * MLA_SLOT
        kn = _rms(kv[:, b:b + 128], kn_g[...], MLA_NOPE_DIM)
        mkT_o[b:b + 128, :] = kn.T.astype(BF16)
        mkT_o[b + 128:b + 256, :] = kpe_t
        mv_o[:, h * MLA_V_DIM:(h + 1) * MLA_V_DIM] = kv[:, b + 128:b + 256].astype(BF16)

    cd, sad, sbd = rope_d_ref[:, 0:128], rope_d_ref[:, 128:256], rope_d_ref[:, 256:384]

    def rope_d(y):
        return y * cd + pltpu.roll(y, 120, 1) * sad + pltpu.roll(y, 8, 1) * sbd

    def group_rms(x, g):
        x2 = x * x
        s_lo = jnp.sum(jnp.where(lo, x2, 0.0), axis=-1, keepdims=True)
        s_hi = jnp.sum(jnp.where(lo, 0.0, x2), axis=-1, keepdims=True)
        ms = jnp.where(lo, s_lo, s_hi) * (1.0 / DIFF_QK_DIM)
        return x * lax.rsqrt(ms + EPS) * g

    def split_components(y):
        return jnp.where(lo, y, 0.0), jnp.where(lo, pltpu.roll(y, 64, 1), 0.0)

    df_scale = DIFF_QK_DIM ** -0.5 * LOG2E
    for h in range(DIFF_HEADS):
        q = rope_d(group_rms(proj_ref[:, C_DQ + h * 128:C_DQ + (h + 1) * 128], dq_g[...])) * df_scale
        q0, q1 = split_components(q)
        dq_o[:, (2 * h) * 128:(2 * h + 1) * 128] = q0.astype(BF16)
        dq_o[:, (2 * h + 1) * 128:(2 * h + 2) * 128] = q1.astype(BF16)
        k = rope_d(group_rms(proj_ref[:, C_DK + h * 128:C_DK + (h + 1) * 128], dk_g[...]))
        k0, k1 = split_components(k)
        dkT_o[(2 * h) * 128:(2 * h + 1) * 128, :] = k0.T.astype(BF16)
        dkT_o[(2 * h + 1) * 128:(2 * h + 2) * 128, :] = k1.T.astype(BF16)
    dv_o[...] = proj_ref[:, C_DV:C_DV + DIFF_VW].astype(BF16)


def head_prep(proj, rope_m, rope_d, lw, *, seq_len, tm):
    M = proj.shape[0]
    nt = seq_len // tm
    row = lambda w: pl.BlockSpec((tm, w), lambda i: (i, 0))
    colT = lambda h: pl.BlockSpec((h, tm), lambda i: (0, i))
    full = lambda a: pl.BlockSpec(a.shape, lambda i: (0,) * a.ndim)
    rope = pl.BlockSpec((tm, 384), lambda i: (i % nt, 0))
    params = (lw["naq_g"], lw["nak_g"], lw["qa_g"], lw["wqb"], lw["kva_g"], lw["wkvb"],
              lw["qn_g"], lw["qpe_g"], lw["kn_g"], lw["kpe_g"], lw["dq_g"], lw["dk_g"])
    out_shapes = (
        jax.ShapeDtypeStruct((M, NA_W), BF16), jax.ShapeDtypeStruct((M, NA_W), BF16),
        jax.ShapeDtypeStruct((M, NA_W), BF16),
        jax.ShapeDtypeStruct((M, MLA_HEADS * MLA_SLOT), BF16),
        jax.ShapeDtypeStruct((MLA_HEADS * MLA_SLOT, M), BF16),
        jax.ShapeDtypeStruct((M, MLA_VW), BF16),
        jax.ShapeDtypeStruct((M, 2 * DIFF_HEADS * 128), BF16),
        jax.ShapeDtypeStruct((2 * DIFF_HEADS * 128, M), BF16),
        jax.ShapeDtypeStruct((M, DIFF_VW), BF16),
    )
    out_specs = (row(NA_W), row(NA_W), row(NA_W), row(MLA_HEADS * MLA_SLOT), colT(MLA_HEADS * MLA_SLOT),
                 row(MLA_VW), row(2 * DIFF_HEADS * 128), colT(2 * DIFF_HEADS * 128), row(DIFF_VW))
    return pl.pallas_call(
        _prep_kernel,
        grid=(M // tm,),
        in_specs=[row(IN_COLS_PAD), rope, rope] + [full(a) for a in params],
        out_specs=out_specs,
        out_shape=out_shapes,
        compiler_params=_cparams(("parallel",)),
        name="head_prep",
    )(proj, rope_m, rope_d, *params)


def _flash_kernel(q_ref, kT_ref, v_ref, o_ref, sa_ref, sb_ref, acc_ref, *, tk):
    tq = q_ref.shape[0]
    nk = kT_ref.shape[1] // tk
    q = q_ref[...]

    def scores(c):
        off = pl.multiple_of(c * tk, tk)
        return jnp.dot(q, kT_ref[:, pl.ds(off, tk)], preferred_element_type=F32)

    def softmax_pv(s_ref, c, m, l):
        s = s_ref[...]
        m_new = jnp.maximum(m, jnp.max(s, axis=-1, keepdims=True))
        alpha = jnp.exp2(m - m_new)
        p = jnp.exp2(s - m_new)
        psum = p[:, 0:LANES]
        for t in range(1, tk // LANES):
            psum = psum + p[:, t * LANES:(t + 1) * LANES]
        off = pl.multiple_of(c * tk, tk)
        pv = jnp.dot(p.astype(BF16), v_ref[pl.ds(off, tk), :], preferred_element_type=F32)
        acc_ref[...] = alpha * acc_ref[...] + pv
        return m_new, alpha * l + psum

    def pair(c, m, l, prefetch):
        sb_ref[...] = scores(c + 1)
        m, l = softmax_pv(sa_ref, c, m, l)
        if prefetch:
            sa_ref[...] = scores(c + 2)
        return softmax_pv(sb_ref, c + 1, m, l)

    acc_ref[...] = jnp.zeros_like(acc_ref)
    sa_ref[...] = scores(0)
    m = jnp.full((tq, 1), NEG_BIG, F32)
    l = jnp.zeros((tq, LANES), F32)
    m, l = lax.fori_loop(0, nk // 2 - 1, lambda i, ml: pair(2 * i, *ml, True), (m, l))
    m, l = pair(nk - 2, m, l, False)
    o_ref[...] = (acc_ref[...] / jnp.sum(l, axis=-1, keepdims=True)).astype(o_ref.dtype)


def flash_attention(q, kT, v, *, batch, seq_len, n_heads, dq, v_of_head, tq, tk, out_dtype):
    M = q.shape[0]
    nq = seq_len // tq
    dv = 128
    assert seq_len % (2 * tk) == 0
    return pl.pallas_call(
        functools.partial(_flash_kernel, tk=tk),
        grid=(batch, n_heads, nq),
        in_specs=[pl.BlockSpec((tq, dq), lambda b, h, i: (b * nq + i, h)),
                  pl.BlockSpec((dq, seq_len), lambda b, h, i: (h, b)),
                  pl.BlockSpec((seq_len, dv), lambda b, h, i: (b, v_of_head(h)))],
        out_specs=pl.BlockSpec((tq, dv), lambda b, h, i: (b * nq + i, h)),
        out_shape=jax.ShapeDtypeStruct((M, n_heads * dv), out_dtype),
        scratch_shapes=[pltpu.VMEM((tq, tk), F32), pltpu.VMEM((tq, tk), F32), pltpu.VMEM((tq, dv), F32)],
        compiler_params=_cparams(("parallel", "parallel", "arbitrary")),
        name="flash_attention",
    )(q, kT, v)


def _na_kernel(q_ref, k_ref, v_ref, bias_ref, o_ref, *, rows, rq):
    i = pl.program_id(2)
    win = NA_WIN_ROWS * GRID_W
    for a in range(rq):
        r = i * rq + a
        rs = jnp.clip(r - NA_WIN_ROWS // 2, 0, rows - NA_WIN_ROWS)
        u0 = rs - r + (NA_WIN_ROWS - 1)
        ks = pl.multiple_of(rs * GRID_W, GRID_W)
        q = q_ref[a * GRID_W:(a + 1) * GRID_W, :]
        k = k_ref[pl.ds(ks, win), :]
        s = lax.dot_general(q, k, (((1,), (1,)), ((), ())), preferred_element_type=F32)
        s = s + bias_ref[u0]
        m = jnp.max(s, axis=-1, keepdims=True)
        p = jnp.exp2(s - m)
        l = jnp.sum(p, axis=-1, keepdims=True)
        o = jnp.dot(p.astype(BF16), v_ref[pl.ds(ks, win), :], preferred_element_type=F32)
        o_ref[a * GRID_W:(a + 1) * GRID_W, :] = (o / l).astype(o_ref.dtype)


def na_attention(q, k, v, bias, *, batch, seq_len, rq):
    M = q.shape[0]
    rows = seq_len // GRID_W
    assert rows >= NA_WIN_ROWS and rows % rq == 0
    nblk = rows // rq
    tq = rq * GRID_W
    return pl.pallas_call(
        functools.partial(_na_kernel, rows=rows, rq=rq),
        grid=(batch, NA_HEADS, nblk),
        in_specs=[pl.BlockSpec((tq, HEAD_DIM), lambda b, h, i: (b * nblk + i, h)),
                  pl.BlockSpec((seq_len, HEAD_DIM), lambda b, h, i: (b, h)),
                  pl.BlockSpec((seq_len, HEAD_DIM), lambda b, h, i: (b, h)),
                  pl.BlockSpec((None, NA_WIN_ROWS, GRID_W, NA_WIN_ROWS * GRID_W), lambda b, h, i: (h, 0, 0, 0))],
        out_specs=pl.BlockSpec((tq, HEAD_DIM), lambda b, h, i: (b * nblk + i, h)),
        out_shape=jax.ShapeDtypeStruct((M, NA_W), BF16),
        compiler_params=_cparams(("parallel", "parallel", "arbitrary")),
        name="na_attention",
    )(q, k, v, bias)


def _out_proj_kernel(x_ref, ona_ref, omla_ref, odf_ref, subln_ref, lq1, lk1, lq2, lk2,
                     w_na, w_mla, w_df, o_ref, *, lam_init):
    lam = (jnp.exp(jnp.sum(lq1[...] * lk1[...], axis=-1, keepdims=True))
           - jnp.exp(jnp.sum(lq2[...] * lk2[...], axis=-1, keepdims=True)) + lam_init)
    heads = []
    for h in range(DIFF_HEADS):
        o1 = odf_ref[:, (2 * h) * 128:(2 * h + 1) * 128]
        o2 = odf_ref[:, (2 * h + 1) * 128:(2 * h + 2) * 128]
        o = _rms(o1 - lam * o2, subln_ref[...], DIFF_V_DIM) * (1.0 - lam_init)
        heads.append(o.astype(BF16))
    odf = jnp.concatenate(heads, axis=-1)
    y = jnp.dot(ona_ref[...], w_na[...], preferred_element_type=F32)
    y = y + jnp.dot(omla_ref[...], w_mla[...], preferred_element_type=F32)
    y = y + jnp.dot(odf, w_df[...], preferred_element_type=F32)
    o_ref[...] = x_ref[...] + y


def out_proj(x, o_na, o_mla, o_df, lw, *, lam_init, tm):
    M, D = x.shape
    row = lambda w: pl.BlockSpec((tm, w), lambda i: (i, 0))
    full = lambda a: pl.BlockSpec(a.shape, lambda i: (0,) * a.ndim)
    params = (lw["subln_g"], lw["lq1"], lw["lk1"], lw["lq2"], lw["lk2"], lw["w_out_na"], lw["w_out_mla"], lw["w_out_df"])
    return pl.pallas_call(
        functools.partial(_out_proj_kernel, lam_init=lam_init),
        grid=(M // tm,),
        in_specs=[row(D), row(NA_W), row(MLA_VW), row(2 * DIFF_HEADS * 128)] + [full(a) for a in params],
        out_specs=row(D),
        out_shape=jax.ShapeDtypeStruct((M, D), F32),
        compiler_params=_cparams(("parallel",)),
        name="out_proj",
    )(x, o_na, o_mla, o_df, *params)


HALO = 16


def _ffn_kernel(x_ref, xp_ref, xn_ref, g_ref, wg_ref, wu_ref, cw_ref, cb_ref, wd_ref, o_ref,
                xs_ref, acc_ref, *, seq_len):
    i = pl.program_id(0)
    f = pl.program_id(1)
    tm = x_ref.shape[0]

    @pl.when(f == 0)
    def _():
        d = x_ref.shape[-1]
        row0 = i * tm
        at_start = (row0 % seq_len) == 0
        at_end = ((row0 + tm) % seq_len) == 0
        prev = _rms(xp_ref[...], g_ref[...], d)
        nxt = _rms(xn_ref[...], g_ref[...], d)
        xs_ref[0:HALO, :] = jnp.where(at_start, 0.0, prev).astype(BF16)
        xs_ref[HALO:HALO + tm, :] = _rms(x_ref[...], g_ref[...], d).astype(BF16)
        xs_ref[HALO + tm:2 * HALO + tm, :] = jnp.where(at_end, 0.0, nxt).astype(BF16)
        acc_ref[...] = jnp.zeros_like(acc_ref)

    n_ext = tm + 2 * HALO
    gate = jnp.dot(xs_ref[...], wg_ref[...], preferred_element_type=F32)
    g_prev = pltpu.roll(gate, 1, 0)[HALO:HALO + tm, :]
    g_next = pltpu.roll(gate, n_ext - 1, 0)[HALO:HALO + tm, :]
    g_mid = gate[HALO:HALO + tm, :]
    g = g_prev * cw_ref[0:1, :] + g_mid * cw_ref[1:2, :] + g_next * cw_ref[2:3, :] + cb_ref[...]
    u = jnp.dot(xs_ref[HALO:HALO + tm, :], wu_ref[...], preferred_element_type=F32)
    a = (g * jax.nn.sigmoid(g) * u).astype(BF16)
    acc_ref[...] += jnp.dot(a, wd_ref[...], preferred_element_type=F32)

    @pl.when(f == pl.num_programs(1) - 1)
    def _():
        o_ref[...] = x_ref[...] + acc_ref[...]


def ffn(x, lw, *, seq_len, tm, tf):
    M, D = x.shape
    F = lw["w_gate"].shape[1]
    assert seq_len % tm == 0 and tm % HALO == 0
    hb = tm // HALO
    last = M // HALO - 1
    return pl.pallas_call(
        functools.partial(_ffn_kernel, seq_len=seq_len),
        grid=(M // tm, F // tf),
        in_specs=[pl.BlockSpec((tm, D), lambda i, f: (i, 0)),
                  pl.BlockSpec((HALO, D), lambda i, f: (jnp.maximum(i * hb - 1, 0), 0)),
                  pl.BlockSpec((HALO, D), lambda i, f: (jnp.minimum((i + 1) * hb, last), 0)),
                  pl.BlockSpec((1, D), lambda i, f: (0, 0)),
                  pl.BlockSpec((D, tf), lambda i, f: (0, f)),
                  pl.BlockSpec((D, tf), lambda i, f: (0, f)),
                  pl.BlockSpec((3, tf), lambda i, f: (0, f)),
                  pl.BlockSpec((1, tf), lambda i, f: (0, f)),
                  pl.BlockSpec((tf, D), lambda i, f: (f, 0))],
        out_specs=pl.BlockSpec((tm, D), lambda i, f: (i, 0)),
        out_shape=jax.ShapeDtypeStruct((M, D), F32),
        scratch_shapes=[pltpu.VMEM((tm + 2 * HALO, D), BF16), pltpu.VMEM((tm, D), F32)],
        compiler_params=_cparams(("parallel", "arbitrary")),
        name="ffn",
    )(x, x, x, lw["ffn_g"], lw["w_gate"], lw["w_up"], lw["conv_w"], lw["conv_b"], lw["w_down"])


def _ple_kernel(x_ref, p_ref, g_ref, wg_ref, wp_ref, o_ref):
    x = x_ref[...]
    xn = _rms(x, g_ref[...], x.shape[-1]).astype(BF16)
    gate = jax.nn.sigmoid(jnp.dot(xn, wg_ref[...], preferred_element_type=F32))
    proj = jnp.dot(p_ref[...].astype(BF16), wp_ref[...], preferred_element_type=F32)
    o_ref[...] = x + gate * proj


def ple(x, p, lw, *, tm):
    M, D = x.shape
    row = lambda w: pl.BlockSpec((tm, w), lambda i: (i, 0))
    full = lambda a: pl.BlockSpec(a.shape, lambda i: (0,) * a.ndim)
    params = (lw["ple_g"], lw["w_ple_gate"], lw["w_ple_proj"])
    return pl.pallas_call(
        _ple_kernel,
        grid=(M // tm,),
        in_specs=[row(D), row(p.shape[1])] + [full(a) for a in params],
        out_specs=row(D),
        out_shape=jax.ShapeDtypeStruct((M, D), F32),
        compiler_params=_cparams(("parallel",)),
        name="ple",
    )(x, p, *params)


def _rope_tables(seq_len):
    def angles(dim, theta):
        inv = 1.0 / (theta ** (jnp.arange(0, dim, 2, dtype=F32) / dim))
        ang = jnp.arange(seq_len, dtype=F32)[:, None] * inv[None, :]
        return jnp.cos(ang), jnp.sin(ang)

    z = lambda w: jnp.zeros((seq_len, w), F32)
    cos, sin = angles(MLA_ROPE_DIM, MLA_ROPE_THETA)
    half = MLA_ROPE_DIM // 2
    cm = jnp.concatenate([cos, cos, z(64)], axis=1)
    sam = jnp.concatenate([-sin, z(half), z(64)], axis=1)
    sbm = jnp.concatenate([z(half), sin, z(64)], axis=1)
    rope_m = jnp.concatenate([cm, sam, sbm], axis=1)

    cos, sin = angles(ROPE_PART_DIM, ROPE_THETA)
    half = ROPE_PART_DIM // 2
    rest = DIFF_QK_DIM - ROPE_PART_DIM
    c64 = jnp.concatenate([cos, cos, jnp.ones((seq_len, rest), F32)], axis=1)
    sa64 = jnp.concatenate([-sin, z(half), z(rest)], axis=1)
    sb64 = jnp.concatenate([z(half), sin, z(rest)], axis=1)
    rope_d = jnp.concatenate([c64, c64, sa64, sa64, sb64, sb64], axis=1)
    return rope_m, rope_d


def _na_bias_table(rpb):
    cols = jnp.arange(GRID_W)
    cs = jnp.clip(cols - NA_WIN_COLS // 2, 0, GRID_W - NA_WIN_COLS)
    kc = jnp.arange(GRID_W)
    valid = (kc[None, :] >= cs[:, None]) & (kc[None, :] < cs[:, None] + NA_WIN_COLS)
    nv = 2 * NA_WIN_COLS - 1
    onehot = (kc[None, None, :] - cols[None, :, None] + (NA_WIN_COLS - 1) == jnp.arange(nv)[:, None, None])
    toep = jnp.einsum("huv,vck->huck", rpb.astype(F32), onehot.astype(F32), precision=lax.Precision.HIGHEST)
    toep = jnp.where(valid[None, None], toep * LOG2E, NEG_BIG)
    tab = jnp.stack([jnp.concatenate([toep[:, u0 + i] for i in range(NA_WIN_ROWS)], axis=-1)
                     for u0 in range(NA_WIN_ROWS)], axis=1)
    return tab


def _layer_params(i, norm_mix, w_in, na_q_norm, na_k_norm, na_rpb, mla_q_a_norm, mla_w_q_b, mla_kv_a_norm,
                  mla_w_kv_b, mla_q_nope_norm, mla_q_pe_norm, mla_k_nope_norm, mla_k_pe_norm,
                  diff_q_norm, diff_k_norm, diff_lambda_q1, diff_lambda_k1, diff_lambda_q2, diff_lambda_k2,
                  diff_subln, w_out, norm_ffn, w_gate, w_up, conv_w, conv_b, w_down,
                  ple_norm, w_ple_gate, w_ple_proj):
    r = lambda a: a[i].reshape(1, -1).astype(F32)
    pad_to = lambda a, n: jnp.pad(a, ((0, 0), (0, n - a.shape[1])))
    w = w_in[i]
    kpe0 = 3 * NA_W + MLA_Q_RANK + MLA_KV_RANK
    w_perm = jnp.concatenate([w[:, :kpe0], w[:, kpe0 + MLA_ROPE_DIM:], w[:, kpe0:kpe0 + MLA_ROPE_DIM],
                              jnp.zeros((w.shape[0], 128 - MLA_ROPE_DIM), w.dtype)], axis=1).astype(BF16)
    wqb = mla_w_q_b[i].reshape(MLA_Q_RANK, MLA_HEADS, MLA_QK_DIM)
    wqb = jnp.pad(wqb, ((0, 0), (0, 0), (0, MLA_SLOT - MLA_QK_DIM))).reshape(MLA_Q_RANK, MLA_HEADS * MLA_SLOT)
    wo = w_out[i].astype(BF16)
    return dict(
        mix_g=r(norm_mix), w_in=w_perm,
        naq_g=r(na_q_norm), nak_g=r(na_k_norm), na_bias=_na_bias_table(na_rpb[i]),
        qa_g=r(mla_q_a_norm), wqb=wqb.astype(BF16), kva_g=r(mla_kv_a_norm), wkvb=mla_w_kv_b[i].astype(BF16),
        qn_g=r(mla_q_nope_norm), qpe_g=pad_to(r(mla_q_pe_norm), 128),
        kn_g=r(mla_k_nope_norm), kpe_g=pad_to(r(mla_k_pe_norm), 128),
        dq_g=jnp.tile(r(diff_q_norm), (1, 2)), dk_g=jnp.tile(r(diff_k_norm), (1, 2)),
        lq1=r(diff_lambda_q1), lk1=r(diff_lambda_k1), lq2=r(diff_lambda_q2), lk2=r(diff_lambda_k2),
        subln_g=r(diff_subln),
        w_out_na=wo[:NA_W], w_out_mla=wo[NA_W:NA_W + MLA_VW], w_out_df=wo[NA_W + MLA_VW:],
        ffn_g=r(norm_ffn), w_gate=w_gate[i].astype(BF16), w_up=w_up[i].astype(BF16),
        conv_w=conv_w[i].astype(F32), conv_b=r(conv_b), w_down=w_down[i].astype(BF16),
        ple_g=r(ple_norm), w_ple_gate=w_ple_gate[i].astype(BF16), w_ple_proj=w_ple_proj[i].astype(BF16),
    )


def _tile(n, pref):
    t = min(pref, n)
    while n % t:
        t //= 2
    return t


def _encoder_layer(x, p_l, lw, rope_m, rope_d, *, layer_idx, batch, seq_len):
    tm = _tile(seq_len, 512)
    proj = rms_matmul(x, lw["mix_g"], lw["w_in"], tm=tm, tn=1024, out_dtype=F32)
    (na_q, na_k, na_v, m_q, m_kT, m_v, d_q, d_kT, d_v) = head_prep(
        proj, rope_m, rope_d, lw, seq_len=seq_len, tm=_tile(seq_len, 256))
    o_na = na_attention(na_q, na_k, na_v, lw["na_bias"], batch=batch, seq_len=seq_len,
                        rq=_tile(seq_len // GRID_W, 8))
    tk = _tile(seq_len // 2, 1024)
    tq = _tile(seq_len, 1024 if seq_len >= 8192 else 512)
    o_mla = flash_attention(m_q, m_kT, m_v, batch=batch, seq_len=seq_len, n_heads=MLA_HEADS, dq=MLA_SLOT,
                            v_of_head=lambda h: h, tq=tq, tk=tk, out_dtype=BF16)
    o_df = flash_attention(d_q, d_kT, d_v, batch=batch, seq_len=seq_len, n_heads=2 * DIFF_HEADS, dq=128,
                           v_of_head=lambda h: h // 2, tq=tq, tk=tk, out_dtype=F32)
    lam_init = 0.8 - 0.6 * math.exp(-0.3 * layer_idx)
    x = out_proj(x, o_na, o_mla, o_df, lw, lam_init=lam_init, tm=tm)
    x = ffn(x, lw, seq_len=seq_len, tm=tm, tf=512)
    x = ple(x, p_l, lw, tm=tm)
    return x


def kernel(x_prompt, x_sample, p_prompt, p_sample, norm_mix, w_in, na_q_norm, na_k_norm, na_rpb, mla_q_a_norm, mla_w_q_b, mla_kv_a_norm, mla_w_kv_b, mla_q_nope_norm, mla_q_pe_norm, mla_k_nope_norm, mla_k_pe_norm, diff_q_norm, diff_k_norm, diff_lambda_q1, diff_lambda_k1, diff_lambda_q2, diff_lambda_k2, diff_subln, w_out, norm_ffn, w_gate, w_up, conv_w, conv_b, w_down, ple_norm, w_ple_gate, w_ple_proj):
    weights = (norm_mix, w_in, na_q_norm, na_k_norm, na_rpb, mla_q_a_norm, mla_w_q_b, mla_kv_a_norm, mla_w_kv_b,
               mla_q_nope_norm, mla_q_pe_norm, mla_k_nope_norm, mla_k_pe_norm, diff_q_norm, diff_k_norm,
               diff_lambda_q1, diff_lambda_k1, diff_lambda_q2, diff_lambda_k2, diff_subln, w_out,
               norm_ffn, w_gate, w_up, conv_w, conv_b, w_down, ple_norm, w_ple_gate, w_ple_proj)
    depth = norm_mix.shape[0]
    groups = []
    for x, p in ((x_prompt, p_prompt), (x_sample, p_sample)):
        b, t, d = x.shape
        groups.append(dict(x=x.reshape(b * t, d), p=p.reshape(depth, b * t, p.shape[-1]), batch=b, seq_len=t,
                           rope=_rope_tables(t), shape=x.shape))
    for i in range(depth):
        lw = _layer_params(i, *weights)
        for g in groups:
            g["x"] = _encoder_layer(g["x"], g["p"][i], lw, *g["rope"], layer_idx=i,
                                    batch=g["batch"], seq_len=g["seq_len"])
    return tuple(g["x"].reshape(g["shape"]) for g in groups)
```

```python
import functools
import math

import jax
import jax.numpy as jnp
import numpy as np
from jax import lax
from jax.experimental import pallas as pl
from jax.experimental.pallas import tpu as pltpu

F32 = jnp.float32
BF16 = jnp.bfloat16

EPS = 1e-6
LOG2E = 1.4426950408889634
NEG_BIG = -1e30

D_MODEL = 2048
PLE_DIM = 256
GRID_W = 64
HEAD_DIM = 128
NA_HEADS = 6
NA_WIN_ROWS = 8
NA_WIN_COLS = 16
MLA_HEADS = 5
MLA_Q_RANK = 512
MLA_KV_RANK = 256
MLA_NOPE_DIM = 128
MLA_ROPE_DIM = 64
MLA_V_DIM = 128
MLA_ROPE_THETA = 10000.0
DIFF_HEADS = 5
DIFF_QK_DIM = 64
DIFF_V_DIM = 128
ROPE_THETA = 500000.0
ROPE_PART_DIM = DIFF_QK_DIM // 4
D_FF = 5632

NA_W = NA_HEADS * HEAD_DIM
MLA_QK_DIM = MLA_NOPE_DIM + MLA_ROPE_DIM
MLA_SLOT = 256
DIFF_W = DIFF_HEADS * 2 * DIFF_QK_DIM
DIFF_VW = DIFF_HEADS * DIFF_V_DIM
MLA_VW = MLA_HEADS * MLA_V_DIM

C_NAQ = 0
C_NAK = C_NAQ + NA_W
C_NAV = C_NAK + NA_W
C_CQ = C_NAV + NA_W
C_CKV = C_CQ + MLA_Q_RANK
C_DQ = C_CKV + MLA_KV_RANK
C_DK = C_DQ + DIFF_W
C_DV = C_DK + DIFF_W
C_KPE = C_DV + DIFF_VW
IN_COLS_PAD = C_KPE + 128

LANES = 128
IN_TILE = 1024
FF_TILE = 512
VMEM_LIMIT = 56 * 1024 * 1024


def _cparams(sem):
    return pltpu.CompilerParams(dimension_semantics=sem, vmem_limit_bytes=VMEM_LIMIT)


def _rms(x, g, n):
    ms = jnp.sum(x * x, axis=-1, keepdims=True) * (1.0 / n)
    return x * lax.rsqrt(ms + EPS) * g


def _rms_matmul_kernel(x_ref, g_ref, w_ref, o_ref, xn_ref):
    @pl.when(pl.program_id(1) == 0)
    def _():
        xn_ref[...] = _rms(x_ref[...], g_ref[...], x_ref.shape[-1]).astype(BF16)

    o_ref[...] = jnp.dot(xn_ref[...], w_ref[...], preferred_element_type=F32).astype(o_ref.dtype)


def _col_tiles(w, tn):
    k, n = w.shape
    return w.reshape(k, n // tn, tn).transpose(1, 0, 2)


def rms_matmul(x, g, w, *, tm, out_dtype):
    M, K = x.shape
    nt, _, tn = w.shape
    N = nt * tn
    return pl.pallas_call(
        _rms_matmul_kernel,
        grid=(M // tm, nt),
        in_specs=[pl.BlockSpec((tm, K), lambda i, j: (i, 0)),
                  pl.BlockSpec((1, K), lambda i, j: (0, 0)),
                  pl.BlockSpec((None, K, tn), lambda i, j: (j, 0, 0))],
        out_specs=pl.BlockSpec((tm, tn), lambda i, j: (i, j)),
        out_shape=jax.ShapeDtypeStruct((M, N), out_dtype),
        scratch_shapes=[pltpu.VMEM((tm, K), BF16)],
        compiler_params=_cparams(("parallel", "arbitrary")),
        name="rms_matmul",
    )(x, g, w)


def _prep_kernel(proj_ref, rope_m_ref, rope_d_ref, naq_g, nak_g, qa_g, wqb_ref, kva_g, wkvb_ref,
                 qn_g, qpe_g, kn_g, kpe_g, dq_g, dk_g,
                 naq_o, nak_o, nav_o, mq_o, mkT_o, mv_o, dq_o, dkT_o, dv_o):
    tm = proj_ref.shape[0]
    lane = lax.broadcasted_iota(jnp.int32, (tm, LANES), 1)
    lo = lane < 64

    na_scale = HEAD_DIM ** -0.5 * LOG2E
    for h in range(NA_HEADS):
        sl = slice(h * HEAD_DIM, (h + 1) * HEAD_DIM)
        q = _rms(proj_ref[:, C_NAQ + h * HEAD_DIM:C_NAQ + (h + 1) * HEAD_DIM], naq_g[...], HEAD_DIM)
        naq_o[:, sl] = (q * na_scale).astype(BF16)
        k = _rms(proj_ref[:, C_NAK + h * HEAD_DIM:C_NAK + (h + 1) * HEAD_DIM], nak_g[...], HEAD_DIM)
        nak_o[:, sl] = k.astype(BF16)
    nav_o[...] = proj_ref[:, C_NAV:C_NAV + NA_W].astype(BF16)

    cm, sam, sbm = rope_m_ref[:, 0:128], rope_m_ref[:, 128:256], rope_m_ref[:, 256:384]

    def rope_m(y):
        return y * cm + pltpu.roll(y, 96, 1) * sam + pltpu.roll(y, 32, 1) * sbm

    mla_scale = MLA_QK_DIM ** -0.5 * LOG2E
    cq = _rms(proj_ref[:, C_CQ:C_CQ + MLA_Q_RANK], qa_g[...], MLA_Q_RANK).astype(BF16)
    qm = jnp.dot(cq, wqb_ref[...], preferred_element_type=F32)
    for h in range(MLA_HEADS):
        b = h * MLA_SLOT
        nope = _rms(qm[:, b:b + 128], qn_g[...], MLA_NOPE_DIM)
        pe = rope_m(_rms(qm[:, b + 128:b + 256], qpe_g[...], MLA_ROPE_DIM))
        mq_o[:, b:b + 128] = (nope * mla_scale).astype(BF16)
        mq_o[:, b + 128:b + 256] = (pe * mla_scale).astype(BF16)
    ckv = _rms(proj_ref[:, C_CKV:C_CKV + MLA_KV_RANK], kva_g[...], MLA_KV_RANK).astype(BF16)
    kv = jnp.dot(ckv, wkvb_ref[...], preferred_element_type=F32)
    kpe = rope_m(_rms(proj_ref[:, C_KPE:C_KPE + 128], kpe_g[...], MLA_ROPE_DIM))
    kpe_t = kpe.T.astype(BF16)
    for h in range(MLA_HEADS):
        b = h * MLA_SLOT
        kn = _rms(kv[:, b:b + 128], kn_g[...], MLA_NOPE_DIM)
        mkT_o[b:b + 128, :] = kn.T.astype(BF16)
        mkT_o[b + 128:b + 256, :] = kpe_t
        mv_o[:, h * MLA_V_DIM:(h + 1) * MLA_V_DIM] = kv[:, b + 128:b + 256].astype(BF16)

    cd, sad, sbd = rope_d_ref[:, 0:128], rope_d_ref[:, 128:256], rope_d_ref[:, 256:384]

    def rope_d(y):
        return y * cd + pltpu.roll(y, 120, 1) * sad + pltpu.roll(y, 8, 1) * sbd

    def group_rms(x, g):
        x2 = x * x
        s_lo = jnp.sum(jnp.where(lo, x2, 0.0), axis=-1, keepdims=True)
        s_hi = jnp.sum(jnp.where(lo, 0.0, x2), axis=-1, keepdims=True)
        ms = jnp.where(lo, s_lo, s_hi) * (1.0 / DIFF_QK_DIM)
        return x * lax.rsqrt(ms + EPS) * g

    def split_components(y):
        return jnp.where(lo, y, 0.0), jnp.where(lo, pltpu.roll(y, 64, 1), 0.0)

    df_scale = DIFF_QK_DIM ** -0.5 * LOG2E
    for h in range(DIFF_HEADS):
        q = rope_d(group_rms(proj_ref[:, C_DQ + h * 128:C_DQ + (h + 1) * 128], dq_g[...])) * df_scale
        q0, q1 = split_components(q)
        dq_o[:, (2 * h) * 128:(2 * h + 1) * 128] = q0.astype(BF16)
        dq_o[:, (2 * h + 1) * 128:(2 * h + 2) * 128] = q1.astype(BF16)
        k = rope_d(group_rms(proj_ref[:, C_DK + h * 128:C_DK + (h + 1) * 128], dk_g[...]))
        k0, k1 = split_components(k)
        dkT_o[(2 * h) * 128:(2 * h + 1) * 128, :] = k0.T.astype(BF16)
        dkT_o[(2 * h + 1) * 128:(2 * h + 2) * 128, :] = k1.T.astype(BF16)
    dv_o[...] = proj_ref[:, C_DV:C_DV + DIFF_VW].astype(BF16)


def head_prep(proj, rope_m, rope_d, lw, *, seq_len, tm):
    M = proj.shape[0]
    nt = seq_len // tm
    row = lambda w: pl.BlockSpec((tm, w), lambda i: (i, 0))
    colT = lambda h: pl.BlockSpec((h, tm), lambda i: (0, i))
    full = lambda a: pl.BlockSpec(a.shape, lambda i: (0,) * a.ndim)
    rope = pl.BlockSpec((tm, 384), lambda i: (i % nt, 0))
    params = (lw["naq_g"], lw["nak_g"], lw["qa_g"], lw["wqb"], lw["kva_g"], lw["wkvb"],
              lw["qn_g"], lw["qpe_g"], lw["kn_g"], lw["kpe_g"], lw["dq_g"], lw["dk_g"])
    out_shapes = (
        jax.ShapeDtypeStruct((M, NA_W), BF16), jax.ShapeDtypeStruct((M, NA_W), BF16),
        jax.ShapeDtypeStruct((M, NA_W), BF16),
        jax.ShapeDtypeStruct((M, MLA_HEADS * MLA_SLOT), BF16),
        jax.ShapeDtypeStruct((MLA_HEADS * MLA_SLOT, M), BF16),
        jax.ShapeDtypeStruct((M, MLA_VW), BF16),
        jax.ShapeDtypeStruct((M, 2 * DIFF_HEADS * 128), BF16),
        jax.ShapeDtypeStruct((2 * DIFF_HEADS * 128, M), BF16),
        jax.ShapeDtypeStruct((M, DIFF_VW), BF16),
    )
    out_specs = (row(NA_W), row(NA_W), row(NA_W), row(MLA_HEADS * MLA_SLOT), colT(MLA_HEADS * MLA_SLOT),
                 row(MLA_VW), row(2 * DIFF_HEADS * 128), colT(2 * DIFF_HEADS * 128), row(DIFF_VW))
    return pl.pallas_call(
        _prep_kernel,
        grid=(M // tm,),
        in_specs=[row(IN_COLS_PAD), rope, rope] + [full(a) for a in params],
        out_specs=out_specs,
        out_shape=out_shapes,
        compiler_params=_cparams(("parallel",)),
        name="head_prep",
    )(proj, rope_m, rope_d, *params)


def _flash_kernel(q_ref, kT_ref, v_ref, o_ref, sa_ref, sb_ref, acc_ref, *, tq, tk):
    seq_len = q_ref.shape[0]
    nk = seq_len // tk
    total = (seq_len // tq) * nk

    def scores(f):
        qoff = pl.multiple_of((f // nk) * tq, tq)
        koff = pl.multiple_of((f % nk) * tk, tk)
        return jnp.dot(q_ref[pl.ds(qoff, tq), :], kT_ref[:, pl.ds(koff, tk)],
                       preferred_element_type=F32)

    def softmax_pv(s_ref, c, m, l):
        s = s_ref[...]
        m_new = jnp.maximum(m, jnp.max(s, axis=-1, keepdims=True))
        alpha = jnp.exp2(m - m_new)
        p = jnp.exp2(s - m_new)
        psum = p[:, 0:LANES]
        for t in range(1, tk // LANES):
            psum = psum + p[:, t * LANES:(t + 1) * LANES]
        koff = pl.multiple_of(c * tk, tk)
        pv = jnp.dot(p.astype(BF16), v_ref[pl.ds(koff, tk), :], preferred_element_type=F32)
        acc_ref[...] = alpha * acc_ref[...] + pv
        return m_new, alpha * l + psum

    def pair(i, carry):
        m, l = carry
        f = 2 * i
        c = f % nk
        m = jnp.where(c == 0, NEG_BIG, m)
        sb_ref[...] = scores(f + 1)
        m, l = softmax_pv(sa_ref, c, m, l)
        sa_ref[...] = scores(jnp.minimum(f + 2, total - 1))
        m, l = softmax_pv(sb_ref, c + 1, m, l)

        @pl.when(c + 2 == nk)
        def _():
            qoff = pl.multiple_of((f // nk) * tq, tq)
            out = acc_ref[...] / jnp.sum(l, axis=-1, keepdims=True)
            o_ref[pl.ds(qoff, tq), :] = out.astype(o_ref.dtype)

        return m, l

    acc_ref[...] = jnp.zeros_like(acc_ref)
    sa_ref[...] = scores(0)
    init = (jnp.full((tq, 1), NEG_BIG, F32), jnp.zeros((tq, LANES), F32))
    lax.fori_loop(0, total // 2, pair, init)


def flash_attention(q, kT, v, *, batch, seq_len, n_heads, dq, v_of_head, tq, tk, out_dtype):
    M = q.shape[0]
    dv = 128
    assert seq_len % (2 * tk) == 0 and seq_len % tq == 0
    once = pl.Buffered(1)
    return pl.pallas_call(
        functools.partial(_flash_kernel, tq=tq, tk=tk),
        grid=(batch, n_heads),
        in_specs=[pl.BlockSpec((seq_len, dq), lambda b, h: (b, h), pipeline_mode=once),
                  pl.BlockSpec((dq, seq_len), lambda b, h: (h, b), pipeline_mode=once),
                  pl.BlockSpec((seq_len, dv), lambda b, h: (b, v_of_head(h)), pipeline_mode=once)],
        out_specs=pl.BlockSpec((seq_len, dv), lambda b, h: (b, h)),
        out_shape=jax.ShapeDtypeStruct((M, n_heads * dv), out_dtype),
        scratch_shapes=[pltpu.VMEM((tq, tk), F32), pltpu.VMEM((tq, tk), F32), pltpu.VMEM((tq, dv), F32)],
        compiler_params=_cparams(("parallel", "parallel")),
        name="flash_attention",
    )(q, kT, v)


NA_QROWS = 8
NA_KROWS = 2 * NA_WIN_ROWS


def _na_kernel(q_ref, k_ref, v_ref, bias_ref, o_ref, *, rows):
    i = pl.program_id(2)
    ws = jnp.clip(i * NA_QROWS - NA_WIN_ROWS // 2, 0, rows - NA_KROWS)
    ks = pl.multiple_of(ws * GRID_W, GRID_W)
    win = NA_KROWS * GRID_W
    k = k_ref[pl.ds(ks, win), :]
    s = lax.dot_general(q_ref[...], k, (((1,), (1,)), ((), ())), preferred_element_type=F32)
    s = s + bias_ref[...]
    m = jnp.max(s, axis=-1, keepdims=True)
    p = jnp.exp2(s - m)
    l = jnp.sum(p, axis=-1, keepdims=True)
    o = jnp.dot(p.astype(BF16), v_ref[pl.ds(ks, win), :], preferred_element_type=F32)
    o_ref[...] = (o / l).astype(o_ref.dtype)


def na_attention(q, k, v, bias, *, batch, seq_len):
    M = q.shape[0]
    rows = seq_len // GRID_W
    assert rows >= NA_KROWS and rows % NA_QROWS == 0
    nblk = rows // NA_QROWS
    tq = NA_QROWS * GRID_W
    pattern = lambda i: jnp.where(i == 0, 0, jnp.where(i == nblk - 1, 2, 1))
    return pl.pallas_call(
        functools.partial(_na_kernel, rows=rows),
        grid=(batch, NA_HEADS, nblk),
        in_specs=[pl.BlockSpec((tq, HEAD_DIM), lambda b, h, i: (b * nblk + i, h)),
                  pl.BlockSpec((seq_len, HEAD_DIM), lambda b, h, i: (b, h)),
                  pl.BlockSpec((seq_len, HEAD_DIM), lambda b, h, i: (b, h)),
                  pl.BlockSpec((None, None, tq, NA_KROWS * GRID_W), lambda b, h, i: (h, pattern(i), 0, 0))],
        out_specs=pl.BlockSpec((tq, HEAD_DIM), lambda b, h, i: (b * nblk + i, h)),
        out_shape=jax.ShapeDtypeStruct((M, NA_W), BF16),
        compiler_params=_cparams(("parallel", "parallel", "arbitrary")),
        name="na_attention",
    )(q, k, v, bias)


def _out_proj_kernel(x_ref, ona_ref, omla_ref, odf_ref, subln_ref, lq1, lk1, lq2, lk2,
                     w_na, w_mla, w_df, o_ref, *, lam_init):
    lam = (jnp.exp(jnp.sum(lq1[...] * lk1[...], axis=-1, keepdims=True))
           - jnp.exp(jnp.sum(lq2[...] * lk2[...], axis=-1, keepdims=True)) + lam_init)
    heads = []
    for h in range(DIFF_HEADS):
        o1 = odf_ref[:, (2 * h) * 128:(2 * h + 1) * 128]
        o2 = odf_ref[:, (2 * h + 1) * 128:(2 * h + 2) * 128]
        o = _rms(o1 - lam * o2, subln_ref[...], DIFF_V_DIM) * (1.0 - lam_init)
        heads.append(o.astype(BF16))
    odf = jnp.concatenate(heads, axis=-1)
    y = jnp.dot(ona_ref[...], w_na[...], preferred_element_type=F32)
    y = y + jnp.dot(omla_ref[...], w_mla[...], preferred_element_type=F32)
    y = y + jnp.dot(odf, w_df[...], preferred_element_type=F32)
    o_ref[...] = x_ref[...] + y


def out_proj(x, o_na, o_mla, o_df, lw, *, lam_init, tm):
    M, D = x.shape
    row = lambda w: pl.BlockSpec((tm, w), lambda i: (i, 0))
    full = lambda a: pl.BlockSpec(a.shape, lambda i: (0,) * a.ndim)
    params = (lw["subln_g"], lw["lq1"], lw["lk1"], lw["lq2"], lw["lk2"], lw["w_out_na"], lw["w_out_mla"], lw["w_out_df"])
    return pl.pallas_call(
        functools.partial(_out_proj_kernel, lam_init=lam_init),
        grid=(M // tm,),
        in_specs=[row(D), row(NA_W), row(MLA_VW), row(2 * DIFF_HEADS * 128)] + [full(a) for a in params],
        out_specs=row(D),
        out_shape=jax.ShapeDtypeStruct((M, D), F32),
        compiler_params=_cparams(("parallel",)),
        name="out_proj",
    )(x, o_na, o_mla, o_df, *params)


HALO = 16


def _ffn_kernel(x_ref, xp_ref, xn_ref, g_ref, wg_ref, wu_ref, cw_ref, cb_ref, wd_ref, o_ref,
                xs_ref, *, seq_len, rows_per_chain):
    i = pl.program_id(0)
    f = pl.program_id(1)
    tm = x_ref.shape[0]

    @pl.when(f == 0)
    def _():
        d = x_ref.shape[-1]
        row0 = i * tm
        at_start = (row0 % seq_len) == 0
        at_end = ((row0 + tm) % seq_len) == 0
        prev = _rms(xp_ref[...], g_ref[...], d)
        nxt = _rms(xn_ref[...], g_ref[...], d)
        xs_ref[0:HALO, :] = jnp.where(at_start, 0.0, prev).astype(BF16)
        xs_ref[HALO:HALO + tm, :] = _rms(x_ref[...], g_ref[...], d).astype(BF16)
        xs_ref[HALO + tm:2 * HALO + tm, :] = jnp.where(at_end, 0.0, nxt).astype(BF16)
        o_ref[...] = x_ref[...]

    tc = rows_per_chain
    n_ext = tc + 2 * HALO
    chains = range(0, tm, tc)
    gates = [jnp.dot(xs_ref[r0:r0 + n_ext, :], wg_ref[...], preferred_element_type=F32) for r0 in chains]
    ups = [jnp.dot(xs_ref[r0 + HALO:r0 + HALO + tc, :], wu_ref[...], preferred_element_type=F32)
           for r0 in chains]
    for r0, gate, u in zip(chains, gates, ups):
        g_prev = pltpu.roll(gate, 1, 0)[HALO:HALO + tc, :]
        g_next = pltpu.roll(gate, n_ext - 1, 0)[HALO:HALO + tc, :]
        g_mid = gate[HALO:HALO + tc, :]
        g = g_prev * cw_ref[0:1, :] + g_mid * cw_ref[1:2, :] + g_next * cw_ref[2:3, :] + cb_ref[...]
        a = (g * jax.nn.sigmoid(g) * u).astype(BF16)
        o_ref[r0:r0 + tc, :] += jnp.dot(a, wd_ref[...], preferred_element_type=F32)


def ffn(x, lw, *, seq_len, tm):
    M, D = x.shape
    nf, _, tf = lw["w_gate"].shape
    assert seq_len % tm == 0 and tm % HALO == 0
    hb = tm // HALO
    last = M // HALO - 1
    return pl.pallas_call(
        functools.partial(_ffn_kernel, seq_len=seq_len, rows_per_chain=min(tm, 512)),
        grid=(M // tm, nf),
        in_specs=[pl.BlockSpec((tm, D), lambda i, f: (i, 0), pipeline_mode=pl.Buffered(1)),
                  pl.BlockSpec((HALO, D), lambda i, f: (jnp.maximum(i * hb - 1, 0), 0)),
                  pl.BlockSpec((HALO, D), lambda i, f: (jnp.minimum((i + 1) * hb, last), 0)),
                  pl.BlockSpec((1, D), lambda i, f: (0, 0)),
                  pl.BlockSpec((None, D, tf), lambda i, f: (f, 0, 0)),
                  pl.BlockSpec((None, D, tf), lambda i, f: (f, 0, 0)),
                  pl.BlockSpec((3, tf), lambda i, f: (0, f)),
                  pl.BlockSpec((1, tf), lambda i, f: (0, f)),
                  pl.BlockSpec((tf, D), lambda i, f: (f, 0))],
        out_specs=pl.BlockSpec((tm, D), lambda i, f: (i, 0)),
        out_shape=jax.ShapeDtypeStruct((M, D), F32),
        scratch_shapes=[pltpu.VMEM((tm + 2 * HALO, D), BF16)],
        compiler_params=_cparams(("parallel", "arbitrary")),
        name="ffn",
    )(x, x, x, lw["ffn_g"], lw["w_gate"], lw["w_up"], lw["conv_w"], lw["conv_b"], lw["w_down"])


def _ple_kernel(x_ref, p_ref, g_ref, wg_ref, wp_ref, o_ref):
    x = x_ref[...]
    xn = _rms(x, g_ref[...], x.shape[-1]).astype(BF16)
    gate = jax.nn.sigmoid(jnp.dot(xn, wg_ref[...], preferred_element_type=F32))
    proj = jnp.dot(p_ref[...].astype(BF16), wp_ref[...], preferred_element_type=F32)
    o_ref[...] = x + gate * proj


def ple(x, p, lw, *, tm):
    M, D = x.shape
    row = lambda w: pl.BlockSpec((tm, w), lambda i: (i, 0))
    full = lambda a: pl.BlockSpec(a.shape, lambda i: (0,) * a.ndim)
    params = (lw["ple_g"], lw["w_ple_gate"], lw["w_ple_proj"])
    return pl.pallas_call(
        _ple_kernel,
        grid=(M // tm,),
        in_specs=[row(D), row(p.shape[1])] + [full(a) for a in params],
        out_specs=row(D),
        out_shape=jax.ShapeDtypeStruct((M, D), F32),
        compiler_params=_cparams(("parallel",)),
        name="ple",
    )(x, p, *params)


def _rope_tables(seq_len):
    def angles(dim, theta):
        inv = 1.0 / (theta ** (jnp.arange(0, dim, 2, dtype=F32) / dim))
        ang = jnp.arange(seq_len, dtype=F32)[:, None] * inv[None, :]
        return jnp.cos(ang), jnp.sin(ang)

    z = lambda w: jnp.zeros((seq_len, w), F32)
    cos, sin = angles(MLA_ROPE_DIM, MLA_ROPE_THETA)
    half = MLA_ROPE_DIM // 2
    cm = jnp.concatenate([cos, cos, z(64)], axis=1)
    sam = jnp.concatenate([-sin, z(half), z(64)], axis=1)
    sbm = jnp.concatenate([z(half), sin, z(64)], axis=1)
    rope_m = jnp.concatenate([cm, sam, sbm], axis=1)

    cos, sin = angles(ROPE_PART_DIM, ROPE_THETA)
    half = ROPE_PART_DIM // 2
    rest = DIFF_QK_DIM - ROPE_PART_DIM
    c64 = jnp.concatenate([cos, cos, jnp.ones((seq_len, rest), F32)], axis=1)
    sa64 = jnp.concatenate([-sin, z(half), z(rest)], axis=1)
    sb64 = jnp.concatenate([z(half), sin, z(rest)], axis=1)
    rope_d = jnp.concatenate([c64, c64, sa64, sa64, sb64, sb64], axis=1)
    return rope_m, rope_d


def _na_bias_table(rpb):
    cols = jnp.arange(GRID_W)
    cs = jnp.clip(cols - NA_WIN_COLS // 2, 0, GRID_W - NA_WIN_COLS)
    kc = jnp.arange(GRID_W)
    valid = (kc[None, :] >= cs[:, None]) & (kc[None, :] < cs[:, None] + NA_WIN_COLS)
    nv = 2 * NA_WIN_COLS - 1
    onehot = (kc[None, None, :] - cols[None, :, None] + (NA_WIN_COLS - 1) == jnp.arange(nv)[:, None, None])
    toep = jnp.einsum("huv,vck->huck", rpb.astype(F32), onehot.astype(F32), precision=lax.Precision.HIGHEST)
    toep = jnp.where(valid[None, None], toep * LOG2E, NEG_BIG)
    n_h = rpb.shape[0]
    masked = 2 * NA_WIN_ROWS - 1
    toep = jnp.concatenate([toep, jnp.full((n_h, 1, GRID_W, GRID_W), NEG_BIG, F32)], axis=1)
    half = NA_WIN_ROWS // 2
    tile = np.full((3, NA_QROWS, NA_KROWS), masked, np.int32)
    for pat, delta in enumerate((0, -half, -NA_WIN_ROWS)):
        for a in range(NA_QROWS):
            first = (max(a - half, 0), a, min(a + half, NA_WIN_ROWS))[pat]
            for i in range(first, first + NA_WIN_ROWS):
                tile[pat, a, i] = delta + i - a + (NA_WIN_ROWS - 1)
    tab = jnp.take(toep, jnp.asarray(tile.reshape(-1)), axis=1)
    tab = tab.reshape(n_h, 3, NA_QROWS, NA_KROWS, GRID_W, GRID_W).transpose(0, 1, 2, 4, 3, 5)
    return tab.reshape(n_h, 3, NA_QROWS * GRID_W, NA_KROWS * GRID_W)


def _layer_params(i, norm_mix, w_in, na_q_norm, na_k_norm, na_rpb, mla_q_a_norm, mla_w_q_b, mla_kv_a_norm,
                  mla_w_kv_b, mla_q_nope_norm, mla_q_pe_norm, mla_k_nope_norm, mla_k_pe_norm,
                  diff_q_norm, diff_k_norm, diff_lambda_q1, diff_lambda_k1, diff_lambda_q2, diff_lambda_k2,
                  diff_subln, w_out, norm_ffn, w_gate, w_up, conv_w, conv_b, w_down,
                  ple_norm, w_ple_gate, w_ple_proj):
    r = lambda a: a[i].reshape(1, -1).astype(F32)
    pad_to = lambda a, n: jnp.pad(a, ((0, 0), (0, n - a.shape[1])))
    w = w_in[i]
    kpe0 = 3 * NA_W + MLA_Q_RANK + MLA_KV_RANK
    w_perm = jnp.concatenate([w[:, :kpe0], w[:, kpe0 + MLA_ROPE_DIM:], w[:, kpe0:kpe0 + MLA_ROPE_DIM],
                              jnp.zeros((w.shape[0], 128 - MLA_ROPE_DIM), w.dtype)], axis=1).astype(BF16)
    wqb = mla_w_q_b[i].reshape(MLA_Q_RANK, MLA_HEADS, MLA_QK_DIM)
    wqb = jnp.pad(wqb, ((0, 0), (0, 0), (0, MLA_SLOT - MLA_QK_DIM))).reshape(MLA_Q_RANK, MLA_HEADS * MLA_SLOT)
    wo = w_out[i].astype(BF16)
    return dict(
        mix_g=r(norm_mix), w_in=_col_tiles(w_perm, IN_TILE),
        naq_g=r(na_q_norm), nak_g=r(na_k_norm), na_bias=_na_bias_table(na_rpb[i]),
        qa_g=r(mla_q_a_norm), wqb=wqb.astype(BF16), kva_g=r(mla_kv_a_norm), wkvb=mla_w_kv_b[i].astype(BF16),
        qn_g=r(mla_q_nope_norm), qpe_g=pad_to(r(mla_q_pe_norm), 128),
        kn_g=r(mla_k_nope_norm), kpe_g=pad_to(r(mla_k_pe_norm), 128),
        dq_g=jnp.tile(r(diff_q_norm), (1, 2)), dk_g=jnp.tile(r(diff_k_norm), (1, 2)),
        lq1=r(diff_lambda_q1), lk1=r(diff_lambda_k1), lq2=r(diff_lambda_q2), lk2=r(diff_lambda_k2),
        subln_g=r(diff_subln),
        w_out_na=wo[:NA_W], w_out_mla=wo[NA_W:NA_W + MLA_VW], w_out_df=wo[NA_W + MLA_VW:],
        ffn_g=r(norm_ffn), w_gate=_col_tiles(w_gate[i].astype(BF16), FF_TILE),
        w_up=_col_tiles(w_up[i].astype(BF16), FF_TILE),
        conv_w=conv_w[i].astype(F32), conv_b=r(conv_b), w_down=w_down[i].astype(BF16),
        ple_g=r(ple_norm), w_ple_gate=w_ple_gate[i].astype(BF16), w_ple_proj=w_ple_proj[i].astype(BF16),
    )


def _tile(n, pref):
    t = min(pref, n)
    while n % t:
        t //= 2
    return t


def _encoder_layer(x, p_l, lw, rope_m, rope_d, *, layer_idx, batch, seq_len):
    tm = _tile(seq_len, 512)
    proj = rms_matmul(x, lw["mix_g"], lw["w_in"], tm=_tile(seq_len, 1024), out_dtype=F32)
    (na_q, na_k, na_v, m_q, m_kT, m_v, d_q, d_kT, d_v) = head_prep(
        proj, rope_m, rope_d, lw, seq_len=seq_len, tm=_tile(seq_len, 256))
    o_na = na_attention(na_q, na_k, na_v, lw["na_bias"], batch=batch, seq_len=seq_len)
    tk = _tile(seq_len // 2, 1024)
    tq = _tile(seq_len, 1024 if seq_len >= 2048 else 512)
    o_mla = flash_attention(m_q, m_kT, m_v, batch=batch, seq_len=seq_len, n_heads=MLA_HEADS, dq=MLA_SLOT,
                            v_of_head=lambda h: h, tq=tq, tk=tk, out_dtype=BF16)
    o_df = flash_attention(d_q, d_kT, d_v, batch=batch, seq_len=seq_len, n_heads=2 * DIFF_HEADS, dq=128,
                           v_of_head=lambda h: h // 2, tq=tq, tk=tk, out_dtype=F32)
    lam_init = 0.8 - 0.6 * math.exp(-0.3 * layer_idx)
    x = out_proj(x, o_na, o_mla, o_df, lw, lam_init=lam_init, tm=tm)
    x = ffn(x, lw, seq_len=seq_len, tm=_tile(seq_len, 1024))
    x = ple(x, p_l, lw, tm=tm)
    return x


def kernel(x_prompt, x_sample, p_prompt, p_sample, norm_mix, w_in, na_q_norm, na_k_norm, na_rpb, mla_q_a_norm, mla_w_q_b, mla_kv_a_norm, mla_w_kv_b, mla_q_nope_norm, mla_q_pe_norm, mla_k_nope_norm, mla_k_pe_norm, diff_q_norm, diff_k_norm, diff_lambda_q1, diff_lambda_k1, diff_lambda_q2, diff_lambda_k2, diff_subln, w_out, norm_ffn, w_gate, w_up, conv_w, conv_b, w_down, ple_norm, w_ple_gate, w_ple_proj):
    weights = (norm_mix, w_in, na_q_norm, na_k_norm, na_rpb, mla_q_a_norm, mla_w_q_b, mla_kv_a_norm, mla_w_kv_b,
               mla_q_nope_norm, mla_q_pe_norm, mla_k_nope_norm, mla_k_pe_norm, diff_q_norm, diff_k_norm,
               diff_lambda_q1, diff_lambda_k1, diff_lambda_q2, diff_lambda_k2, diff_subln, w_out,
               norm_ffn, w_gate, w_up, conv_w, conv_b, w_down, ple_norm, w_ple_gate, w_ple_proj)
    depth = norm_mix.shape[0]
    groups = []
    for x, p in ((x_prompt, p_prompt), (x_sample, p_sample)):
        b, t, d = x.shape
        groups.append(dict(x=x.reshape(b * t, d), p=p.reshape(depth, b * t, p.shape[-1]), batch=b, seq_len=t,
                           rope=_rope_tables(t), shape=x.shape))
    for i in range(depth):
        lw = _layer_params(i, *weights)
        for g in groups:
            g["x"] = _encoder_layer(g["x"], g["p"][i], lw, *g["rope"], layer_idx=i,
                                    batch=g["batch"], seq_len=g["seq_len"])
    return tuple(g["x"].reshape(g["shape"]) for g in groups)
```

```python
import functools
import math

import jax
import jax.numpy as jnp
import numpy as np
from jax import lax
from jax.experimental import pallas as pl
from jax.experimental.pallas import tpu as pltpu

F32 = jnp.float32
BF16 = jnp.bfloat16

EPS = 1e-6
LOG2E = 1.4426950408889634
NEG_BIG = -1e30

D_MODEL = 2048
PLE_DIM = 256
GRID_W = 64
HEAD_DIM = 128
NA_HEADS = 6
NA_WIN_ROWS = 8
NA_WIN_COLS = 16
MLA_HEADS = 5
MLA_Q_RANK = 512
MLA_KV_RANK = 256
MLA_NOPE_DIM = 128
MLA_ROPE_DIM = 64
MLA_V_DIM = 128
MLA_ROPE_THETA = 10000.0
DIFF_HEADS = 5
DIFF_QK_DIM = 64
DIFF_V_DIM = 128
ROPE_THETA = 500000.0
ROPE_PART_DIM = DIFF_QK_DIM // 4
D_FF = 5632

NA_W = NA_HEADS * HEAD_DIM
MLA_QK_DIM = MLA_NOPE_DIM + MLA_ROPE_DIM
MLA_SLOT = 256
DIFF_W = DIFF_HEADS * 2 * DIFF_QK_DIM
DIFF_VW = DIFF_HEADS * DIFF_V_DIM
MLA_VW = MLA_HEADS * MLA_V_DIM

C_NAQ = 0
C_NAK = C_NAQ + NA_W
C_NAV = C_NAK + NA_W
C_CQ = C_NAV + NA_W
C_CKV = C_CQ + MLA_Q_RANK
C_DQ = C_CKV + MLA_KV_RANK
C_DK = C_DQ + DIFF_W
C_DV = C_DK + DIFF_W
C_KPE = C_DV + DIFF_VW
IN_COLS_PAD = C_KPE + 128

LANES = 128
IN_TILE = 1024
FF_TILE = 512
VMEM_LIMIT = 56 * 1024 * 1024


def _cparams(sem):
    return pltpu.CompilerParams(dimension_semantics=sem, vmem_limit_bytes=VMEM_LIMIT)


def _rms(x, g, n):
    ms = jnp.sum(x * x, axis=-1, keepdims=True) * (1.0 / n)
    return x * lax.rsqrt(ms + EPS) * g


def _rms_matmul_kernel(x_ref, g_ref, w_ref, o_ref, xn_ref):
    @pl.when(pl.program_id(1) == 0)
    def _():
        xn_ref[...] = _rms(x_ref[...], g_ref[...], x_ref.shape[-1]).astype(BF16)

    o_ref[...] = jnp.dot(xn_ref[...], w_ref[...], preferred_element_type=F32).astype(o_ref.dtype)


def _col_tiles(w, tn):
    k, n = w.shape
    return w.reshape(k, n // tn, tn).transpose(1, 0, 2)


def rms_matmul(x, g, w, *, tm, out_dtype):
    M, K = x.shape
    nt, _, tn = w.shape
    N = nt * tn
    return pl.pallas_call(
        _rms_matmul_kernel,
        grid=(M // tm, nt),
        in_specs=[pl.BlockSpec((tm, K), lambda i, j: (i, 0)),
                  pl.BlockSpec((1, K), lambda i, j: (0, 0)),
                  pl.BlockSpec((None, K, tn), lambda i, j: (j, 0, 0))],
        out_specs=pl.BlockSpec((tm, tn), lambda i, j: (i, j)),
        out_shape=jax.ShapeDtypeStruct((M, N), out_dtype),
        scratch_shapes=[pltpu.VMEM((tm, K), BF16)],
        compiler_params=_cparams(("parallel", "arbitrary")),
        name="rms_matmul",
    )(x, g, w)


def _prep_kernel(proj_ref, rope_m_ref, rope_d_ref, naq_g, nak_g, qa_g, wqb_ref, kva_g, wkvb_ref,
                 qn_g, qpe_g, kn_g, kpe_g, dq_g, dk_g,
                 naq_o, nak_o, nav_o, mq_o, mk_o, mvT_o, dq_o, dk_o, dvT_o):
    tm = proj_ref.shape[0]
    lane = lax.broadcasted_iota(jnp.int32, (tm, LANES), 1)
    lo = lane < 64

    na_scale = HEAD_DIM ** -0.5 * LOG2E
    for h in range(NA_HEADS):
        sl = slice(h * HEAD_DIM, (h + 1) * HEAD_DIM)
        q = _rms(proj_ref[:, C_NAQ + h * HEAD_DIM:C_NAQ + (h + 1) * HEAD_DIM], naq_g[...], HEAD_DIM)
        naq_o[:, sl] = (q * na_scale).astype(BF16)
        k = _rms(proj_ref[:, C_NAK + h * HEAD_DIM:C_NAK + (h + 1) * HEAD_DIM], nak_g[...], HEAD_DIM)
        nak_o[:, sl] = k.astype(BF16)
    nav_o[...] = proj_ref[:, C_NAV:C_NAV + NA_W].astype(BF16)

    cm, sam, sbm = rope_m_ref[:, 0:128], rope_m_ref[:, 128:256], rope_m_ref[:, 256:384]

    def rope_m(y):
        return y * cm + pltpu.roll(y, 96, 1) * sam + pltpu.roll(y, 32, 1) * sbm

    mla_scale = MLA_QK_DIM ** -0.5 * LOG2E
    cq = _rms(proj_ref[:, C_CQ:C_CQ + MLA_Q_RANK], qa_g[...], MLA_Q_RANK).astype(BF16)
    qm = jnp.dot(cq, wqb_ref[...], preferred_element_type=F32)
    for h in range(MLA_HEADS):
        b = h * MLA_SLOT
        nope = _rms(qm[:, b:b + 128], qn_g[...], MLA_NOPE_DIM)
        pe = rope_m(_rms(qm[:, b + 128:b + 256], qpe_g[...], MLA_ROPE_DIM))
        mq_o[:, b:b + 128] = (nope * mla_scale).astype(BF16)
        mq_o[:, b + 128:b + 256] = (pe * mla_scale).astype(BF16)
    ckv = _rms(proj_ref[:, C_CKV:C_CKV + MLA_KV_RANK], kva_g[...], MLA_KV_RANK).astype(BF16)
    kv = jnp.dot(ckv, wkvb_ref[...], preferred_element_type=F32)
    kpe = rope_m(_rms(proj_ref[:, C_KPE:C_KPE + 128], kpe_g[...], MLA_ROPE_DIM))
    kpe = kpe.astype(BF16)
    for h in range(MLA_HEADS):
        b = h * MLA_SLOT
        kn = _rms(kv[:, b:b + 128], kn_g[...], MLA_NOPE_DIM)
        mk_o[:, b:b + 128] = kn.astype(BF16)
        mk_o[:, b + 128:b + 256] = kpe
        mvT_o[h * MLA_V_DIM:(h + 1) * MLA_V_DIM, :] = kv[:, b + 128:b + 256].T.astype(BF16)

    cd, sad, sbd = rope_d_ref[:, 0:128], rope_d_ref[:, 128:256], rope_d_ref[:, 256:384]

    def rope_d(y):
        return y * cd + pltpu.roll(y, 120, 1) * sad + pltpu.roll(y, 8, 1) * sbd

    def group_rms(x, g):
        x2 = x * x
        s_lo = jnp.sum(jnp.where(lo, x2, 0.0), axis=-1, keepdims=True)
        s_hi = jnp.sum(jnp.where(lo, 0.0, x2), axis=-1, keepdims=True)
        ms = jnp.where(lo, s_lo, s_hi) * (1.0 / DIFF_QK_DIM)
        return x * lax.rsqrt(ms + EPS) * g

    def split_components(y):
        return jnp.where(lo, y, 0.0), jnp.where(lo, pltpu.roll(y, 64, 1), 0.0)

    df_scale = DIFF_QK_DIM ** -0.5 * LOG2E
    for h in range(DIFF_HEADS):
        q = rope_d(group_rms(proj_ref[:, C_DQ + h * 128:C_DQ + (h + 1) * 128], dq_g[...])) * df_scale
        q0, q1 = split_components(q)
        dq_o[:, (2 * h) * 128:(2 * h + 1) * 128] = q0.astype(BF16)
        dq_o[:, (2 * h + 1) * 128:(2 * h + 2) * 128] = q1.astype(BF16)
        k = rope_d(group_rms(proj_ref[:, C_DK + h * 128:C_DK + (h + 1) * 128], dk_g[...]))
        k0, k1 = split_components(k)
        dk_o[:, (2 * h) * 128:(2 * h + 1) * 128] = k0.astype(BF16)
        dk_o[:, (2 * h + 1) * 128:(2 * h + 2) * 128] = k1.astype(BF16)
        v = proj_ref[:, C_DV + h * DIFF_V_DIM:C_DV + (h + 1) * DIFF_V_DIM]
        dvT_o[h * DIFF_V_DIM:(h + 1) * DIFF_V_DIM, :] = v.T.astype(BF16)


def head_prep(proj, rope_m, rope_d, lw, *, seq_len, tm):
    M = proj.shape[0]
    nt = seq_len // tm
    row = lambda w: pl.BlockSpec((tm, w), lambda i: (i, 0))
    colT = lambda h: pl.BlockSpec((h, tm), lambda i: (0, i))
    full = lambda a: pl.BlockSpec(a.shape, lambda i: (0,) * a.ndim)
    rope = pl.BlockSpec((tm, 384), lambda i: (i % nt, 0))
    params = (lw["naq_g"], lw["nak_g"], lw["qa_g"], lw["wqb"], lw["kva_g"], lw["wkvb"],
              lw["qn_g"], lw["qpe_g"], lw["kn_g"], lw["kpe_g"], lw["dq_g"], lw["dk_g"])
    out_shapes = (
        jax.ShapeDtypeStruct((M, NA_W), BF16), jax.ShapeDtypeStruct((M, NA_W), BF16),
        jax.ShapeDtypeStruct((M, NA_W), BF16),
        jax.ShapeDtypeStruct((M, MLA_HEADS * MLA_SLOT), BF16),
        jax.ShapeDtypeStruct((M, MLA_HEADS * MLA_SLOT), BF16),
        jax.ShapeDtypeStruct((MLA_VW, M), BF16),
        jax.ShapeDtypeStruct((M, 2 * DIFF_HEADS * 128), BF16),
        jax.ShapeDtypeStruct((M, 2 * DIFF_HEADS * 128), BF16),
        jax.ShapeDtypeStruct((DIFF_VW, M), BF16),
    )
    out_specs = (row(NA_W), row(NA_W), row(NA_W), row(MLA_HEADS * MLA_SLOT), row(MLA_HEADS * MLA_SLOT),
                 colT(MLA_VW), row(2 * DIFF_HEADS * 128), row(2 * DIFF_HEADS * 128), colT(DIFF_VW))
    return pl.pallas_call(
        _prep_kernel,
        grid=(M // tm,),
        in_specs=[row(IN_COLS_PAD), rope, rope] + [full(a) for a in params],
        out_specs=out_specs,
        out_shape=out_shapes,
        compiler_params=_cparams(("parallel",)),
        name="head_prep",
    )(proj, rope_m, rope_d, *params)


def _flash_kernel(q_ref, k_ref, vT_ref, o_ref, sa_ref, sb_ref, acc_ref, *, tq, tk):
    seq_len = q_ref.shape[0]
    nk = seq_len // tk
    total = (seq_len // tq) * nk
    sub = 8

    def scores(f):
        qoff = pl.multiple_of((f // nk) * tq, tq)
        koff = pl.multiple_of((f % nk) * tk, tk)
        return lax.dot_general(k_ref[pl.ds(koff, tk), :], q_ref[pl.ds(qoff, tq), :],
                               (((1,), (1,)), ((), ())), preferred_element_type=F32)

    def softmax_pv(s_ref, c, m, l):
        s = s_ref[...]
        part = jnp.max(s.reshape(tk // sub, sub, tq), axis=0)
        m_new = jnp.maximum(m, jnp.max(part, axis=0, keepdims=True))
        alpha = jnp.exp2(m - m_new)
        p = jnp.exp2(s - m_new)
        psum = jnp.sum(p.reshape(tk // sub, sub, tq), axis=0)
        koff = pl.multiple_of(c * tk, tk)
        pv = jnp.dot(vT_ref[:, pl.ds(koff, tk)], p.astype(BF16), preferred_element_type=F32)
        acc_ref[...] = alpha * acc_ref[...] + pv
        return m_new, alpha * l + psum

    def pair(i, carry):
        m, l = carry
        f = 2 * i
        c = f % nk
        m = jnp.where(c == 0, NEG_BIG, m)
        sb_ref[...] = scores(f + 1)
        m, l = softmax_pv(sa_ref, c, m, l)
        sa_ref[...] = scores(jnp.minimum(f + 2, total - 1))
        m, l = softmax_pv(sb_ref, c + 1, m, l)

        @pl.when(c + 2 == nk)
        def _():
            qoff = pl.multiple_of((f // nk) * tq, tq)
            out = acc_ref[...] / jnp.sum(l, axis=0, keepdims=True)
            o_ref[pl.ds(qoff, tq), :] = out.T.astype(o_ref.dtype)

        return m, l

    acc_ref[...] = jnp.zeros_like(acc_ref)
    sa_ref[...] = scores(0)
    init = (jnp.full((1, tq), NEG_BIG, F32), jnp.zeros((sub, tq), F32))
    lax.fori_loop(0, total // 2, pair, init)


def flash_attention(q, k, vT, *, batch, seq_len, n_heads, dq, v_of_head, tq, tk, out_dtype):
    M = q.shape[0]
    dv = 128
    assert seq_len % (2 * tk) == 0 and seq_len % tq == 0
    once = pl.Buffered(1)
    return pl.pallas_call(
        functools.partial(_flash_kernel, tq=tq, tk=tk),
        grid=(batch, n_heads),
        in_specs=[pl.BlockSpec((seq_len, dq), lambda b, h: (b, h), pipeline_mode=once),
                  pl.BlockSpec((seq_len, dq), lambda b, h: (b, h), pipeline_mode=once),
                  pl.BlockSpec((dv, seq_len), lambda b, h: (v_of_head(h), b), pipeline_mode=once)],
        out_specs=pl.BlockSpec((seq_len, dv), lambda b, h: (b, h)),
        out_shape=jax.ShapeDtypeStruct((M, n_heads * dv), out_dtype),
        scratch_shapes=[pltpu.VMEM((tk, tq), F32), pltpu.VMEM((tk, tq), F32), pltpu.VMEM((dv, tq), F32)],
        compiler_params=_cparams(("parallel", "parallel")),
        name="flash_attention",
    )(q, k, vT)


NA_QROWS = 8
NA_KROWS = 2 * NA_WIN_ROWS


def _na_kernel(q_ref, k_ref, v_ref, bias_ref, o_ref, *, rows):
    i = pl.program_id(2)
    ws = jnp.clip(i * NA_QROWS - NA_WIN_ROWS // 2, 0, rows - NA_KROWS)
    ks = pl.multiple_of(ws * GRID_W, GRID_W)
    win = NA_KROWS * GRID_W
    k = k_ref[pl.ds(ks, win), :]
    s = lax.dot_general(q_ref[...], k, (((1,), (1,)), ((), ())), preferred_element_type=F32)
    s = s + bias_ref[...]
    m = jnp.max(s, axis=-1, keepdims=True)
    p = jnp.exp2(s - m)
    l = jnp.sum(p, axis=-1, keepdims=True)
    o = jnp.dot(p.astype(BF16), v_ref[pl.ds(ks, win), :], preferred_element_type=F32)
    o_ref[...] = (o / l).astype(o_ref.dtype)


def na_attention(q, k, v, bias, *, batch, seq_len):
    M = q.shape[0]
    rows = seq_len // GRID_W
    assert rows >= NA_KROWS and rows % NA_QROWS == 0
    nblk = rows // NA_QROWS
    tq = NA_QROWS * GRID_W
    pattern = lambda i: jnp.where(i == 0, 0, jnp.where(i == nblk - 1, 2, 1))
    return pl.pallas_call(
        functools.partial(_na_kernel, rows=rows),
        grid=(batch, NA_HEADS, nblk),
        in_specs=[pl.BlockSpec((tq, HEAD_DIM), lambda b, h, i: (b * nblk + i, h)),
                  pl.BlockSpec((seq_len, HEAD_DIM), lambda b, h, i: (b, h)),
                  pl.BlockSpec((seq_len, HEAD_DIM), lambda b, h, i: (b, h)),
                  pl.BlockSpec((None, None, tq, NA_KROWS * GRID_W), lambda b, h, i: (h, pattern(i), 0, 0))],
        out_specs=pl.BlockSpec((tq, HEAD_DIM), lambda b, h, i: (b * nblk + i, h)),
        out_shape=jax.ShapeDtypeStruct((M, NA_W), BF16),
        compiler_params=_cparams(("parallel", "parallel", "arbitrary")),
        name="na_attention",
    )(q, k, v, bias)


def _out_proj_kernel(x_ref, ona_ref, omla_ref, odf_ref, subln_ref, lq1, lk1, lq2, lk2,
                     w_na, w_mla, w_df, o_ref, *, lam_init):
    lam = (jnp.exp(jnp.sum(lq1[...] * lk1[...], axis=-1, keepdims=True))
           - jnp.exp(jnp.sum(lq2[...] * lk2[...], axis=-1, keepdims=True)) + lam_init)
    heads = []
    for h in range(DIFF_HEADS):
        o1 = odf_ref[:, (2 * h) * 128:(2 * h + 1) * 128]
        o2 = odf_ref[:, (2 * h + 1) * 128:(2 * h + 2) * 128]
        o = _rms(o1 - lam * o2, subln_ref[...], DIFF_V_DIM) * (1.0 - lam_init)
        heads.append(o.astype(BF16))
    odf = jnp.concatenate(heads, axis=-1)
    y = jnp.dot(ona_ref[...], w_na[...], preferred_element_type=F32)
    y = y + jnp.dot(omla_ref[...], w_mla[...], preferred_element_type=F32)
    y = y + jnp.dot(odf, w_df[...], preferred_element_type=F32)
    o_ref[...] = x_ref[...] + y


def out_proj(x, o_na, o_mla, o_df, lw, *, lam_init, tm):
    M, D = x.shape
    row = lambda w: pl.BlockSpec((tm, w), lambda i: (i, 0))
    full = lambda a: pl.BlockSpec(a.shape, lambda i: (0,) * a.ndim)
    params = (lw["subln_g"], lw["lq1"], lw["lk1"], lw["lq2"], lw["lk2"], lw["w_out_na"], lw["w_out_mla"], lw["w_out_df"])
    return pl.pallas_call(
        functools.partial(_out_proj_kernel, lam_init=lam_init),
        grid=(M // tm,),
        in_specs=[row(D), row(NA_W), row(MLA_VW), row(2 * DIFF_HEADS * 128)] + [full(a) for a in params],
        out_specs=row(D),
        out_shape=jax.ShapeDtypeStruct((M, D), F32),
        compiler_params=_cparams(("parallel",)),
        name="out_proj",
    )(x, o_na, o_mla, o_df, *params)


HALO = 16


def _ffn_kernel(x_ref, xp_ref, xn_ref, g_ref, wg_ref, wu_ref, cw_ref, cb_ref, wd_ref, o_ref,
                xs_ref, *, seq_len, rows_per_chain):
    i = pl.program_id(0)
    f = pl.program_id(1)
    tm = x_ref.shape[0]

    @pl.when(f == 0)
    def _():
        d = x_ref.shape[-1]
        row0 = i * tm
        at_start = (row0 % seq_len) == 0
        at_end = ((row0 + tm) % seq_len) == 0
        prev = _rms(xp_ref[...], g_ref[...], d)
        nxt = _rms(xn_ref[...], g_ref[...], d)
        xs_ref[0:HALO, :] = jnp.where(at_start, 0.0, prev).astype(BF16)
        xs_ref[HALO:HALO + tm, :] = _rms(x_ref[...], g_ref[...], d).astype(BF16)
        xs_ref[HALO + tm:2 * HALO + tm, :] = jnp.where(at_end, 0.0, nxt).astype(BF16)
        o_ref[...] = x_ref[...]

    tc = rows_per_chain
    n_ext = tc + 2 * HALO
    chains = range(0, tm, tc)
    gates = [jnp.dot(xs_ref[r0:r0 + n_ext, :], wg_ref[...], preferred_element_type=F32) for r0 in chains]
    ups = [jnp.dot(xs_ref[r0 + HALO:r0 + HALO + tc, :], wu_ref[...], preferred_element_type=F32)
           for r0 in chains]
    for r0, gate, u in zip(chains, gates, ups):
        g_prev = pltpu.roll(gate, 1, 0)[HALO:HALO + tc, :]
        g_next = pltpu.roll(gate, n_ext - 1, 0)[HALO:HALO + tc, :]
        g_mid = gate[HALO:HALO + tc, :]
        g = g_prev * cw_ref[0:1, :] + g_mid * cw_ref[1:2, :] + g_next * cw_ref[2:3, :] + cb_ref[...]
        a = (g * jax.nn.sigmoid(g) * u).astype(BF16)
        o_ref[r0:r0 + tc, :] += jnp.dot(a, wd_ref[...], preferred_element_type=F32)


def ffn(x, lw, *, seq_len, tm):
    M, D = x.shape
    nf, _, tf = lw["w_gate"].shape
    assert seq_len % tm == 0 and tm % HALO == 0
    hb = tm // HALO
    last = M // HALO - 1
    return pl.pallas_call(
        functools.partial(_ffn_kernel, seq_len=seq_len, rows_per_chain=min(tm, 512)),
        grid=(M // tm, nf),
        in_specs=[pl.BlockSpec((tm, D), lambda i, f: (i, 0), pipeline_mode=pl.Buffered(1)),
                  pl.BlockSpec((HALO, D), lambda i, f: (jnp.maximum(i * hb - 1, 0), 0)),
                  pl.BlockSpec((HALO, D), lambda i, f: (jnp.minimum((i + 1) * hb, last), 0)),
                  pl.BlockSpec((1, D), lambda i, f: (0, 0)),
                  pl.BlockSpec((None, D, tf), lambda i, f: (f, 0, 0)),
                  pl.BlockSpec((None, D, tf), lambda i, f: (f, 0, 0)),
                  pl.BlockSpec((3, tf), lambda i, f: (0, f)),
                  pl.BlockSpec((1, tf), lambda i, f: (0, f)),
                  pl.BlockSpec((tf, D), lambda i, f: (f, 0))],
        out_specs=pl.BlockSpec((tm, D), lambda i, f: (i, 0)),
        out_shape=jax.ShapeDtypeStruct((M, D), F32),
        scratch_shapes=[pltpu.VMEM((tm + 2 * HALO, D), BF16)],
        compiler_params=_cparams(("parallel", "arbitrary")),
        name="ffn",
    )(x, x, x, lw["ffn_g"], lw["w_gate"], lw["w_up"], lw["conv_w"], lw["conv_b"], lw["w_down"])


def _ple_kernel(x_ref, p_ref, g_ref, wg_ref, wp_ref, o_ref):
    x = x_ref[...]
    xn = _rms(x, g_ref[...], x.shape[-1]).astype(BF16)
    gate = jax.nn.sigmoid(jnp.dot(xn, wg_ref[...], preferred_element_type=F32))
    proj = jnp.dot(p_ref[...].astype(BF16), wp_ref[...], preferred_element_type=F32)
    o_ref[...] = x + gate * proj


def ple(x, p, lw, *, tm):
    M, D = x.shape
    row = lambda w: pl.BlockSpec((tm, w), lambda i: (i, 0))
    full = lambda a: pl.BlockSpec(a.shape, lambda i: (0,) * a.ndim)
    params = (lw["ple_g"], lw["w_ple_gate"], lw["w_ple_proj"])
    return pl.pallas_call(
        _ple_kernel,
        grid=(M // tm,),
        in_specs=[row(D), row(p.shape[1])] + [full(a) for a in params],
        out_specs=row(D),
        out_shape=jax.ShapeDtypeStruct((M, D), F32),
        compiler_params=_cparams(("parallel",)),
        name="ple",
    )(x, p, *params)


def _rope_tables(seq_len):
    def angles(dim, theta):
        inv = 1.0 / (theta ** (jnp.arange(0, dim, 2, dtype=F32) / dim))
        ang = jnp.arange(seq_len, dtype=F32)[:, None] * inv[None, :]
        return jnp.cos(ang), jnp.sin(ang)

    z = lambda w: jnp.zeros((seq_len, w), F32)
    cos, sin = angles(MLA_ROPE_DIM, MLA_ROPE_THETA)
    half = MLA_ROPE_DIM // 2
    cm = jnp.concatenate([cos, cos, z(64)], axis=1)
    sam = jnp.concatenate([-sin, z(half), z(64)], axis=1)
    sbm = jnp.concatenate([z(half), sin, z(64)], axis=1)
    rope_m = jnp.concatenate([cm, sam, sbm], axis=1)

    cos, sin = angles(ROPE_PART_DIM, ROPE_THETA)
    half = ROPE_PART_DIM // 2
    rest = DIFF_QK_DIM - ROPE_PART_DIM
    c64 = jnp.concatenate([cos, cos, jnp.ones((seq_len, rest), F32)], axis=1)
    sa64 = jnp.concatenate([-sin, z(half), z(rest)], axis=1)
    sb64 = jnp.concatenate([z(half), sin, z(rest)], axis=1)
    rope_d = jnp.concatenate([c64, c64, sa64, sa64, sb64, sb64], axis=1)
    return rope_m, rope_d


def _na_bias_table(rpb):
    cols = jnp.arange(GRID_W)
    cs = jnp.clip(cols - NA_WIN_COLS // 2, 0, GRID_W - NA_WIN_COLS)
    kc = jnp.arange(GRID_W)
    valid = (kc[None, :] >= cs[:, None]) & (kc[None, :] < cs[:, None] + NA_WIN_COLS)
    nv = 2 * NA_WIN_COLS - 1
    onehot = (kc[None, None, :] - cols[None, :, None] + (NA_WIN_COLS - 1) == jnp.arange(nv)[:, None, None])
    toep = jnp.einsum("huv,vck->huck", rpb.astype(F32), onehot.astype(F32), precision=lax.Precision.HIGHEST)
    toep = jnp.where(valid[None, None], toep * LOG2E, NEG_BIG)
    n_h = rpb.shape[0]
    masked = 2 * NA_WIN_ROWS - 1
    toep = jnp.concatenate([toep, jnp.full((n_h, 1, GRID_W, GRID_W), NEG_BIG, F32)], axis=1)
    half = NA_WIN_ROWS // 2
    tile = np.full((3, NA_QROWS, NA_KROWS), masked, np.int32)
    for pat, delta in enumerate((0, -half, -NA_WIN_ROWS)):
        for a in range(NA_QROWS):
            first = (max(a - half, 0), a, min(a + half, NA_WIN_ROWS))[pat]
            for i in range(first, first + NA_WIN_ROWS):
                tile[pat, a, i] = delta + i - a + (NA_WIN_ROWS - 1)
    tab = jnp.take(toep, jnp.asarray(tile.reshape(-1)), axis=1)
    tab = tab.reshape(n_h, 3, NA_QROWS, NA_KROWS, GRID_W, GRID_W).transpose(0, 1, 2, 4, 3, 5)
    return tab.reshape(n_h, 3, NA_QROWS * GRID_W, NA_KROWS * GRID_W)


def _layer_params(i, norm_mix, w_in, na_q_norm, na_k_norm, na_rpb, mla_q_a_norm, mla_w_q_b, mla_kv_a_norm,
                  mla_w_kv_b, mla_q_nope_norm, mla_q_pe_norm, mla_k_nope_norm, mla_k_pe_norm,
                  diff_q_norm, diff_k_norm, diff_lambda_q1, diff_lambda_k1, diff_lambda_q2, diff_lambda_k2,
                  diff_subln, w_out, norm_ffn, w_gate, w_up, conv_w, conv_b, w_down,
                  ple_norm, w_ple_gate, w_ple_proj):
    r = lambda a: a[i].reshape(1, -1).astype(F32)
    pad_to = lambda a, n: jnp.pad(a, ((0, 0), (0, n - a.shape[1])))
    w = w_in[i]
    kpe0 = 3 * NA_W + MLA_Q_RANK + MLA_KV_RANK
    w_perm = jnp.concatenate([w[:, :kpe0], w[:, kpe0 + MLA_ROPE_DIM:], w[:, kpe0:kpe0 + MLA_ROPE_DIM],
                              jnp.zeros((w.shape[0], 128 - MLA_ROPE_DIM), w.dtype)], axis=1).astype(BF16)
    wqb = mla_w_q_b[i].reshape(MLA_Q_RANK, MLA_HEADS, MLA_QK_DIM)
    wqb = jnp.pad(wqb, ((0, 0), (0, 0), (0, MLA_SLOT - MLA_QK_DIM))).reshape(MLA_Q_RANK, MLA_HEADS * MLA_SLOT)
    wo = w_out[i].astype(BF16)
    return dict(
        mix_g=r(norm_mix), w_in=_col_tiles(w_perm, IN_TILE),
        naq_g=r(na_q_norm), nak_g=r(na_k_norm), na_bias=_na_bias_table(na_rpb[i]),
        qa_g=r(mla_q_a_norm), wqb=wqb.astype(BF16), kva_g=r(mla_kv_a_norm), wkvb=mla_w_kv_b[i].astype(BF16),
        qn_g=r(mla_q_nope_norm), qpe_g=pad_to(r(mla_q_pe_norm), 128),
        kn_g=r(mla_k_nope_norm), kpe_g=pad_to(r(mla_k_pe_norm), 128),
        dq_g=jnp.tile(r(diff_q_norm), (1, 2)), dk_g=jnp.tile(r(diff_k_norm), (1, 2)),
        lq1=r(diff_lambda_q1), lk1=r(diff_lambda_k1), lq2=r(diff_lambda_q2), lk2=r(diff_lambda_k2),
        subln_g=r(diff_subln),
        w_out_na=wo[:NA_W], w_out_mla=wo[NA_W:NA_W + MLA_VW], w_out_df=wo[NA_W + MLA_VW:],
        ffn_g=r(norm_ffn), w_gate=_col_tiles(w_gate[i].astype(BF16), FF_TILE),
        w_up=_col_tiles(w_up[i].astype(BF16), FF_TILE),
        conv_w=conv_w[i].astype(F32), conv_b=r(conv_b), w_down=w_down[i].astype(BF16),
        ple_g=r(ple_norm), w_ple_gate=w_ple_gate[i].astype(BF16), w_ple_proj=w_ple_proj[i].astype(BF16),
    )


def _tile(n, pref):
    t = min(pref, n)
    while n % t:
        t //= 2
    return t


def _encoder_layer(x, p_l, lw, rope_m, rope_d, *, layer_idx, batch, seq_len):
    tm = _tile(seq_len, 512)
    proj = rms_matmul(x, lw["mix_g"], lw["w_in"], tm=_tile(seq_len, 1024), out_dtype=F32)
    (na_q, na_k, na_v, m_q, m_k, m_vT, d_q, d_k, d_vT) = head_prep(
        proj, rope_m, rope_d, lw, seq_len=seq_len, tm=_tile(seq_len, 256))
    o_na = na_attention(na_q, na_k, na_v, lw["na_bias"], batch=batch, seq_len=seq_len)
    tq = _tile(seq_len, 512)
    tk = _tile(seq_len // 2, 2048)
    o_mla = flash_attention(m_q, m_k, m_vT, batch=batch, seq_len=seq_len, n_heads=MLA_HEADS, dq=MLA_SLOT,
                            v_of_head=lambda h: h, tq=tq, tk=tk, out_dtype=BF16)
    o_df = flash_attention(d_q, d_k, d_vT, batch=batch, seq_len=seq_len, n_heads=2 * DIFF_HEADS, dq=128,
                           v_of_head=lambda h: h // 2, tq=tq, tk=tk, out_dtype=F32)
    lam_init = 0.8 - 0.6 * math.exp(-0.3 * layer_idx)
    x = out_proj(x, o_na, o_mla, o_df, lw, lam_init=lam_init, tm=tm)
    x = ffn(x, lw, seq_len=seq_len, tm=tm)
    x = ple(x, p_l, lw, tm=tm)
    return x


def kernel(x_prompt, x_sample, p_prompt, p_sample, norm_mix, w_in, na_q_norm, na_k_norm, na_rpb, mla_q_a_norm, mla_w_q_b, mla_kv_a_norm, mla_w_kv_b, mla_q_nope_norm, mla_q_pe_norm, mla_k_nope_norm, mla_k_pe_norm, diff_q_norm, diff_k_norm, diff_lambda_q1, diff_lambda_k1, diff_lambda_q2, diff_lambda_k2, diff_subln, w_out, norm_ffn, w_gate, w_up, conv_w, conv_b, w_down, ple_norm, w_ple_gate, w_ple_proj):
    weights = (norm_mix, w_in, na_q_norm, na_k_norm, na_rpb, mla_q_a_norm, mla_w_q_b, mla_kv_a_norm, mla_w_kv_b,
               mla_q_nope_norm, mla_q_pe_norm, mla_k_nope_norm, mla_k_pe_norm, diff_q_norm, diff_k_norm,
               diff_lambda_q1, diff_lambda_k1, diff_lambda_q2, diff_lambda_k2, diff_subln, w_out,
               norm_ffn, w_gate, w_up, conv_w, conv_b, w_down, ple_norm, w_ple_gate, w_ple_proj)
    depth = norm_mix.shape[0]
    groups = []
    for x, p in ((x_prompt, p_prompt), (x_sample, p_sample)):
        b, t, d = x.shape
        groups.append(dict(x=x.reshape(b * t, d), p=p.reshape(depth, b * t, p.shape[-1]), batch=b, seq_len=t,
                           rope=_rope_tables(t), shape=x.shape))
    for i in range(depth):
        lw = _layer_params(i, *weights)
        for g in groups:
            g["x"] = _encoder_layer(g["x"], g["p"][i], lw, *g["rope"], layer_idx=i,
                                    batch=g["batch"], seq_len=g["seq_len"])
    return tuple(g["x"].reshape(g["shape"]) for g in groups)
```

```python
import functools
import math

import jax
import jax.numpy as jnp
import numpy as np
from jax import lax
from jax.experimental import pallas as pl
from jax.experimental.pallas import tpu as pltpu

F32 = jnp.float32
BF16 = jnp.bfloat16

EPS = 1e-6
LOG2E = 1.4426950408889634
NEG_BIG = -1e30

D_MODEL = 2048
PLE_DIM = 256
GRID_W = 64
HEAD_DIM = 128
NA_HEADS = 6
NA_WIN_ROWS = 8
NA_WIN_COLS = 16
MLA_HEADS = 5
MLA_Q_RANK = 512
MLA_KV_RANK = 256
MLA_NOPE_DIM = 128
MLA_ROPE_DIM = 64
MLA_V_DIM = 128
MLA_ROPE_THETA = 10000.0
DIFF_HEADS = 5
DIFF_QK_DIM = 64
DIFF_V_DIM = 128
ROPE_THETA = 500000.0
ROPE_PART_DIM = DIFF_QK_DIM // 4
D_FF = 5632

NA_W = NA_HEADS * HEAD_DIM
MLA_QK_DIM = MLA_NOPE_DIM + MLA_ROPE_DIM
MLA_SLOT = 256
DIFF_W = DIFF_HEADS * 2 * DIFF_QK_DIM
DIFF_VW = DIFF_HEADS * DIFF_V_DIM
MLA_VW = MLA_HEADS * MLA_V_DIM
VT_ROWS = 128 + 16

C_NAQ = 0
C_NAK = C_NAQ + NA_W
C_NAV = C_NAK + NA_W
C_CQ = C_NAV + NA_W
C_CKV = C_CQ + MLA_Q_RANK
C_DQ = C_CKV + MLA_KV_RANK
C_DK = C_DQ + DIFF_W
C_DV = C_DK + DIFF_W
C_KPE = C_DV + DIFF_VW
IN_COLS_PAD = C_KPE + 128

LANES = 128
IN_TILE = 1024
FF_TILE = 512
VMEM_LIMIT = 56 * 1024 * 1024


def _cparams(sem):
    return pltpu.CompilerParams(dimension_semantics=sem, vmem_limit_bytes=VMEM_LIMIT)


def _rms(x, g, n):
    ms = jnp.sum(x * x, axis=-1, keepdims=True) * (1.0 / n)
    return x * lax.rsqrt(ms + EPS) * g


def _rms_matmul_kernel(x_ref, g_ref, w_ref, o_ref, xn_ref):
    @pl.when(pl.program_id(1) == 0)
    def _():
        xn_ref[...] = _rms(x_ref[...], g_ref[...], x_ref.shape[-1]).astype(BF16)

    o_ref[...] = jnp.dot(xn_ref[...], w_ref[...], preferred_element_type=F32).astype(o_ref.dtype)


def _col_tiles(w, tn):
    k, n = w.shape
    return w.reshape(k, n // tn, tn).transpose(1, 0, 2)


def rms_matmul(x, g, w, *, tm, out_dtype):
    M, K = x.shape
    nt, _, tn = w.shape
    N = nt * tn
    return pl.pallas_call(
        _rms_matmul_kernel,
        grid=(M // tm, nt),
        in_specs=[pl.BlockSpec((tm, K), lambda i, j: (i, 0)),
                  pl.BlockSpec((1, K), lambda i, j: (0, 0)),
                  pl.BlockSpec((None, K, tn), lambda i, j: (j, 0, 0))],
        out_specs=pl.BlockSpec((tm, tn), lambda i, j: (i, j)),
        out_shape=jax.ShapeDtypeStruct((M, N), out_dtype),
        scratch_shapes=[pltpu.VMEM((tm, K), BF16)],
        compiler_params=_cparams(("parallel", "arbitrary")),
        name="rms_matmul",
    )(x, g, w)


def _prep_kernel(proj_ref, rope_m_ref, rope_d_ref, naq_g, nak_g, qa_g, wqb_ref, kva_g, wkvb_ref,
                 qn_g, qpe_g, kn_g, kpe_g, dq_g, dk_g,
                 naq_o, nak_o, nav_o, mq_o, mk_o, mvT_o, dq_o, dk_o, dvT_o):
    tm = proj_ref.shape[0]
    lane = lax.broadcasted_iota(jnp.int32, (tm, LANES), 1)
    lo = lane < 64
    ones_row = (lax.broadcasted_iota(jnp.int32, (VT_ROWS - 128, tm), 0) == 0).astype(BF16)

    na_scale = HEAD_DIM ** -0.5 * LOG2E
    for h in range(NA_HEADS):
        sl = slice(h * HEAD_DIM, (h + 1) * HEAD_DIM)
        q = _rms(proj_ref[:, C_NAQ + h * HEAD_DIM:C_NAQ + (h + 1) * HEAD_DIM], naq_g[...], HEAD_DIM)
        naq_o[:, sl] = (q * na_scale).astype(BF16)
        k = _rms(proj_ref[:, C_NAK + h * HEAD_DIM:C_NAK + (h + 1) * HEAD_DIM], nak_g[...], HEAD_DIM)
        nak_o[:, sl] = k.astype(BF16)
    nav_o[...] = proj_ref[:, C_NAV:C_NAV + NA_W].astype(BF16)

    cm, sam, sbm = rope_m_ref[:, 0:128], rope_m_ref[:, 128:256], rope_m_ref[:, 256:384]

    def rope_m(y):
        return y * cm + pltpu.roll(y, 96, 1) * sam + pltpu.roll(y, 32, 1) * sbm

    mla_scale = MLA_QK_DIM ** -0.5 * LOG2E
    cq = _rms(proj_ref[:, C_CQ:C_CQ + MLA_Q_RANK], qa_g[...], MLA_Q_RANK).astype(BF16)
    qm = jnp.dot(cq, wqb_ref[...], preferred_element_type=F32)
    for h in range(MLA_HEADS):
        b = h * MLA_SLOT
        nope = _rms(qm[:, b:b + 128], qn_g[...], MLA_NOPE_DIM)
        pe = rope_m(_rms(qm[:, b + 128:b + 256], qpe_g[...], MLA_ROPE_DIM))
        mq_o[:, b:b + 128] = (nope * mla_scale).astype(BF16)
        mq_o[:, b + 128:b + 256] = (pe * mla_scale).astype(BF16)
    ckv = _rms(proj_ref[:, C_CKV:C_CKV + MLA_KV_RANK], kva_g[...], MLA_KV_RANK).astype(BF16)
    kv = jnp.dot(ckv, wkvb_ref[...], preferred_element_type=F32)
    kpe = rope_m(_rms(proj_ref[:, C_KPE:C_KPE + 128], kpe_g[...], MLA_ROPE_DIM))
    kpe = kpe.astype(BF16)
    for h in range(MLA_HEADS):
        b = h * MLA_SLOT
        kn = _rms(kv[:, b:b + 128], kn_g[...], MLA_NOPE_DIM)
        mk_o[:, b:b + 128] = kn.astype(BF16)
        mk_o[:, b + 128:b + 256] = kpe
        mvT_o[h * VT_ROWS:h * VT_ROWS + MLA_V_DIM, :] = kv[:, b + 128:b + 256].T.astype(BF16)
        mvT_o[h * VT_ROWS + MLA_V_DIM:(h + 1) * VT_ROWS, :] = ones_row

    cd, sad, sbd = rope_d_ref[:, 0:128], rope_d_ref[:, 128:256], rope_d_ref[:, 256:384]

    def rope_d(y):
        return y * cd + pltpu.roll(y, 120, 1) * sad + pltpu.roll(y, 8, 1) * sbd

    def group_rms(x, g):
        x2 = x * x
        s_lo = jnp.sum(jnp.where(lo, x2, 0.0), axis=-1, keepdims=True)
        s_hi = jnp.sum(jnp.where(lo, 0.0, x2), axis=-1, keepdims=True)
        ms = jnp.where(lo, s_lo, s_hi) * (1.0 / DIFF_QK_DIM)
        return x * lax.rsqrt(ms + EPS) * g

    def split_components(y):
        return jnp.where(lo, y, 0.0), jnp.where(lo, pltpu.roll(y, 64, 1), 0.0)

    df_scale = DIFF_QK_DIM ** -0.5 * LOG2E
    for h in range(DIFF_HEADS):
        q = rope_d(group_rms(proj_ref[:, C_DQ + h * 128:C_DQ + (h + 1) * 128], dq_g[...])) * df_scale
        q0, q1 = split_components(q)
        dq_o[:, (2 * h) * 128:(2 * h + 1) * 128] = q0.astype(BF16)
        dq_o[:, (2 * h + 1) * 128:(2 * h + 2) * 128] = q1.astype(BF16)
        k = rope_d(group_rms(proj_ref[:, C_DK + h * 128:C_DK + (h + 1) * 128], dk_g[...]))
        k0, k1 = split_components(k)
        dk_o[:, (2 * h) * 128:(2 * h + 1) * 128] = k0.astype(BF16)
        dk_o[:, (2 * h + 1) * 128:(2 * h + 2) * 128] = k1.astype(BF16)
        v = proj_ref[:, C_DV + h * DIFF_V_DIM:C_DV + (h + 1) * DIFF_V_DIM]
        dvT_o[h * VT_ROWS:h * VT_ROWS + DIFF_V_DIM, :] = v.T.astype(BF16)
        dvT_o[h * VT_ROWS + DIFF_V_DIM:(h + 1) * VT_ROWS, :] = ones_row


def head_prep(proj, rope_m, rope_d, lw, *, seq_len, tm):
    M = proj.shape[0]
    nt = seq_len // tm
    row = lambda w: pl.BlockSpec((tm, w), lambda i: (i, 0))
    colT = lambda h: pl.BlockSpec((h, tm), lambda i: (0, i))
    full = lambda a: pl.BlockSpec(a.shape, lambda i: (0,) * a.ndim)
    rope = pl.BlockSpec((tm, 384), lambda i: (i % nt, 0))
    params = (lw["naq_g"], lw["nak_g"], lw["qa_g"], lw["wqb"], lw["kva_g"], lw["wkvb"],
              lw["qn_g"], lw["qpe_g"], lw["kn_g"], lw["kpe_g"], lw["dq_g"], lw["dk_g"])
    out_shapes = (
        jax.ShapeDtypeStruct((M, NA_W), BF16), jax.ShapeDtypeStruct((M, NA_W), BF16),
        jax.ShapeDtypeStruct((M, NA_W), BF16),
        jax.ShapeDtypeStruct((M, MLA_HEADS * MLA_SLOT), BF16),
        jax.ShapeDtypeStruct((M, MLA_HEADS * MLA_SLOT), BF16),
        jax.ShapeDtypeStruct((MLA_HEADS * VT_ROWS, M), BF16),
        jax.ShapeDtypeStruct((M, 2 * DIFF_HEADS * 128), BF16),
        jax.ShapeDtypeStruct((M, 2 * DIFF_HEADS * 128), BF16),
        jax.ShapeDtypeStruct((DIFF_HEADS * VT_ROWS, M), BF16),
    )
    out_specs = (row(NA_W), row(NA_W), row(NA_W), row(MLA_HEADS * MLA_SLOT), row(MLA_HEADS * MLA_SLOT),
                 colT(MLA_HEADS * VT_ROWS), row(2 * DIFF_HEADS * 128), row(2 * DIFF_HEADS * 128),
                 colT(DIFF_HEADS * VT_ROWS))
    return pl.pallas_call(
        _prep_kernel,
        grid=(M // tm,),
        in_specs=[row(IN_COLS_PAD), rope, rope] + [full(a) for a in params],
        out_specs=out_specs,
        out_shape=out_shapes,
        compiler_params=_cparams(("parallel",)),
        name="head_prep",
    )(proj, rope_m, rope_d, *params)


def _flash_kernel(q_ref, k_ref, vT_ref, o_ref, sa_ref, sb_ref, acc_ref, *, tq, tk):
    seq_len = q_ref.shape[0]
    dv = o_ref.shape[1]
    nk = seq_len // tk
    total = (seq_len // tq) * nk
    sub = 16

    def scores(f):
        qoff = pl.multiple_of((f // nk) * tq, tq)
        koff = pl.multiple_of((f % nk) * tk, tk)
        s = lax.dot_general(k_ref[pl.ds(koff, tk), :], q_ref[pl.ds(qoff, tq), :],
                            (((1,), (1,)), ((), ())), preferred_element_type=F32)
        return s.astype(BF16)

    def softmax_pv(s_ref, c, m):
        s = s_ref[...]
        part = jnp.max(s.reshape(tk // sub, sub, tq), axis=0).astype(F32)
        m_new = jnp.maximum(m, jnp.max(part, axis=0, keepdims=True))
        alpha = jnp.exp2(m - m_new)
        p = jnp.exp2(s - m_new.astype(BF16))
        koff = pl.multiple_of(c * tk, tk)
        pv = jnp.dot(vT_ref[:, pl.ds(koff, tk)], p, preferred_element_type=F32)
        acc_ref[...] = alpha * acc_ref[...] + pv
        return m_new

    def pair(i, m):
        f = 2 * i
        c = f % nk
        m = jnp.where(c == 0, NEG_BIG, m)
        sb_ref[...] = scores(f + 1)
        m = softmax_pv(sa_ref, c, m)
        sa_ref[...] = scores(jnp.minimum(f + 2, total - 1))
        m = softmax_pv(sb_ref, c + 1, m)

        @pl.when(c + 2 == nk)
        def _():
            qoff = pl.multiple_of((f // nk) * tq, tq)
            out = acc_ref[0:dv, :] / acc_ref[dv:dv + 1, :]
            o_ref[pl.ds(qoff, tq), :] = out.T.astype(o_ref.dtype)

        return m

    acc_ref[...] = jnp.zeros_like(acc_ref)
    sa_ref[...] = scores(0)
    lax.fori_loop(0, total // 2, pair, jnp.full((1, tq), NEG_BIG, F32))


def flash_attention(q, k, vT, *, batch, seq_len, n_heads, dq, v_of_head, tq, tk, out_dtype):
    M = q.shape[0]
    dv = 128
    assert seq_len % (2 * tk) == 0 and seq_len % tq == 0
    once = pl.Buffered(1)
    return pl.pallas_call(
        functools.partial(_flash_kernel, tq=tq, tk=tk),
        grid=(batch, n_heads),
        in_specs=[pl.BlockSpec((seq_len, dq), lambda b, h: (b, h), pipeline_mode=once),
                  pl.BlockSpec((seq_len, dq), lambda b, h: (b, h), pipeline_mode=once),
                  pl.BlockSpec((VT_ROWS, seq_len), lambda b, h: (v_of_head(h), b), pipeline_mode=once)],
        out_specs=pl.BlockSpec((seq_len, dv), lambda b, h: (b, h)),
        out_shape=jax.ShapeDtypeStruct((M, n_heads * dv), out_dtype),
        scratch_shapes=[pltpu.VMEM((tk, tq), BF16), pltpu.VMEM((tk, tq), BF16), pltpu.VMEM((VT_ROWS, tq), F32)],
        compiler_params=_cparams(("parallel", "parallel")),
        name="flash_attention",
    )(q, k, vT)


NA_QROWS = 8
NA_KROWS = 2 * NA_WIN_ROWS


def _na_kernel(q_ref, k_ref, v_ref, bias_ref, o_ref, *, rows):
    i = pl.program_id(2)
    ws = jnp.clip(i * NA_QROWS - NA_WIN_ROWS // 2, 0, rows - NA_KROWS)
    ks = pl.multiple_of(ws * GRID_W, GRID_W)
    win = NA_KROWS * GRID_W
    k = k_ref[pl.ds(ks, win), :]
    s = lax.dot_general(q_ref[...], k, (((1,), (1,)), ((), ())), preferred_element_type=F32)
    s = s + bias_ref[...]
    m = jnp.max(s, axis=-1, keepdims=True)
    p = jnp.exp2(s - m)
    l = jnp.sum(p, axis=-1, keepdims=True)
    o = jnp.dot(p.astype(BF16), v_ref[pl.ds(ks, win), :], preferred_element_type=F32)
    o_ref[...] = (o / l).astype(o_ref.dtype)


def na_attention(q, k, v, bias, *, batch, seq_len):
    M = q.shape[0]
    rows = seq_len // GRID_W
    assert rows >= NA_KROWS and rows % NA_QROWS == 0
    nblk = rows // NA_QROWS
    tq = NA_QROWS * GRID_W
    pattern = lambda i: jnp.where(i == 0, 0, jnp.where(i == nblk - 1, 2, 1))
    return pl.pallas_call(
        functools.partial(_na_kernel, rows=rows),
        grid=(batch, NA_HEADS, nblk),
        in_specs=[pl.BlockSpec((tq, HEAD_DIM), lambda b, h, i: (b * nblk + i, h)),
                  pl.BlockSpec((seq_len, HEAD_DIM), lambda b, h, i: (b, h)),
                  pl.BlockSpec((seq_len, HEAD_DIM), lambda b, h, i: (b, h)),
                  pl.BlockSpec((None, None, tq, NA_KROWS * GRID_W), lambda b, h, i: (h, pattern(i), 0, 0))],
        out_specs=pl.BlockSpec((tq, HEAD_DIM), lambda b, h, i: (b * nblk + i, h)),
        out_shape=jax.ShapeDtypeStruct((M, NA_W), BF16),
        compiler_params=_cparams(("parallel", "parallel", "arbitrary")),
        name="na_attention",
    )(q, k, v, bias)


def _out_proj_kernel(x_ref, ona_ref, omla_ref, odf_ref, subln_ref, lq1, lk1, lq2, lk2,
                     w_na, w_mla, w_df, o_ref, *, lam_init):
    lam = (jnp.exp(jnp.sum(lq1[...] * lk1[...], axis=-1, keepdims=True))
           - jnp.exp(jnp.sum(lq2[...] * lk2[...], axis=-1, keepdims=True)) + lam_init)
    heads = []
    for h in range(DIFF_HEADS):
        o1 = odf_ref[:, (2 * h) * 128:(2 * h + 1) * 128]
        o2 = odf_ref[:, (2 * h + 1) * 128:(2 * h + 2) * 128]
        o = _rms(o1 - lam * o2, subln_ref[...], DIFF_V_DIM) * (1.0 - lam_init)
        heads.append(o.astype(BF16))
    odf = jnp.concatenate(heads, axis=-1)
    y = jnp.dot(ona_ref[...], w_na[...], preferred_element_type=F32)
    y = y + jnp.dot(omla_ref[...], w_mla[...], preferred_element_type=F32)
    y = y + jnp.dot(odf, w_df[...], preferred_element_type=F32)
    o_ref[...] = x_ref[...] + y


def out_proj(x, o_na, o_mla, o_df, lw, *, lam_init, tm):
    M, D = x.shape
    row = lambda w: pl.BlockSpec((tm, w), lambda i: (i, 0))
    full = lambda a: pl.BlockSpec(a.shape, lambda i: (0,) * a.ndim)
    params = (lw["subln_g"], lw["lq1"], lw["lk1"], lw["lq2"], lw["lk2"], lw["w_out_na"], lw["w_out_mla"], lw["w_out_df"])
    return pl.pallas_call(
        functools.partial(_out_proj_kernel, lam_init=lam_init),
        grid=(M // tm,),
        in_specs=[row(D), row(NA_W), row(MLA_VW), row(2 * DIFF_HEADS * 128)] + [full(a) for a in params],
        out_specs=row(D),
        out_shape=jax.ShapeDtypeStruct((M, D), F32),
        compiler_params=_cparams(("parallel",)),
        name="out_proj",
    )(x, o_na, o_mla, o_df, *params)


HALO = 16


def _ffn_kernel(x_ref, xp_ref, xn_ref, g_ref, wg_ref, wu_ref, cw_ref, cb_ref, wd_ref, o_ref,
                xs_ref, acc_ref, *, seq_len):
    i = pl.program_id(0)
    f = pl.program_id(1)
    tm = x_ref.shape[0]

    @pl.when(f == 0)
    def _():
        d = x_ref.shape[-1]
        row0 = i * tm
        at_start = (row0 % seq_len) == 0
        at_end = ((row0 + tm) % seq_len) == 0
        prev = _rms(xp_ref[...], g_ref[...], d)
        nxt = _rms(xn_ref[...], g_ref[...], d)
        xs_ref[0:HALO, :] = jnp.where(at_start, 0.0, prev).astype(BF16)
        xs_ref[HALO:HALO + tm, :] = _rms(x_ref[...], g_ref[...], d).astype(BF16)
        xs_ref[HALO + tm:2 * HALO + tm, :] = jnp.where(at_end, 0.0, nxt).astype(BF16)
        acc_ref[...] = jnp.zeros_like(acc_ref)

    n_ext = tm + 2 * HALO
    gate = jnp.dot(xs_ref[...], wg_ref[...], preferred_element_type=F32)
    g_prev = pltpu.roll(gate, 1, 0)[HALO:HALO + tm, :]
    g_next = pltpu.roll(gate, n_ext - 1, 0)[HALO:HALO + tm, :]
    g_mid = gate[HALO:HALO + tm, :]
    g = g_prev * cw_ref[0:1, :] + g_mid * cw_ref[1:2, :] + g_next * cw_ref[2:3, :] + cb_ref[...]
    u = jnp.dot(xs_ref[HALO:HALO + tm, :], wu_ref[...], preferred_element_type=F32)
    a = (g * jax.nn.sigmoid(g) * u).astype(BF16)
    acc_ref[...] += jnp.dot(a, wd_ref[...], preferred_element_type=F32)

    @pl.when(f == pl.num_programs(1) - 1)
    def _():
        o_ref[...] = x_ref[...] + acc_ref[...]


def ffn(x, lw, *, seq_len, tm):
    M, D = x.shape
    nf, _, tf = lw["w_gate"].shape
    assert seq_len % tm == 0 and tm % HALO == 0
    hb = tm // HALO
    last = M // HALO - 1
    return pl.pallas_call(
        functools.partial(_ffn_kernel, seq_len=seq_len),
        grid=(M // tm, nf),
        in_specs=[pl.BlockSpec((tm, D), lambda i, f: (i, 0)),
                  pl.BlockSpec((HALO, D), lambda i, f: (jnp.maximum(i * hb - 1, 0), 0)),
                  pl.BlockSpec((HALO, D), lambda i, f: (jnp.minimum((i + 1) * hb, last), 0)),
                  pl.BlockSpec((1, D), lambda i, f: (0, 0)),
                  pl.BlockSpec((None, D, tf), lambda i, f: (f, 0, 0)),
                  pl.BlockSpec((None, D, tf), lambda i, f: (f, 0, 0)),
                  pl.BlockSpec((3, tf), lambda i, f: (0, f)),
                  pl.BlockSpec((1, tf), lambda i, f: (0, f)),
                  pl.BlockSpec((tf, D), lambda i, f: (f, 0))],
        out_specs=pl.BlockSpec((tm, D), lambda i, f: (i, 0)),
        out_shape=jax.ShapeDtypeStruct((M, D), F32),
        scratch_shapes=[pltpu.VMEM((tm + 2 * HALO, D), BF16), pltpu.VMEM((tm, D), F32)],
        compiler_params=_cparams(("parallel", "arbitrary")),
        name="ffn",
    )(x, x, x, lw["ffn_g"], lw["w_gate"], lw["w_up"], lw["conv_w"], lw["conv_b"], lw["w_down"])


def _ple_kernel(x_ref, p_ref, g_ref, wg_ref, wp_ref, o_ref):
    x = x_ref[...]
    xn = _rms(x, g_ref[...], x.shape[-1]).astype(BF16)
    gate = jax.nn.sigmoid(jnp.dot(xn, wg_ref[...], preferred_element_type=F32))
    proj = jnp.dot(p_ref[...].astype(BF16), wp_ref[...], preferred_element_type=F32)
    o_ref[...] = x + gate * proj


def ple(x, p, lw, *, tm):
    M, D = x.shape
    row = lambda w: pl.BlockSpec((tm, w), lambda i: (i, 0))
    full = lambda a: pl.BlockSpec(a.shape, lambda i: (0,) * a.ndim)
    params = (lw["ple_g"], lw["w_ple_gate"], lw["w_ple_proj"])
    return pl.pallas_call(
        _ple_kernel,
        grid=(M // tm,),
        in_specs=[row(D), row(p.shape[1])] + [full(a) for a in params],
        out_specs=row(D),
        out_shape=jax.ShapeDtypeStruct((M, D), F32),
        compiler_params=_cparams(("parallel",)),
        name="ple",
    )(x, p, *params)


def _rope_tables(seq_len):
    def angles(dim, theta):
        inv = 1.0 / (theta ** (jnp.arange(0, dim, 2, dtype=F32) / dim))
        ang = jnp.arange(seq_len, dtype=F32)[:, None] * inv[None, :]
        return jnp.cos(ang), jnp.sin(ang)

    z = lambda w: jnp.zeros((seq_len, w), F32)
    cos, sin = angles(MLA_ROPE_DIM, MLA_ROPE_THETA)
    half = MLA_ROPE_DIM // 2
    cm = jnp.concatenate([cos, cos, z(64)], axis=1)
    sam = jnp.concatenate([-sin, z(half), z(64)], axis=1)
    sbm = jnp.concatenate([z(half), sin, z(64)], axis=1)
    rope_m = jnp.concatenate([cm, sam, sbm], axis=1)

    cos, sin = angles(ROPE_PART_DIM, ROPE_THETA)
    half = ROPE_PART_DIM // 2
    rest = DIFF_QK_DIM - ROPE_PART_DIM
    c64 = jnp.concatenate([cos, cos, jnp.ones((seq_len, rest), F32)], axis=1)
    sa64 = jnp.concatenate([-sin, z(half), z(rest)], axis=1)
    sb64 = jnp.concatenate([z(half), sin, z(rest)], axis=1)
    rope_d = jnp.concatenate([c64, c64, sa64, sa64, sb64, sb64], axis=1)
    return rope_m, rope_d


def _na_bias_table(rpb):
    cols = jnp.arange(GRID_W)
    cs = jnp.clip(cols - NA_WIN_COLS // 2, 0, GRID_W - NA_WIN_COLS)
    kc = jnp.arange(GRID_W)
    valid = (kc[None, :] >= cs[:, None]) & (kc[None, :] < cs[:, None] + NA_WIN_COLS)
    nv = 2 * NA_WIN_COLS - 1
    onehot = (kc[None, None, :] - cols[None, :, None] + (NA_WIN_COLS - 1) == jnp.arange(nv)[:, None, None])
    toep = jnp.einsum("huv,vck->huck", rpb.astype(F32), onehot.astype(F32), precision=lax.Precision.HIGHEST)
    toep = jnp.where(valid[None, None], toep * LOG2E, NEG_BIG)
    n_h = rpb.shape[0]
    masked = 2 * NA_WIN_ROWS - 1
    toep = jnp.concatenate([toep, jnp.full((n_h, 1, GRID_W, GRID_W), NEG_BIG, F32)], axis=1)
    half = NA_WIN_ROWS // 2
    tile = np.full((3, NA_QROWS, NA_KROWS), masked, np.int32)
    for pat, delta in enumerate((0, -half, -NA_WIN_ROWS)):
        for a in range(NA_QROWS):
            first = (max(a - half, 0), a, min(a + half, NA_WIN_ROWS))[pat]
            for i in range(first, first + NA_WIN_ROWS):
                tile[pat, a, i] = delta + i - a + (NA_WIN_ROWS - 1)
    tab = jnp.take(toep, jnp.asarray(tile.reshape(-1)), axis=1)
    tab = tab.reshape(n_h, 3, NA_QROWS, NA_KROWS, GRID_W, GRID_W).transpose(0, 1, 2, 4, 3, 5)
    return tab.reshape(n_h, 3, NA_QROWS * GRID_W, NA_KROWS * GRID_W)


def _layer_params(i, norm_mix, w_in, na_q_norm, na_k_norm, na_rpb, mla_q_a_norm, mla_w_q_b, mla_kv_a_norm,
                  mla_w_kv_b, mla_q_nope_norm, mla_q_pe_norm, mla_k_nope_norm, mla_k_pe_norm,
                  diff_q_norm, diff_k_norm, diff_lambda_q1, diff_lambda_k1, diff_lambda_q2, diff_lambda_k2,
                  diff_subln, w_out, norm_ffn, w_gate, w_up, conv_w, conv_b, w_down,
                  ple_norm, w_ple_gate, w_ple_proj):
    r = lambda a: a[i].reshape(1, -1).astype(F32)
    pad_to = lambda a, n: jnp.pad(a, ((0, 0), (0, n - a.shape[1])))
    w = w_in[i]
    kpe0 = 3 * NA_W + MLA_Q_RANK + MLA_KV_RANK
    w_perm = jnp.concatenate([w[:, :kpe0], w[:, kpe0 + MLA_ROPE_DIM:], w[:, kpe0:kpe0 + MLA_ROPE_DIM],
                              jnp.zeros((w.shape[0], 128 - MLA_ROPE_DIM), w.dtype)], axis=1).astype(BF16)
    wqb = mla_w_q_b[i].reshape(MLA_Q_RANK, MLA_HEADS, MLA_QK_DIM)
    wqb = jnp.pad(wqb, ((0, 0), (0, 0), (0, MLA_SLOT - MLA_QK_DIM))).reshape(MLA_Q_RANK, MLA_HEADS * MLA_SLOT)
    wo = w_out[i].astype(BF16)
    return dict(
        mix_g=r(norm_mix), w_in=_col_tiles(w_perm, IN_TILE),
        naq_g=r(na_q_norm), nak_g=r(na_k_norm), na_bias=_na_bias_table(na_rpb[i]),
        qa_g=r(mla_q_a_norm), wqb=wqb.astype(BF16), kva_g=r(mla_kv_a_norm), wkvb=mla_w_kv_b[i].astype(BF16),
        qn_g=r(mla_q_nope_norm), qpe_g=pad_to(r(mla_q_pe_norm), 128),
        kn_g=r(mla_k_nope_norm), kpe_g=pad_to(r(mla_k_pe_norm), 128),
        dq_g=jnp.tile(r(diff_q_norm), (1, 2)), dk_g=jnp.tile(r(diff_k_norm), (1, 2)),
        lq1=r(diff_lambda_q1), lk1=r(diff_lambda_k1), lq2=r(diff_lambda_q2), lk2=r(diff_lambda_k2),
        subln_g=r(diff_subln),
        w_out_na=wo[:NA_W], w_out_mla=wo[NA_W:NA_W + MLA_VW], w_out_df=wo[NA_W + MLA_VW:],
        ffn_g=r(norm_ffn), w_gate=_col_tiles(w_gate[i].astype(BF16), FF_TILE),
        w_up=_col_tiles(w_up[i].astype(BF16), FF_TILE),
        conv_w=conv_w[i].astype(F32), conv_b=r(conv_b), w_down=w_down[i].astype(BF16),
        ple_g=r(ple_norm), w_ple_gate=w_ple_gate[i].astype(BF16), w_ple_proj=w_ple_proj[i].astype(BF16),
    )


def _tile(n, pref):
    t = min(pref, n)
    while n % t:
        t //= 2
    return t


def _encoder_layer(x, p_l, lw, rope_m, rope_d, *, layer_idx, batch, seq_len):
    tm = _tile(seq_len, 512)
    proj = rms_matmul(x, lw["mix_g"], lw["w_in"], tm=_tile(seq_len, 1024), out_dtype=F32)
    (na_q, na_k, na_v, m_q, m_k, m_vT, d_q, d_k, d_vT) = head_prep(
        proj, rope_m, rope_d, lw, seq_len=seq_len, tm=_tile(seq_len, 256))
    o_na = na_attention(na_q, na_k, na_v, lw["na_bias"], batch=batch, seq_len=seq_len)
    tq = _tile(seq_len, 512)
    tk = _tile(seq_len // 2, 2048)
    o_mla = flash_attention(m_q, m_k, m_vT, batch=batch, seq_len=seq_len, n_heads=MLA_HEADS, dq=MLA_SLOT,
                            v_of_head=lambda h: h, tq=tq, tk=tk, out_dtype=BF16)
    o_df = flash_attention(d_q, d_k, d_vT, batch=batch, seq_len=seq_len, n_heads=2 * DIFF_HEADS, dq=128,
                           v_of_head=lambda h: h // 2, tq=tq, tk=tk, out_dtype=F32)
    lam_init = 0.8 - 0.6 * math.exp(-0.3 * layer_idx)
    x = out_proj(x, o_na, o_mla, o_df, lw, lam_init=lam_init, tm=tm)
    x = ffn(x, lw, seq_len=seq_len, tm=tm)
    x = ple(x, p_l, lw, tm=tm)
    return x


def kernel(x_prompt, x_sample, p_prompt, p_sample, norm_mix, w_in, na_q_norm, na_k_norm, na_rpb, mla_q_a_norm, mla_w_q_b, mla_kv_a_norm, mla_w_kv_b, mla_q_nope_norm, mla_q_pe_norm, mla_k_nope_norm, mla_k_pe_norm, diff_q_norm, diff_k_norm, diff_lambda_q1, diff_lambda_k1, diff_lambda_q2, diff_lambda_k2, diff_subln, w_out, norm_ffn, w_gate, w_up, conv_w, conv_b, w_down, ple_norm, w_ple_gate, w_ple_proj):
    weights = (norm_mix, w_in, na_q_norm, na_k_norm, na_rpb, mla_q_a_norm, mla_w_q_b, mla_kv_a_norm, mla_w_kv_b,
               mla_q_nope_norm, mla_q_pe_norm, mla_k_nope_norm, mla_k_pe_norm, diff_q_norm, diff_k_norm,
               diff_lambda_q1, diff_lambda_k1, diff_lambda_q2, diff_lambda_k2, diff_subln, w_out,
               norm_ffn, w_gate, w_up, conv_w, conv_b, w_down, ple_norm, w_ple_gate, w_ple_proj)
    depth = norm_mix.shape[0]
    groups = []
    for x, p in ((x_prompt, p_prompt), (x_sample, p_sample)):
        b, t, d = x.shape
        groups.append(dict(x=x.reshape(b * t, d), p=p.reshape(depth, b * t, p.shape[-1]), batch=b, seq_len=t,
                           rope=_rope_tables(t), shape=x.shape))
    for i in range(depth):
        lw = _layer_params(i, *weights)
        for g in groups:
            g["x"] = _encoder_layer(g["x"], g["p"][i], lw, *g["rope"], layer_idx=i,
                                    batch=g["batch"], seq_len=g["seq_len"])
    return tuple(g["x"].reshape(g["shape"]) for g in groups)
```

```python
import functools
import math

import jax
import jax.numpy as jnp
import numpy as np
from jax import lax
from jax.experimental import pallas as pl
from jax.experimental.pallas import tpu as pltpu

F32 = jnp.float32
BF16 = jnp.bfloat16

EPS = 1e-6
LOG2E = 1.4426950408889634
NEG_BIG = -1e30

D_MODEL = 2048
PLE_DIM = 256
GRID_W = 64
HEAD_DIM = 128
NA_HEADS = 6
NA_WIN_ROWS = 8
NA_WIN_COLS = 16
MLA_HEADS = 5
MLA_Q_RANK = 512
MLA_KV_RANK = 256
MLA_NOPE_DIM = 128
MLA_ROPE_DIM = 64
MLA_V_DIM = 128
MLA_ROPE_THETA = 10000.0
DIFF_HEADS = 5
DIFF_QK_DIM = 64
DIFF_V_DIM = 128
ROPE_THETA = 500000.0
ROPE_PART_DIM = DIFF_QK_DIM // 4
D_FF = 5632

NA_W = NA_HEADS * HEAD_DIM
MLA_QK_DIM = MLA_NOPE_DIM + MLA_ROPE_DIM
MLA_SLOT = 256
DIFF_W = DIFF_HEADS * 2 * DIFF_QK_DIM
DIFF_VW = DIFF_HEADS * DIFF_V_DIM
MLA_VW = MLA_HEADS * MLA_V_DIM
VT_ROWS = 128 + 16

C_NAQ = 0
C_NAK = C_NAQ + NA_W
C_NAV = C_NAK + NA_W
C_CQ = C_NAV + NA_W
C_CKV = C_CQ + MLA_Q_RANK
C_DQ = C_CKV + MLA_KV_RANK
C_DK = C_DQ + DIFF_W
C_DV = C_DK + DIFF_W
C_KPE = C_DV + DIFF_VW
IN_COLS_PAD = C_KPE + 128

LANES = 128
IN_TILE = 1024
FF_TILE = 512
VMEM_LIMIT = 56 * 1024 * 1024


def _cparams(sem):
    return pltpu.CompilerParams(dimension_semantics=sem, vmem_limit_bytes=VMEM_LIMIT)


def _rms(x, g, n):
    ms = jnp.sum(x * x, axis=-1, keepdims=True) * (1.0 / n)
    return x * lax.rsqrt(ms + EPS) * g


def _rms_matmul_kernel(x_ref, g_ref, w_ref, o_ref, xn_ref):
    @pl.when(pl.program_id(1) == 0)
    def _():
        xn_ref[...] = _rms(x_ref[...], g_ref[...], x_ref.shape[-1]).astype(BF16)

    o_ref[...] = jnp.dot(xn_ref[...], w_ref[...], preferred_element_type=F32).astype(o_ref.dtype)


def _col_tiles(w, tn):
    k, n = w.shape
    return w.reshape(k, n // tn, tn).transpose(1, 0, 2)


def rms_matmul(x, g, w, *, tm, out_dtype):
    M, K = x.shape
    nt, _, tn = w.shape
    N = nt * tn
    return pl.pallas_call(
        _rms_matmul_kernel,
        grid=(M // tm, nt),
        in_specs=[pl.BlockSpec((tm, K), lambda i, j: (i, 0)),
                  pl.BlockSpec((1, K), lambda i, j: (0, 0)),
                  pl.BlockSpec((None, K, tn), lambda i, j: (j, 0, 0))],
        out_specs=pl.BlockSpec((tm, tn), lambda i, j: (i, j)),
        out_shape=jax.ShapeDtypeStruct((M, N), out_dtype),
        scratch_shapes=[pltpu.VMEM((tm, K), BF16)],
        compiler_params=_cparams(("parallel", "arbitrary")),
        name="rms_matmul",
    )(x, g, w)


def _prep_kernel(proj_ref, rope_m_ref, rope_d_ref, naq_g, nak_g, qa_g, wqb_ref, kva_g, wkvb_ref,
                 qn_g, qpe_g, kn_g, kpe_g, dq_g, dk_g,
                 naq_o, nak_o, nav_o, mq_o, mk_o, mvT_o, dq_o, dk_o, dvT_o):
    tm = proj_ref.shape[0]
    lane = lax.broadcasted_iota(jnp.int32, (tm, LANES), 1)
    lo = lane < 64
    ones_row = (lax.broadcasted_iota(jnp.int32, (VT_ROWS - 128, tm), 0) == 0).astype(BF16)

    na_scale = HEAD_DIM ** -0.5 * LOG2E
    for h in range(NA_HEADS):
        sl = slice(h * HEAD_DIM, (h + 1) * HEAD_DIM)
        q = _rms(proj_ref[:, C_NAQ + h * HEAD_DIM:C_NAQ + (h + 1) * HEAD_DIM], naq_g[...], HEAD_DIM)
        naq_o[:, sl] = (q * na_scale).astype(BF16)
        k = _rms(proj_ref[:, C_NAK + h * HEAD_DIM:C_NAK + (h + 1) * HEAD_DIM], nak_g[...], HEAD_DIM)
        nak_o[:, sl] = k.astype(BF16)
    nav_o[...] = proj_ref[:, C_NAV:C_NAV + NA_W].astype(BF16)

    cm, sam, sbm = rope_m_ref[:, 0:128], rope_m_ref[:, 128:256], rope_m_ref[:, 256:384]

    def rope_m(y):
        return y * cm + pltpu.roll(y, 96, 1) * sam + pltpu.roll(y, 32, 1) * sbm

    mla_scale = MLA_QK_DIM ** -0.5 * LOG2E
    cq = _rms(proj_ref[:, C_CQ:C_CQ + MLA_Q_RANK], qa_g[...], MLA_Q_RANK).astype(BF16)
    qm = jnp.dot(cq, wqb_ref[...], preferred_element_type=F32)
    for h in range(MLA_HEADS):
        b = h * MLA_SLOT
        nope = _rms(qm[:, b:b + 128], qn_g[...], MLA_NOPE_DIM)
        pe = rope_m(_rms(qm[:, b + 128:b + 256], qpe_g[...], MLA_ROPE_DIM))
        mq_o[:, b:b + 128] = (nope * mla_scale).astype(BF16)
        mq_o[:, b + 128:b + 256] = (pe * mla_scale).astype(BF16)
    ckv = _rms(proj_ref[:, C_CKV:C_CKV + MLA_KV_RANK], kva_g[...], MLA_KV_RANK).astype(BF16)
    kv = jnp.dot(ckv, wkvb_ref[...], preferred_element_type=F32)
    kpe = rope_m(_rms(proj_ref[:, C_KPE:C_KPE + 128], kpe_g[...], MLA_ROPE_DIM))
    kpe = kpe.astype(BF16)
    for h in range(MLA_HEADS):
        b = h * MLA_SLOT
        kn = _rms(kv[:, b:b + 128], kn_g[...], MLA_NOPE_DIM)
        mk_o[:, b:b + 128] = kn.astype(BF16)
        mk_o[:, b + 128:b + 256] = kpe
        mvT_o[h * VT_ROWS:h * VT_ROWS + MLA_V_DIM, :] = kv[:, b + 128:b + 256].T.astype(BF16)
        mvT_o[h * VT_ROWS + MLA_V_DIM:(h + 1) * VT_ROWS, :] = ones_row

    cd, sad, sbd = rope_d_ref[:, 0:128], rope_d_ref[:, 128:256], rope_d_ref[:, 256:384]

    def rope_d(y):
        return y * cd + pltpu.roll(y, 120, 1) * sad + pltpu.roll(y, 8, 1) * sbd

    def group_rms(x, g):
        x2 = x * x
        s_lo = jnp.sum(jnp.where(lo, x2, 0.0), axis=-1, keepdims=True)
        s_hi = jnp.sum(jnp.where(lo, 0.0, x2), axis=-1, keepdims=True)
        ms = jnp.where(lo, s_lo, s_hi) * (1.0 / DIFF_QK_DIM)
        return x * lax.rsqrt(ms + EPS) * g

    def split_components(y):
        return jnp.where(lo, y, 0.0), jnp.where(lo, pltpu.roll(y, 64, 1), 0.0)

    df_scale = DIFF_QK_DIM ** -0.5 * LOG2E
    for h in range(DIFF_HEADS):
        q = rope_d(group_rms(proj_ref[:, C_DQ + h * 128:C_DQ + (h + 1) * 128], dq_g[...])) * df_scale
        q0, q1 = split_components(q)
        dq_o[:, (2 * h) * 128:(2 * h + 1) * 128] = q0.astype(BF16)
        dq_o[:, (2 * h + 1) * 128:(2 * h + 2) * 128] = q1.astype(BF16)
        k = rope_d(group_rms(proj_ref[:, C_DK + h * 128:C_DK + (h + 1) * 128], dk_g[...]))
        k0, k1 = split_components(k)
        dk_o[:, (2 * h) * 128:(2 * h + 1) * 128] = k0.astype(BF16)
        dk_o[:, (2 * h + 1) * 128:(2 * h + 2) * 128] = k1.astype(BF16)
        v = proj_ref[:, C_DV + h * DIFF_V_DIM:C_DV + (h + 1) * DIFF_V_DIM]
        dvT_o[h * VT_ROWS:h * VT_ROWS + DIFF_V_DIM, :] = v.T.astype(BF16)
        dvT_o[h * VT_ROWS + DIFF_V_DIM:(h + 1) * VT_ROWS, :] = ones_row


def head_prep(proj, rope_m, rope_d, lw, *, seq_len, tm):
    M = proj.shape[0]
    nt = seq_len // tm
    row = lambda w: pl.BlockSpec((tm, w), lambda i: (i, 0))
    colT = lambda h: pl.BlockSpec((h, tm), lambda i: (0, i))
    full = lambda a: pl.BlockSpec(a.shape, lambda i: (0,) * a.ndim)
    rope = pl.BlockSpec((tm, 384), lambda i: (i % nt, 0))
    params = (lw["naq_g"], lw["nak_g"], lw["qa_g"], lw["wqb"], lw["kva_g"], lw["wkvb"],
              lw["qn_g"], lw["qpe_g"], lw["kn_g"], lw["kpe_g"], lw["dq_g"], lw["dk_g"])
    out_shapes = (
        jax.ShapeDtypeStruct((M, NA_W), BF16), jax.ShapeDtypeStruct((M, NA_W), BF16),
        jax.ShapeDtypeStruct((M, NA_W), BF16),
        jax.ShapeDtypeStruct((M, MLA_HEADS * MLA_SLOT), BF16),
        jax.ShapeDtypeStruct((M, MLA_HEADS * MLA_SLOT), BF16),
        jax.ShapeDtypeStruct((MLA_HEADS * VT_ROWS, M), BF16),
        jax.ShapeDtypeStruct((M, 2 * DIFF_HEADS * 128), BF16),
        jax.ShapeDtypeStruct((M, 2 * DIFF_HEADS * 128), BF16),
        jax.ShapeDtypeStruct((DIFF_HEADS * VT_ROWS, M), BF16),
    )
    out_specs = (row(NA_W), row(NA_W), row(NA_W), row(MLA_HEADS * MLA_SLOT), row(MLA_HEADS * MLA_SLOT),
                 colT(MLA_HEADS * VT_ROWS), row(2 * DIFF_HEADS * 128), row(2 * DIFF_HEADS * 128),
                 colT(DIFF_HEADS * VT_ROWS))
    return pl.pallas_call(
        _prep_kernel,
        grid=(M // tm,),
        in_specs=[row(IN_COLS_PAD), rope, rope] + [full(a) for a in params],
        out_specs=out_specs,
        out_shape=out_shapes,
        compiler_params=_cparams(("parallel",)),
        name="head_prep",
    )(proj, rope_m, rope_d, *params)


def _flash_kernel(q_ref, k_ref, vT_ref, o_ref, sa_ref, sb_ref, acc_ref, *, tq, tk):
    seq_len = q_ref.shape[0]
    dv = o_ref.shape[1]
    nk = seq_len // tk
    total = (seq_len // tq) * nk
    sub = 16

    def scores(f):
        qoff = pl.multiple_of((f // nk) * tq, tq)
        koff = pl.multiple_of((f % nk) * tk, tk)
        s = lax.dot_general(k_ref[pl.ds(koff, tk), :], q_ref[pl.ds(qoff, tq), :],
                            (((1,), (1,)), ((), ())), preferred_element_type=F32)
        return s.astype(BF16)

    def softmax_pv(s_ref, c, m):
        s = s_ref[...]
        part = jnp.max(s.reshape(tk // sub, sub, tq), axis=0).astype(F32)
        m_new = jnp.maximum(m, jnp.max(part, axis=0, keepdims=True))
        alpha = jnp.exp2(m - m_new)
        p = jnp.exp2(s - m_new.astype(BF16))
        koff = pl.multiple_of(c * tk, tk)
        pv = jnp.dot(vT_ref[:, pl.ds(koff, tk)], p, preferred_element_type=F32)
        acc_ref[...] = alpha * acc_ref[...] + pv
        return m_new

    def pair(i, m):
        f = 2 * i
        c = f % nk
        m = jnp.where(c == 0, NEG_BIG, m)
        sb_ref[...] = scores(f + 1)
        m = softmax_pv(sa_ref, c, m)
        sa_ref[...] = scores(jnp.minimum(f + 2, total - 1))
        m = softmax_pv(sb_ref, c + 1, m)

        @pl.when(c + 2 == nk)
        def _():
            qoff = pl.multiple_of((f // nk) * tq, tq)
            out = acc_ref[0:dv, :] / acc_ref[dv:dv + 1, :]
            o_ref[pl.ds(qoff, tq), :] = out.T.astype(o_ref.dtype)

        return m

    acc_ref[...] = jnp.zeros_like(acc_ref)
    sa_ref[...] = scores(0)
    lax.fori_loop(0, total // 2, pair, jnp.full((1, tq), NEG_BIG, F32))


def flash_attention(q, k, vT, *, batch, seq_len, n_heads, dq, v_of_head, tq, tk, out_dtype):
    M = q.shape[0]
    dv = 128
    assert seq_len % (2 * tk) == 0 and seq_len % tq == 0
    once = pl.Buffered(1)
    return pl.pallas_call(
        functools.partial(_flash_kernel, tq=tq, tk=tk),
        grid=(batch, n_heads),
        in_specs=[pl.BlockSpec((seq_len, dq), lambda b, h: (b, h), pipeline_mode=once),
                  pl.BlockSpec((seq_len, dq), lambda b, h: (b, h), pipeline_mode=once),
                  pl.BlockSpec((VT_ROWS, seq_len), lambda b, h: (v_of_head(h), b), pipeline_mode=once)],
        out_specs=pl.BlockSpec((seq_len, dv), lambda b, h: (b, h)),
        out_shape=jax.ShapeDtypeStruct((M, n_heads * dv), out_dtype),
        scratch_shapes=[pltpu.VMEM((tk, tq), BF16), pltpu.VMEM((tk, tq), BF16), pltpu.VMEM((VT_ROWS, tq), F32)],
        compiler_params=_cparams(("parallel", "parallel")),
        name="flash_attention",
    )(q, k, vT)


NA_QROWS = 8
NA_KROWS = 2 * NA_WIN_ROWS


def _na_kernel(q_ref, k_ref, v_ref, bias_ref, o_ref, *, rows):
    i = pl.program_id(2)
    ws = jnp.clip(i * NA_QROWS - NA_WIN_ROWS // 2, 0, rows - NA_KROWS)
    ks = pl.multiple_of(ws * GRID_W, GRID_W)
    win = NA_KROWS * GRID_W
    k = k_ref[pl.ds(ks, win), :]
    s = lax.dot_general(q_ref[...], k, (((1,), (1,)), ((), ())), preferred_element_type=F32)
    s = s + bias_ref[...]
    m = jnp.max(s, axis=-1, keepdims=True)
    p = jnp.exp2(s - m)
    l = jnp.sum(p, axis=-1, keepdims=True)
    o = jnp.dot(p.astype(BF16), v_ref[pl.ds(ks, win), :], preferred_element_type=F32)
    o_ref[...] = (o / l).astype(o_ref.dtype)


def na_attention(q, k, v, bias, *, batch, seq_len):
    M = q.shape[0]
    rows = seq_len // GRID_W
    assert rows >= NA_KROWS and rows % NA_QROWS == 0
    nblk = rows // NA_QROWS
    tq = NA_QROWS * GRID_W
    pattern = lambda i: jnp.where(i == 0, 0, jnp.where(i == nblk - 1, 2, 1))
    return pl.pallas_call(
        functools.partial(_na_kernel, rows=rows),
        grid=(batch, NA_HEADS, nblk),
        in_specs=[pl.BlockSpec((tq, HEAD_DIM), lambda b, h, i: (b * nblk + i, h)),
                  pl.BlockSpec((seq_len, HEAD_DIM), lambda b, h, i: (b, h)),
                  pl.BlockSpec((seq_len, HEAD_DIM), lambda b, h, i: (b, h)),
                  pl.BlockSpec((None, None, tq, NA_KROWS * GRID_W), lambda b, h, i: (h, pattern(i), 0, 0))],
        out_specs=pl.BlockSpec((tq, HEAD_DIM), lambda b, h, i: (b * nblk + i, h)),
        out_shape=jax.ShapeDtypeStruct((M, NA_W), BF16),
        compiler_params=_cparams(("parallel", "parallel", "arbitrary")),
        name="na_attention",
    )(q, k, v, bias)


def _out_proj_kernel(x_ref, ona_ref, omla_ref, odf_ref, subln_ref, lq1, lk1, lq2, lk2,
                     w_na, w_mla, w_df, o_ref, *, lam_init):
    lam = (jnp.exp(jnp.sum(lq1[...] * lk1[...], axis=-1, keepdims=True))
           - jnp.exp(jnp.sum(lq2[...] * lk2[...], axis=-1, keepdims=True)) + lam_init)
    heads = []
    for h in range(DIFF_HEADS):
        o1 = odf_ref[:, (2 * h) * 128:(2 * h + 1) * 128]
        o2 = odf_ref[:, (2 * h + 1) * 128:(2 * h + 2) * 128]
        o = _rms(o1 - lam * o2, subln_ref[...], DIFF_V_DIM) * (1.0 - lam_init)
        heads.append(o.astype(BF16))
    odf = jnp.concatenate(heads, axis=-1)
    y = jnp.dot(ona_ref[...], w_na[...], preferred_element_type=F32)
    y = y + jnp.dot(omla_ref[...], w_mla[...], preferred_element_type=F32)
    y = y + jnp.dot(odf, w_df[...], preferred_element_type=F32)
    o_ref[...] = x_ref[...] + y


def out_proj(x, o_na, o_mla, o_df, lw, *, lam_init, tm):
    M, D = x.shape
    row = lambda w: pl.BlockSpec((tm, w), lambda i: (i, 0))
    full = lambda a: pl.BlockSpec(a.shape, lambda i: (0,) * a.ndim)
    params = (lw["subln_g"], lw["lq1"], lw["lk1"], lw["lq2"], lw["lk2"], lw["w_out_na"], lw["w_out_mla"], lw["w_out_df"])
    return pl.pallas_call(
        functools.partial(_out_proj_kernel, lam_init=lam_init),
        grid=(M // tm,),
        in_specs=[row(D), row(NA_W), row(MLA_VW), row(2 * DIFF_HEADS * 128)] + [full(a) for a in params],
        out_specs=row(D),
        out_shape=jax.ShapeDtypeStruct((M, D), F32),
        compiler_params=_cparams(("parallel",)),
        name="out_proj",
    )(x, o_na, o_mla, o_df, *params)


HALO = 16


def _ffn_kernel(x_ref, xp_ref, xn_ref, g_ref, wg_ref, wu_ref, cw_ref, cb_ref, wd_ref, o_ref,
                xs_ref, acc_ref, *, seq_len):
    i = pl.program_id(0)
    f = pl.program_id(1)
    tm = x_ref.shape[0]

    @pl.when(f == 0)
    def _():
        d = x_ref.shape[-1]
        row0 = i * tm
        at_start = (row0 % seq_len) == 0
        at_end = ((row0 + tm) % seq_len) == 0
        prev = _rms(xp_ref[...], g_ref[...], d)
        nxt = _rms(xn_ref[...], g_ref[...], d)
        xs_ref[0:HALO, :] = jnp.where(at_start, 0.0, prev).astype(BF16)
        xs_ref[HALO:HALO + tm, :] = _rms(x_ref[...], g_ref[...], d).astype(BF16)
        xs_ref[HALO + tm:2 * HALO + tm, :] = jnp.where(at_end, 0.0, nxt).astype(BF16)
        acc_ref[...] = jnp.zeros_like(acc_ref)

    n_ext = tm + 2 * HALO
    gate = jnp.dot(xs_ref[...], wg_ref[...], preferred_element_type=F32)
    g_prev = pltpu.roll(gate, 1, 0)[HALO:HALO + tm, :]
    g_next = pltpu.roll(gate, n_ext - 1, 0)[HALO:HALO + tm, :]
    g_mid = gate[HALO:HALO + tm, :]
    g = g_prev * cw_ref[0:1, :] + g_mid * cw_ref[1:2, :] + g_next * cw_ref[2:3, :] + cb_ref[...]
    u = jnp.dot(xs_ref[HALO:HALO + tm, :], wu_ref[...], preferred_element_type=F32)
    a = (g * jax.nn.sigmoid(g) * u).astype(BF16)
    acc_ref[...] += jnp.dot(a, wd_ref[...], preferred_element_type=F32)

    @pl.when(f == pl.num_programs(1) - 1)
    def _():
        o_ref[...] = x_ref[...] + acc_ref[...]


def ffn(x, lw, *, seq_len, tm):
    M, D = x.shape
    nf, _, tf = lw["w_gate"].shape
    assert seq_len % tm == 0 and tm % HALO == 0
    hb = tm // HALO
    last = M // HALO - 1
    return pl.pallas_call(
        functools.partial(_ffn_kernel, seq_len=seq_len),
        grid=(M // tm, nf),
        in_specs=[pl.BlockSpec((tm, D), lambda i, f: (i, 0)),
                  pl.BlockSpec((HALO, D), lambda i, f: (jnp.maximum(i * hb - 1, 0), 0)),
                  pl.BlockSpec((HALO, D), lambda i, f: (jnp.minimum((i + 1) * hb, last), 0)),
                  pl.BlockSpec((1, D), lambda i, f: (0, 0)),
                  pl.BlockSpec((None, D, tf), lambda i, f: (f, 0, 0)),
                  pl.BlockSpec((None, D, tf), lambda i, f: (f, 0, 0)),
                  pl.BlockSpec((3, tf), lambda i, f: (0, f)),
                  pl.BlockSpec((1, tf), lambda i, f: (0, f)),
                  pl.BlockSpec((tf, D), lambda i, f: (f, 0))],
        out_specs=pl.BlockSpec((tm, D), lambda i, f: (i, 0)),
        out_shape=jax.ShapeDtypeStruct((M, D), F32),
        scratch_shapes=[pltpu.VMEM((tm + 2 * HALO, D), BF16), pltpu.VMEM((tm, D), F32)],
        compiler_params=_cparams(("parallel", "arbitrary")),
        name="ffn",
    )(x, x, x, lw["ffn_g"], lw["w_gate"], lw["w_up"], lw["conv_w"], lw["conv_b"], lw["w_down"])


def _ple_kernel(x_ref, p_ref, g_ref, wg_ref, wp_ref, o_ref):
    x = x_ref[...]
    xn = _rms(x, g_ref[...], x.shape[-1]).astype(BF16)
    gate = jax.nn.sigmoid(jnp.dot(xn, wg_ref[...], preferred_element_type=F32))
    proj = jnp.dot(p_ref[...].astype(BF16), wp_ref[...], preferred_element_type=F32)
    o_ref[...] = x + gate * proj


def ple(x, p, lw, *, tm):
    M, D = x.shape
    row = lambda w: pl.BlockSpec((tm, w), lambda i: (i, 0))
    full = lambda a: pl.BlockSpec(a.shape, lambda i: (0,) * a.ndim)
    params = (lw["ple_g"], lw["w_ple_gate"], lw["w_ple_proj"])
    return pl.pallas_call(
        _ple_kernel,
        grid=(M // tm,),
        in_specs=[row(D), row(p.shape[1])] + [full(a) for a in params],
        out_specs=row(D),
        out_shape=jax.ShapeDtypeStruct((M, D), F32),
        compiler_params=_cparams(("parallel",)),
        name="ple",
    )(x, p, *params)


def _rope_tables(seq_len):
    def angles(dim, theta):
        inv = 1.0 / (theta ** (jnp.arange(0, dim, 2, dtype=F32) / dim))
        ang = jnp.arange(seq_len, dtype=F32)[:, None] * inv[None, :]
        return jnp.cos(ang), jnp.sin(ang)

    z = lambda w: jnp.zeros((seq_len, w), F32)
    cos, sin = angles(MLA_ROPE_DIM, MLA_ROPE_THETA)
    half = MLA_ROPE_DIM // 2
    cm = jnp.concatenate([cos, cos, z(64)], axis=1)
    sam = jnp.concatenate([-sin, z(half), z(64)], axis=1)
    sbm = jnp.concatenate([z(half), sin, z(64)], axis=1)
    rope_m = jnp.concatenate([cm, sam, sbm], axis=1)

    cos, sin = angles(ROPE_PART_DIM, ROPE_THETA)
    half = ROPE_PART_DIM // 2
    rest = DIFF_QK_DIM - ROPE_PART_DIM
    c64 = jnp.concatenate([cos, cos, jnp.ones((seq_len, rest), F32)], axis=1)
    sa64 = jnp.concatenate([-sin, z(half), z(rest)], axis=1)
    sb64 = jnp.concatenate([z(half), sin, z(rest)], axis=1)
    rope_d = jnp.concatenate([c64, c64, sa64, sa64, sb64, sb64], axis=1)
    return rope_m, rope_d


def _na_bias_table(rpb):
    cols = jnp.arange(GRID_W)
    cs = jnp.clip(cols - NA_WIN_COLS // 2, 0, GRID_W - NA_WIN_COLS)
    kc = jnp.arange(GRID_W)
    valid = (kc[None, :] >= cs[:, None]) & (kc[None, :] < cs[:, None] + NA_WIN_COLS)
    nv = 2 * NA_WIN_COLS - 1
    onehot = (kc[None, None, :] - cols[None, :, None] + (NA_WIN_COLS - 1) == jnp.arange(nv)[:, None, None])
    toep = jnp.einsum("huv,vck->huck", rpb.astype(F32), onehot.astype(F32), precision=lax.Precision.HIGHEST)
    toep = jnp.where(valid[None, None], toep * LOG2E, NEG_BIG)
    n_h = rpb.shape[0]
    masked = 2 * NA_WIN_ROWS - 1
    toep = jnp.concatenate([toep, jnp.full((n_h, 1, GRID_W, GRID_W), NEG_BIG, F32)], axis=1)
    half = NA_WIN_ROWS // 2
    tile = np.full((3, NA_QROWS, NA_KROWS), masked, np.int32)
    for pat, delta in enumerate((0, -half, -NA_WIN_ROWS)):
        for a in range(NA_QROWS):
            first = (max(a - half, 0), a, min(a + half, NA_WIN_ROWS))[pat]
            for i in range(first, first + NA_WIN_ROWS):
                tile[pat, a, i] = delta + i - a + (NA_WIN_ROWS - 1)
    tab = jnp.take(toep, jnp.asarray(tile.reshape(-1)), axis=1)
    tab = tab.reshape(n_h, 3, NA_QROWS, NA_KROWS, GRID_W, GRID_W).transpose(0, 1, 2, 4, 3, 5)
    return tab.reshape(n_h, 3, NA_QROWS * GRID_W, NA_KROWS * GRID_W)


def _layer_params(i, norm_mix, w_in, na_q_norm, na_k_norm, na_rpb, mla_q_a_norm, mla_w_q_b, mla_kv_a_norm,
                  mla_w_kv_b, mla_q_nope_norm, mla_q_pe_norm, mla_k_nope_norm, mla_k_pe_norm,
                  diff_q_norm, diff_k_norm, diff_lambda_q1, diff_lambda_k1, diff_lambda_q2, diff_lambda_k2,
                  diff_subln, w_out, norm_ffn, w_gate, w_up, conv_w, conv_b, w_down,
                  ple_norm, w_ple_gate, w_ple_proj):
    r = lambda a: a[i].reshape(1, -1).astype(F32)
    pad_to = lambda a, n: jnp.pad(a, ((0, 0), (0, n - a.shape[1])))
    w = w_in[i]
    kpe0 = 3 * NA_W + MLA_Q_RANK + MLA_KV_RANK
    w_perm = jnp.concatenate([w[:, :kpe0], w[:, kpe0 + MLA_ROPE_DIM:], w[:, kpe0:kpe0 + MLA_ROPE_DIM],
                              jnp.zeros((w.shape[0], 128 - MLA_ROPE_DIM), w.dtype)], axis=1).astype(BF16)
    wqb = mla_w_q_b[i].reshape(MLA_Q_RANK, MLA_HEADS, MLA_QK_DIM)
    wqb = jnp.pad(wqb, ((0, 0), (0, 0), (0, MLA_SLOT - MLA_QK_DIM))).reshape(MLA_Q_RANK, MLA_HEADS * MLA_SLOT)
    wo = w_out[i].astype(BF16)
    return dict(
        mix_g=r(norm_mix), w_in=_col_tiles(w_perm, IN_TILE),
        naq_g=r(na_q_norm), nak_g=r(na_k_norm), na_bias=_na_bias_table(na_rpb[i]),
        qa_g=r(mla_q_a_norm), wqb=wqb.astype(BF16), kva_g=r(mla_kv_a_norm), wkvb=mla_w_kv_b[i].astype(BF16),
        qn_g=r(mla_q_nope_norm), qpe_g=pad_to(r(mla_q_pe_norm), 128),
        kn_g=r(mla_k_nope_norm), kpe_g=pad_to(r(mla_k_pe_norm), 128),
        dq_g=jnp.tile(r(diff_q_norm), (1, 2)), dk_g=jnp.tile(r(diff_k_norm), (1, 2)),
        lq1=r(diff_lambda_q1), lk1=r(diff_lambda_k1), lq2=r(diff_lambda_q2), lk2=r(diff_lambda_k2),
        subln_g=r(diff_subln),
        w_out_na=wo[:NA_W], w_out_mla=wo[NA_W:NA_W + MLA_VW], w_out_df=wo[NA_W + MLA_VW:],
        ffn_g=r(norm_ffn), w_gate=_col_tiles(w_gate[i].astype(BF16), FF_TILE),
        w_up=_col_tiles(w_up[i].astype(BF16), FF_TILE),
        conv_w=conv_w[i].astype(F32), conv_b=r(conv_b), w_down=w_down[i].astype(BF16),
        ple_g=r(ple_norm), w_ple_gate=w_ple_gate[i].astype(BF16), w_ple_proj=w_ple_proj[i].astype(BF16),
    )


def _tile(n, pref):
    t = min(pref, n)
    while n % t:
        t //= 2
    return t


def _encoder_layer(x, p_l, lw, rope_m, rope_d, *, layer_idx, batch, seq_len):
    tm = _tile(seq_len, 512)
    proj = rms_matmul(x, lw["mix_g"], lw["w_in"], tm=_tile(seq_len, 1024), out_dtype=F32)
    (na_q, na_k, na_v, m_q, m_k, m_vT, d_q, d_k, d_vT) = head_prep(
        proj, rope_m, rope_d, lw, seq_len=seq_len, tm=tm)
    o_na = na_attention(na_q, na_k, na_v, lw["na_bias"], batch=batch, seq_len=seq_len)
    tq = _tile(seq_len, 512)
    tk = _tile(seq_len // 2, 8192)
    o_mla = flash_attention(m_q, m_k, m_vT, batch=batch, seq_len=seq_len, n_heads=MLA_HEADS, dq=MLA_SLOT,
                            v_of_head=lambda h: h, tq=tq, tk=tk, out_dtype=BF16)
    o_df = flash_attention(d_q, d_k, d_vT, batch=batch, seq_len=seq_len, n_heads=2 * DIFF_HEADS, dq=128,
                           v_of_head=lambda h: h // 2, tq=tq, tk=tk, out_dtype=F32)
    lam_init = 0.8 - 0.6 * math.exp(-0.3 * layer_idx)
    x = out_proj(x, o_na, o_mla, o_df, lw, lam_init=lam_init, tm=tm)
    x = ffn(x, lw, seq_len=seq_len, tm=tm)
    x = ple(x, p_l, lw, tm=tm)
    return x


def kernel(x_prompt, x_sample, p_prompt, p_sample, norm_mix, w_in, na_q_norm, na_k_norm, na_rpb, mla_q_a_norm, mla_w_q_b, mla_kv_a_norm, mla_w_kv_b, mla_q_nope_norm, mla_q_pe_norm, mla_k_nope_norm, mla_k_pe_norm, diff_q_norm, diff_k_norm, diff_lambda_q1, diff_lambda_k1, diff_lambda_q2, diff_lambda_k2, diff_subln, w_out, norm_ffn, w_gate, w_up, conv_w, conv_b, w_down, ple_norm, w_ple_gate, w_ple_proj):
    weights = (norm_mix, w_in, na_q_norm, na_k_norm, na_rpb, mla_q_a_norm, mla_w_q_b, mla_kv_a_norm, mla_w_kv_b,
               mla_q_nope_norm, mla_q_pe_norm, mla_k_nope_norm, mla_k_pe_norm, diff_q_norm, diff_k_norm,
               diff_lambda_q1, diff_lambda_k1, diff_lambda_q2, diff_lambda_k2, diff_subln, w_out,
               norm_ffn, w_gate, w_up, conv_w, conv_b, w_down, ple_norm, w_ple_gate, w_ple_proj)
    depth = norm_mix.shape[0]
    groups = []
    for x, p in ((x_prompt, p_prompt), (x_sample, p_sample)):
        b, t, d = x.shape
        groups.append(dict(x=x.reshape(b * t, d), p=p.reshape(depth, b * t, p.shape[-1]), batch=b, seq_len=t,
                           rope=_rope_tables(t), shape=x.shape))
    for i in range(depth):
        lw = _layer_params(i, *weights)
        for g in groups:
            g["x"] = _encoder_layer(g["x"], g["p"][i], lw, *g["rope"], layer_idx=i,
                                    batch=g["batch"], seq_len=g["seq_len"])
    return tuple(g["x"].reshape(g["shape"]) for g in groups)
```

```python
import functools
import math

import jax
import jax.numpy as jnp
import numpy as np
from jax import lax
from jax.experimental import pallas as pl
from jax.experimental.pallas import tpu as pltpu

F32 = jnp.float32
BF16 = jnp.bfloat16

EPS = 1e-6
LOG2E = 1.4426950408889634
NEG_BIG = -1e30

D_MODEL = 2048
PLE_DIM = 256
GRID_W = 64
HEAD_DIM = 128
NA_HEADS = 6
NA_WIN_ROWS = 8
NA_WIN_COLS = 16
MLA_HEADS = 5
MLA_Q_RANK = 512
MLA_KV_RANK = 256
MLA_NOPE_DIM = 128
MLA_ROPE_DIM = 64
MLA_V_DIM = 128
MLA_ROPE_THETA = 10000.0
DIFF_HEADS = 5
DIFF_QK_DIM = 64
DIFF_V_DIM = 128
ROPE_THETA = 500000.0
ROPE_PART_DIM = DIFF_QK_DIM // 4
D_FF = 5632

NA_W = NA_HEADS * HEAD_DIM
MLA_QK_DIM = MLA_NOPE_DIM + MLA_ROPE_DIM
MLA_SLOT = 256
DIFF_W = DIFF_HEADS * 2 * DIFF_QK_DIM
DIFF_VW = DIFF_HEADS * DIFF_V_DIM
MLA_VW = MLA_HEADS * MLA_V_DIM
VT_ROWS = 128 + 16

C_NAQ = 0
C_NAK = C_NAQ + NA_W
C_NAV = C_NAK + NA_W
C_CQ = C_NAV + NA_W
C_CKV = C_CQ + MLA_Q_RANK
C_DQ = C_CKV + MLA_KV_RANK
C_DK = C_DQ + DIFF_W
C_DV = C_DK + DIFF_W
C_KPE = C_DV + DIFF_VW
IN_COLS_PAD = C_KPE + 128

LANES = 128
IN_TILE = 1024
FF_TILE = 512
VMEM_LIMIT = 56 * 1024 * 1024


def _cparams(sem):
    return pltpu.CompilerParams(dimension_semantics=sem, vmem_limit_bytes=VMEM_LIMIT)


def _rms(x, g, n):
    ms = jnp.sum(x * x, axis=-1, keepdims=True) * (1.0 / n)
    return x * lax.rsqrt(ms + EPS) * g


def _rms_matmul_kernel(x_ref, g_ref, w_ref, o_ref, xn_ref):
    @pl.when(pl.program_id(1) == 0)
    def _():
        xn_ref[...] = _rms(x_ref[...], g_ref[...], x_ref.shape[-1]).astype(BF16)

    o_ref[...] = jnp.dot(xn_ref[...], w_ref[...], preferred_element_type=F32).astype(o_ref.dtype)


def _col_tiles(w, tn):
    k, n = w.shape
    return w.reshape(k, n // tn, tn).transpose(1, 0, 2)


def rms_matmul(x, g, w, *, tm, out_dtype):
    M, K = x.shape
    nt, _, tn = w.shape
    N = nt * tn
    return pl.pallas_call(
        _rms_matmul_kernel,
        grid=(M // tm, nt),
        in_specs=[pl.BlockSpec((tm, K), lambda i, j: (i, 0)),
                  pl.BlockSpec((1, K), lambda i, j: (0, 0)),
                  pl.BlockSpec((None, K, tn), lambda i, j: (j, 0, 0))],
        out_specs=pl.BlockSpec((tm, tn), lambda i, j: (i, j)),
        out_shape=jax.ShapeDtypeStruct((M, N), out_dtype),
        scratch_shapes=[pltpu.VMEM((tm, K), BF16)],
        compiler_params=_cparams(("parallel", "arbitrary")),
        name="rms_matmul",
    )(x, g, w)


def _prep_kernel(proj_ref, rope_m_ref, rope_d_ref, naq_g, nak_g, qa_g, wqb_ref, kva_g, wkvb_ref,
                 qn_g, qpe_g, kn_g, kpe_g, dq_g, dk_g,
                 naq_o, nak_o, navT_o, mq_o, mk_o, mvT_o, dq_o, dk_o, dvT_o):
    tm = proj_ref.shape[0]
    lane = lax.broadcasted_iota(jnp.int32, (tm, LANES), 1)
    lo = lane < 64
    ones_row = (lax.broadcasted_iota(jnp.int32, (VT_ROWS - 128, tm), 0) == 0).astype(BF16)

    na_scale = HEAD_DIM ** -0.5 * LOG2E
    for h in range(NA_HEADS):
        sl = slice(h * HEAD_DIM, (h + 1) * HEAD_DIM)
        q = _rms(proj_ref[:, C_NAQ + h * HEAD_DIM:C_NAQ + (h + 1) * HEAD_DIM], naq_g[...], HEAD_DIM)
        naq_o[:, sl] = (q * na_scale).astype(BF16)
        k = _rms(proj_ref[:, C_NAK + h * HEAD_DIM:C_NAK + (h + 1) * HEAD_DIM], nak_g[...], HEAD_DIM)
        nak_o[:, sl] = k.astype(BF16)
        v = proj_ref[:, C_NAV + h * HEAD_DIM:C_NAV + (h + 1) * HEAD_DIM]
        navT_o[h * VT_ROWS:h * VT_ROWS + HEAD_DIM, :] = v.T.astype(BF16)
        navT_o[h * VT_ROWS + HEAD_DIM:(h + 1) * VT_ROWS, :] = ones_row

    cm, sam, sbm = rope_m_ref[:, 0:128], rope_m_ref[:, 128:256], rope_m_ref[:, 256:384]

    def rope_m(y):
        return y * cm + pltpu.roll(y, 96, 1) * sam + pltpu.roll(y, 32, 1) * sbm

    mla_scale = MLA_QK_DIM ** -0.5 * LOG2E
    cq = _rms(proj_ref[:, C_CQ:C_CQ + MLA_Q_RANK], qa_g[...], MLA_Q_RANK).astype(BF16)
    qm = jnp.dot(cq, wqb_ref[...], preferred_element_type=F32)
    for h in range(MLA_HEADS):
        b = h * MLA_SLOT
        nope = _rms(qm[:, b:b + 128], qn_g[...], MLA_NOPE_DIM)
        pe = rope_m(_rms(qm[:, b + 128:b + 256], qpe_g[...], MLA_ROPE_DIM))
        mq_o[:, b:b + 128] = (nope * mla_scale).astype(BF16)
        mq_o[:, b + 128:b + 256] = (pe * mla_scale).astype(BF16)
    ckv = _rms(proj_ref[:, C_CKV:C_CKV + MLA_KV_RANK], kva_g[...], MLA_KV_RANK).astype(BF16)
    kv = jnp.dot(ckv, wkvb_ref[...], preferred_element_type=F32)
    kpe = rope_m(_rms(proj_ref[:, C_KPE:C_KPE + 128], kpe_g[...], MLA_ROPE_DIM))
    kpe = kpe.astype(BF16)
    for h in range(MLA_HEADS):
        b = h * MLA_SLOT
        kn = _rms(kv[:, b:b + 128], kn_g[...], MLA_NOPE_DIM)
        mk_o[:, b:b + 128] = kn.astype(BF16)
        mk_o[:, b + 128:b + 256] = kpe
        mvT_o[h * VT_ROWS:h * VT_ROWS + MLA_V_DIM, :] = kv[:, b + 128:b + 256].T.astype(BF16)
        mvT_o[h * VT_ROWS + MLA_V_DIM:(h + 1) * VT_ROWS, :] = ones_row

    cd, sad, sbd = rope_d_ref[:, 0:128], rope_d_ref[:, 128:256], rope_d_ref[:, 256:384]

    def rope_d(y):
        return y * cd + pltpu.roll(y, 120, 1) * sad + pltpu.roll(y, 8, 1) * sbd

    def group_rms(x, g):
        x2 = x * x
        s_lo = jnp.sum(jnp.where(lo, x2, 0.0), axis=-1, keepdims=True)
        s_hi = jnp.sum(jnp.where(lo, 0.0, x2), axis=-1, keepdims=True)
        ms = jnp.where(lo, s_lo, s_hi) * (1.0 / DIFF_QK_DIM)
        return x * lax.rsqrt(ms + EPS) * g

    def split_components(y):
        return jnp.where(lo, y, 0.0), jnp.where(lo, pltpu.roll(y, 64, 1), 0.0)

    df_scale = DIFF_QK_DIM ** -0.5 * LOG2E
    for h in range(DIFF_HEADS):
        q = rope_d(group_rms(proj_ref[:, C_DQ + h * 128:C_DQ + (h + 1) * 128], dq_g[...])) * df_scale
        q0, q1 = split_components(q)
        dq_o[:, (2 * h) * 128:(2 * h + 1) * 128] = q0.astype(BF16)
        dq_o[:, (2 * h + 1) * 128:(2 * h + 2) * 128] = q1.astype(BF16)
        k = rope_d(group_rms(proj_ref[:, C_DK + h * 128:C_DK + (h + 1) * 128], dk_g[...]))
        k0, k1 = split_components(k)
        dk_o[:, (2 * h) * 128:(2 * h + 1) * 128] = k0.astype(BF16)
        dk_o[:, (2 * h + 1) * 128:(2 * h + 2) * 128] = k1.astype(BF16)
        v = proj_ref[:, C_DV + h * DIFF_V_DIM:C_DV + (h + 1) * DIFF_V_DIM]
        dvT_o[h * VT_ROWS:h * VT_ROWS + DIFF_V_DIM, :] = v.T.astype(BF16)
        dvT_o[h * VT_ROWS + DIFF_V_DIM:(h + 1) * VT_ROWS, :] = ones_row


def head_prep(proj, rope_m, rope_d, lw, *, seq_len, tm):
    M = proj.shape[0]
    nt = seq_len // tm
    row = lambda w: pl.BlockSpec((tm, w), lambda i: (i, 0))
    colT = lambda h: pl.BlockSpec((h, tm), lambda i: (0, i))
    full = lambda a: pl.BlockSpec(a.shape, lambda i: (0,) * a.ndim)
    rope = pl.BlockSpec((tm, 384), lambda i: (i % nt, 0))
    params = (lw["naq_g"], lw["nak_g"], lw["qa_g"], lw["wqb"], lw["kva_g"], lw["wkvb"],
              lw["qn_g"], lw["qpe_g"], lw["kn_g"], lw["kpe_g"], lw["dq_g"], lw["dk_g"])
    out_shapes = (
        jax.ShapeDtypeStruct((M, NA_W), BF16), jax.ShapeDtypeStruct((M, NA_W), BF16),
        jax.ShapeDtypeStruct((NA_HEADS * VT_ROWS, M), BF16),
        jax.ShapeDtypeStruct((M, MLA_HEADS * MLA_SLOT), BF16),
        jax.ShapeDtypeStruct((M, MLA_HEADS * MLA_SLOT), BF16),
        jax.ShapeDtypeStruct((MLA_HEADS * VT_ROWS, M), BF16),
        jax.ShapeDtypeStruct((M, 2 * DIFF_HEADS * 128), BF16),
        jax.ShapeDtypeStruct((M, 2 * DIFF_HEADS * 128), BF16),
        jax.ShapeDtypeStruct((DIFF_HEADS * VT_ROWS, M), BF16),
    )
    out_specs = (row(NA_W), row(NA_W), colT(NA_HEADS * VT_ROWS), row(MLA_HEADS * MLA_SLOT), row(MLA_HEADS * MLA_SLOT),
                 colT(MLA_HEADS * VT_ROWS), row(2 * DIFF_HEADS * 128), row(2 * DIFF_HEADS * 128),
                 colT(DIFF_HEADS * VT_ROWS))
    return pl.pallas_call(
        _prep_kernel,
        grid=(M // tm,),
        in_specs=[row(IN_COLS_PAD), rope, rope] + [full(a) for a in params],
        out_specs=out_specs,
        out_shape=out_shapes,
        compiler_params=_cparams(("parallel",)),
        name="head_prep",
    )(proj, rope_m, rope_d, *params)


def _flash_kernel(q_ref, k_ref, vT_ref, o_ref, sa_ref, sb_ref, acc_ref, *, tq, tk):
    seq_len = q_ref.shape[0]
    dv = o_ref.shape[1]
    nk = seq_len // tk
    total = (seq_len // tq) * nk
    sub = 16

    def scores(f):
        qoff = pl.multiple_of((f // nk) * tq, tq)
        koff = pl.multiple_of((f % nk) * tk, tk)
        s = lax.dot_general(k_ref[pl.ds(koff, tk), :], q_ref[pl.ds(qoff, tq), :],
                            (((1,), (1,)), ((), ())), preferred_element_type=F32)
        return s.astype(BF16)

    def softmax_pv(s_ref, c, m):
        s = s_ref[...]
        part = jnp.max(s.reshape(tk // sub, sub, tq), axis=0).astype(F32)
        m_new = jnp.maximum(m, jnp.max(part, axis=0, keepdims=True))
        alpha = jnp.exp2(m - m_new)
        p = jnp.exp2(s - m_new.astype(BF16))
        koff = pl.multiple_of(c * tk, tk)
        pv = jnp.dot(vT_ref[:, pl.ds(koff, tk)], p, preferred_element_type=F32)
        acc_ref[...] = alpha * acc_ref[...] + pv
        return m_new

    def pair(i, m):
        f = 2 * i
        c = f % nk
        m = jnp.where(c == 0, NEG_BIG, m)
        sb_ref[...] = scores(f + 1)
        m = softmax_pv(sa_ref, c, m)
        sa_ref[...] = scores(jnp.minimum(f + 2, total - 1))
        m = softmax_pv(sb_ref, c + 1, m)

        @pl.when(c + 2 == nk)
        def _():
            qoff = pl.multiple_of((f // nk) * tq, tq)
            out = acc_ref[0:dv, :] / acc_ref[dv:dv + 1, :]
            o_ref[pl.ds(qoff, tq), :] = out.T.astype(o_ref.dtype)

        return m

    acc_ref[...] = jnp.zeros_like(acc_ref)
    sa_ref[...] = scores(0)
    lax.fori_loop(0, total // 2, pair, jnp.full((1, tq), NEG_BIG, F32))


def flash_attention(q, k, vT, *, batch, seq_len, n_heads, dq, v_of_head, tq, tk, out_dtype):
    M = q.shape[0]
    dv = 128
    assert seq_len % (2 * tk) == 0 and seq_len % tq == 0
    once = pl.Buffered(1 if seq_len * dq * 2 > 2 * 1024 * 1024 else 2)
    return pl.pallas_call(
        functools.partial(_flash_kernel, tq=tq, tk=tk),
        grid=(batch, n_heads),
        in_specs=[pl.BlockSpec((seq_len, dq), lambda b, h: (b, h), pipeline_mode=once),
                  pl.BlockSpec((seq_len, dq), lambda b, h: (b, h), pipeline_mode=once),
                  pl.BlockSpec((VT_ROWS, seq_len), lambda b, h: (v_of_head(h), b), pipeline_mode=once)],
        out_specs=pl.BlockSpec((seq_len, dv), lambda b, h: (b, h)),
        out_shape=jax.ShapeDtypeStruct((M, n_heads * dv), out_dtype),
        scratch_shapes=[pltpu.VMEM((tk, tq), BF16), pltpu.VMEM((tk, tq), BF16), pltpu.VMEM((VT_ROWS, tq), F32)],
        compiler_params=_cparams(("parallel", "parallel")),
        name="flash_attention",
    )(q, k, vT)


NA_QROWS = 8
NA_KROWS = 2 * NA_WIN_ROWS


def _na_kernel(q_ref, k_ref, vT_ref, bias_ref, o_ref, *, rows):
    i = pl.program_id(2)
    ws = jnp.clip(i * NA_QROWS - NA_WIN_ROWS // 2, 0, rows - NA_KROWS)
    ks = pl.multiple_of(ws * GRID_W, (NA_WIN_ROWS // 2) * GRID_W)
    win = NA_KROWS * GRID_W
    dv = o_ref.shape[1]
    sub = 16
    s = lax.dot_general(k_ref[pl.ds(ks, win), :], q_ref[...], (((1,), (1,)), ((), ())),
                        preferred_element_type=F32)
    s = (s + bias_ref[...]).astype(BF16)
    part = jnp.max(s.reshape(win // sub, sub, s.shape[1]), axis=0).astype(F32)
    m = jnp.max(part, axis=0, keepdims=True)
    p = jnp.exp2(s - m.astype(BF16))
    acc = jnp.dot(vT_ref[:, pl.ds(ks, win)], p, preferred_element_type=F32)
    o_ref[...] = (acc[0:dv, :] / acc[dv:dv + 1, :]).T.astype(o_ref.dtype)


def na_attention(q, k, vT, bias, *, batch, seq_len):
    M = q.shape[0]
    rows = seq_len // GRID_W
    assert rows >= NA_KROWS and rows % NA_QROWS == 0
    nblk = rows // NA_QROWS
    tq = NA_QROWS * GRID_W
    pattern = lambda i: jnp.where(i == 0, 0, jnp.where(i == nblk - 1, 2, 1))
    return pl.pallas_call(
        functools.partial(_na_kernel, rows=rows),
        grid=(batch, NA_HEADS, nblk),
        in_specs=[pl.BlockSpec((tq, HEAD_DIM), lambda b, h, i: (b * nblk + i, h)),
                  pl.BlockSpec((seq_len, HEAD_DIM), lambda b, h, i: (b, h)),
                  pl.BlockSpec((VT_ROWS, seq_len), lambda b, h, i: (h, b)),
                  pl.BlockSpec((None, None, NA_KROWS * GRID_W, tq), lambda b, h, i: (h, pattern(i), 0, 0))],
        out_specs=pl.BlockSpec((tq, HEAD_DIM), lambda b, h, i: (b * nblk + i, h)),
        out_shape=jax.ShapeDtypeStruct((M, NA_W), BF16),
        compiler_params=_cparams(("parallel", "parallel", "arbitrary")),
        name="na_attention",
    )(q, k, vT, bias)


def _out_proj_kernel(x_ref, ona_ref, omla_ref, odf_ref, subln_ref, lq1, lk1, lq2, lk2,
                     w_na, w_mla, w_df, o_ref, *, lam_init):
    lam = (jnp.exp(jnp.sum(lq1[...] * lk1[...], axis=-1, keepdims=True))
           - jnp.exp(jnp.sum(lq2[...] * lk2[...], axis=-1, keepdims=True)) + lam_init)
    heads = []
    for h in range(DIFF_HEADS):
        o1 = odf_ref[:, (2 * h) * 128:(2 * h + 1) * 128]
        o2 = odf_ref[:, (2 * h + 1) * 128:(2 * h + 2) * 128]
        o = _rms(o1 - lam * o2, subln_ref[...], DIFF_V_DIM) * (1.0 - lam_init)
        heads.append(o.astype(BF16))
    odf = jnp.concatenate(heads, axis=-1)
    y = jnp.dot(ona_ref[...], w_na[...], preferred_element_type=F32)
    y = y + jnp.dot(omla_ref[...], w_mla[...], preferred_element_type=F32)
    y = y + jnp.dot(odf, w_df[...], preferred_element_type=F32)
    o_ref[...] = x_ref[...] + y


def out_proj(x, o_na, o_mla, o_df, lw, *, lam_init, tm):
    M, D = x.shape
    row = lambda w: pl.BlockSpec((tm, w), lambda i: (i, 0))
    full = lambda a: pl.BlockSpec(a.shape, lambda i: (0,) * a.ndim)
    params = (lw["subln_g"], lw["lq1"], lw["lk1"], lw["lq2"], lw["lk2"], lw["w_out_na"], lw["w_out_mla"], lw["w_out_df"])
    return pl.pallas_call(
        functools.partial(_out_proj_kernel, lam_init=lam_init),
        grid=(M // tm,),
        in_specs=[row(D), row(NA_W), row(MLA_VW), row(2 * DIFF_HEADS * 128)] + [full(a) for a in params],
        out_specs=row(D),
        out_shape=jax.ShapeDtypeStruct((M, D), F32),
        compiler_params=_cparams(("parallel",)),
        name="out_proj",
    )(x, o_na, o_mla, o_df, *params)


HALO = 16


def _ffn_kernel(x_ref, xp_ref, xn_ref, g_ref, wg_ref, wu_ref, cw_ref, cb_ref, wd_ref, o_ref,
                xs_ref, acc_ref, *, seq_len):
    i = pl.program_id(0)
    f = pl.program_id(1)
    tm = x_ref.shape[0]

    @pl.when(f == 0)
    def _():
        d = x_ref.shape[-1]
        row0 = i * tm
        at_start = (row0 % seq_len) == 0
        at_end = ((row0 + tm) % seq_len) == 0
        prev = _rms(xp_ref[...], g_ref[...], d)
        nxt = _rms(xn_ref[...], g_ref[...], d)
        xs_ref[0:HALO, :] = jnp.where(at_start, 0.0, prev).astype(BF16)
        xs_ref[HALO:HALO + tm, :] = _rms(x_ref[...], g_ref[...], d).astype(BF16)
        xs_ref[HALO + tm:2 * HALO + tm, :] = jnp.where(at_end, 0.0, nxt).astype(BF16)
        acc_ref[...] = jnp.zeros_like(acc_ref)

    n_ext = tm + 2 * HALO
    gate = jnp.dot(xs_ref[...], wg_ref[...], preferred_element_type=F32)
    g_prev = pltpu.roll(gate, 1, 0)[HALO:HALO + tm, :]
    g_next = pltpu.roll(gate, n_ext - 1, 0)[HALO:HALO + tm, :]
    g_mid = gate[HALO:HALO + tm, :]
    g = g_prev * cw_ref[0:1, :] + g_mid * cw_ref[1:2, :] + g_next * cw_ref[2:3, :] + cb_ref[...]
    u = jnp.dot(xs_ref[HALO:HALO + tm, :], wu_ref[...], preferred_element_type=F32)
    a = (g * jax.nn.sigmoid(g) * u).astype(BF16)
    acc_ref[...] += jnp.dot(a, wd_ref[...], preferred_element_type=F32)

    @pl.when(f == pl.num_programs(1) - 1)
    def _():
        o_ref[...] = x_ref[...] + acc_ref[...]


def ffn(x, lw, *, seq_len, tm):
    M, D = x.shape
    nf, _, tf = lw["w_gate"].shape
    assert seq_len % tm == 0 and tm % HALO == 0
    hb = tm // HALO
    last = M // HALO - 1
    return pl.pallas_call(
        functools.partial(_ffn_kernel, seq_len=seq_len),
        grid=(M // tm, nf),
        in_specs=[pl.BlockSpec((tm, D), lambda i, f: (i, 0)),
                  pl.BlockSpec((HALO, D), lambda i, f: (jnp.maximum(i * hb - 1, 0), 0)),
                  pl.BlockSpec((HALO, D), lambda i, f: (jnp.minimum((i + 1) * hb, last), 0)),
                  pl.BlockSpec((1, D), lambda i, f: (0, 0)),
                  pl.BlockSpec((None, D, tf), lambda i, f: (f, 0, 0)),
                  pl.BlockSpec((None, D, tf), lambda i, f: (f, 0, 0)),
                  pl.BlockSpec((3, tf), lambda i, f: (0, f)),
                  pl.BlockSpec((1, tf), lambda i, f: (0, f)),
                  pl.BlockSpec((tf, D), lambda i, f: (f, 0))],
        out_specs=pl.BlockSpec((tm, D), lambda i, f: (i, 0)),
        out_shape=jax.ShapeDtypeStruct((M, D), F32),
        scratch_shapes=[pltpu.VMEM((tm + 2 * HALO, D), BF16), pltpu.VMEM((tm, D), F32)],
        compiler_params=_cparams(("parallel", "arbitrary")),
        name="ffn",
    )(x, x, x, lw["ffn_g"], lw["w_gate"], lw["w_up"], lw["conv_w"], lw["conv_b"], lw["w_down"])


def _ple_kernel(x_ref, p_ref, g_ref, wg_ref, wp_ref, o_ref):
    x = x_ref[...]
    xn = _rms(x, g_ref[...], x.shape[-1]).astype(BF16)
    gate = jax.nn.sigmoid(jnp.dot(xn, wg_ref[...], preferred_element_type=F32))
    proj = jnp.dot(p_ref[...].astype(BF16), wp_ref[...], preferred_element_type=F32)
    o_ref[...] = x + gate * proj


def ple(x, p, lw, *, tm):
    M, D = x.shape
    row = lambda w: pl.BlockSpec((tm, w), lambda i: (i, 0))
    full = lambda a: pl.BlockSpec(a.shape, lambda i: (0,) * a.ndim)
    params = (lw["ple_g"], lw["w_ple_gate"], lw["w_ple_proj"])
    return pl.pallas_call(
        _ple_kernel,
        grid=(M // tm,),
        in_specs=[row(D), row(p.shape[1])] + [full(a) for a in params],
        out_specs=row(D),
        out_shape=jax.ShapeDtypeStruct((M, D), F32),
        compiler_params=_cparams(("parallel",)),
        name="ple",
    )(x, p, *params)


def _rope_tables(seq_len):
    def angles(dim, theta):
        inv = 1.0 / (theta ** (jnp.arange(0, dim, 2, dtype=F32) / dim))
        ang = jnp.arange(seq_len, dtype=F32)[:, None] * inv[None, :]
        return jnp.cos(ang), jnp.sin(ang)

    z = lambda w: jnp.zeros((seq_len, w), F32)
    cos, sin = angles(MLA_ROPE_DIM, MLA_ROPE_THETA)
    half = MLA_ROPE_DIM // 2
    cm = jnp.concatenate([cos, cos, z(64)], axis=1)
    sam = jnp.concatenate([-sin, z(half), z(64)], axis=1)
    sbm = jnp.concatenate([z(half), sin, z(64)], axis=1)
    rope_m = jnp.concatenate([cm, sam, sbm], axis=1)

    cos, sin = angles(ROPE_PART_DIM, ROPE_THETA)
    half = ROPE_PART_DIM // 2
    rest = DIFF_QK_DIM - ROPE_PART_DIM
    c64 = jnp.concatenate([cos, cos, jnp.ones((seq_len, rest), F32)], axis=1)
    sa64 = jnp.concatenate([-sin, z(half), z(rest)], axis=1)
    sb64 = jnp.concatenate([z(half), sin, z(rest)], axis=1)
    rope_d = jnp.concatenate([c64, c64, sa64, sa64, sb64, sb64], axis=1)
    return rope_m, rope_d


def _na_bias_table(rpb):
    cols = jnp.arange(GRID_W)
    cs = jnp.clip(cols - NA_WIN_COLS // 2, 0, GRID_W - NA_WIN_COLS)
    kc = jnp.arange(GRID_W)
    valid = (kc[None, :] >= cs[:, None]) & (kc[None, :] < cs[:, None] + NA_WIN_COLS)
    nv = 2 * NA_WIN_COLS - 1
    onehot = (kc[None, None, :] - cols[None, :, None] + (NA_WIN_COLS - 1) == jnp.arange(nv)[:, None, None])
    toep = jnp.einsum("huv,vck->huck", rpb.astype(F32), onehot.astype(F32), precision=lax.Precision.HIGHEST)
    toep = jnp.where(valid[None, None], toep * LOG2E, NEG_BIG)
    n_h = rpb.shape[0]
    masked = 2 * NA_WIN_ROWS - 1
    toep = jnp.concatenate([toep, jnp.full((n_h, 1, GRID_W, GRID_W), NEG_BIG, F32)], axis=1)
    half = NA_WIN_ROWS // 2
    tile = np.full((3, NA_QROWS, NA_KROWS), masked, np.int32)
    for pat, delta in enumerate((0, -half, -NA_WIN_ROWS)):
        for a in range(NA_QROWS):
            first = (max(a - half, 0), a, min(a + half, NA_WIN_ROWS))[pat]
            for i in range(first, first + NA_WIN_ROWS):
                tile[pat, a, i] = delta + i - a + (NA_WIN_ROWS - 1)
    tab = jnp.take(toep, jnp.asarray(tile.reshape(-1)), axis=1)
    tab = tab.reshape(n_h, 3, NA_QROWS, NA_KROWS, GRID_W, GRID_W).transpose(0, 1, 3, 5, 2, 4)
    return tab.reshape(n_h, 3, NA_KROWS * GRID_W, NA_QROWS * GRID_W)


def _layer_params(i, norm_mix, w_in, na_q_norm, na_k_norm, na_rpb, mla_q_a_norm, mla_w_q_b, mla_kv_a_norm,
                  mla_w_kv_b, mla_q_nope_norm, mla_q_pe_norm, mla_k_nope_norm, mla_k_pe_norm,
                  diff_q_norm, diff_k_norm, diff_lambda_q1, diff_lambda_k1, diff_lambda_q2, diff_lambda_k2,
                  diff_subln, w_out, norm_ffn, w_gate, w_up, conv_w, conv_b, w_down,
                  ple_norm, w_ple_gate, w_ple_proj):
    r = lambda a: a[i].reshape(1, -1).astype(F32)
    pad_to = lambda a, n: jnp.pad(a, ((0, 0), (0, n - a.shape[1])))
    w = w_in[i]
    kpe0 = 3 * NA_W + MLA_Q_RANK + MLA_KV_RANK
    w_perm = jnp.concatenate([w[:, :kpe0], w[:, kpe0 + MLA_ROPE_DIM:], w[:, kpe0:kpe0 + MLA_ROPE_DIM],
                              jnp.zeros((w.shape[0], 128 - MLA_ROPE_DIM), w.dtype)], axis=1).astype(BF16)
    wqb = mla_w_q_b[i].reshape(MLA_Q_RANK, MLA_HEADS, MLA_QK_DIM)
    wqb = jnp.pad(wqb, ((0, 0), (0, 0), (0, MLA_SLOT - MLA_QK_DIM))).reshape(MLA_Q_RANK, MLA_HEADS * MLA_SLOT)
    wo = w_out[i].astype(BF16)
    return dict(
        mix_g=r(norm_mix), w_in=_col_tiles(w_perm, IN_TILE),
        naq_g=r(na_q_norm), nak_g=r(na_k_norm), na_bias=_na_bias_table(na_rpb[i]),
        qa_g=r(mla_q_a_norm), wqb=wqb.astype(BF16), kva_g=r(mla_kv_a_norm), wkvb=mla_w_kv_b[i].astype(BF16),
        qn_g=r(mla_q_nope_norm), qpe_g=pad_to(r(mla_q_pe_norm), 128),
        kn_g=r(mla_k_nope_norm), kpe_g=pad_to(r(mla_k_pe_norm), 128),
        dq_g=jnp.tile(r(diff_q_norm), (1, 2)), dk_g=jnp.tile(r(diff_k_norm), (1, 2)),
        lq1=r(diff_lambda_q1), lk1=r(diff_lambda_k1), lq2=r(diff_lambda_q2), lk2=r(diff_lambda_k2),
        subln_g=r(diff_subln),
        w_out_na=wo[:NA_W], w_out_mla=wo[NA_W:NA_W + MLA_VW], w_out_df=wo[NA_W + MLA_VW:],
        ffn_g=r(norm_ffn), w_gate=_col_tiles(w_gate[i].astype(BF16), FF_TILE),
        w_up=_col_tiles(w_up[i].astype(BF16), FF_TILE),
        conv_w=conv_w[i].astype(F32), conv_b=r(conv_b), w_down=w_down[i].astype(BF16),
        ple_g=r(ple_norm), w_ple_gate=w_ple_gate[i].astype(BF16), w_ple_proj=w_ple_proj[i].astype(BF16),
    )


def _tile(n, pref):
    t = min(pref, n)
    while n % t:
        t //= 2
    return t


def _encoder_layer(x, p_l, lw, rope_m, rope_d, *, layer_idx, batch, seq_len):
    tm = _tile(seq_len, 512)
    proj = rms_matmul(x, lw["mix_g"], lw["w_in"], tm=_tile(seq_len, 1024), out_dtype=F32)
    (na_q, na_k, na_vT, m_q, m_k, m_vT, d_q, d_k, d_vT) = head_prep(
        proj, rope_m, rope_d, lw, seq_len=seq_len, tm=tm)
    o_na = na_attention(na_q, na_k, na_vT, lw["na_bias"], batch=batch, seq_len=seq_len)
    tq = _tile(seq_len, 512)
    tk = _tile(seq_len // 2, 8192)
    o_mla = flash_attention(m_q, m_k, m_vT, batch=batch, seq_len=seq_len, n_heads=MLA_HEADS, dq=MLA_SLOT,
                            v_of_head=lambda h: h, tq=tq, tk=tk, out_dtype=BF16)
    o_df = flash_attention(d_q, d_k, d_vT, batch=batch, seq_len=seq_len, n_heads=2 * DIFF_HEADS, dq=128,
                           v_of_head=lambda h: h // 2, tq=tq, tk=tk, out_dtype=F32)
    lam_init = 0.8 - 0.6 * math.exp(-0.3 * layer_idx)
    x = out_proj(x, o_na, o_mla, o_df, lw, lam_init=lam_init, tm=tm)
    x = ffn(x, lw, seq_len=seq_len, tm=tm)
    x = ple(x, p_l, lw, tm=tm)
    return x


def kernel(x_prompt, x_sample, p_prompt, p_sample, norm_mix, w_in, na_q_norm, na_k_norm, na_rpb, mla_q_a_norm, mla_w_q_b, mla_kv_a_norm, mla_w_kv_b, mla_q_nope_norm, mla_q_pe_norm, mla_k_nope_norm, mla_k_pe_norm, diff_q_norm, diff_k_norm, diff_lambda_q1, diff_lambda_k1, diff_lambda_q2, diff_lambda_k2, diff_subln, w_out, norm_ffn, w_gate, w_up, conv_w, conv_b, w_down, ple_norm, w_ple_gate, w_ple_proj):
    weights = (norm_mix, w_in, na_q_norm, na_k_norm, na_rpb, mla_q_a_norm, mla_w_q_b, mla_kv_a_norm, mla_w_kv_b,
               mla_q_nope_norm, mla_q_pe_norm, mla_k_nope_norm, mla_k_pe_norm, diff_q_norm, diff_k_norm,
               diff_lambda_q1, diff_lambda_k1, diff_lambda_q2, diff_lambda_k2, diff_subln, w_out,
               norm_ffn, w_gate, w_up, conv_w, conv_b, w_down, ple_norm, w_ple_gate, w_ple_proj)
    depth = norm_mix.shape[0]
    groups = []
    for x, p in ((x_prompt, p_prompt), (x_sample, p_sample)):
        b, t, d = x.shape
        groups.append(dict(x=x.reshape(b * t, d), p=p.reshape(depth, b * t, p.shape[-1]), batch=b, seq_len=t,
                           rope=_rope_tables(t), shape=x.shape))
    for i in range(depth):
        lw = _layer_params(i, *weights)
        for g in groups:
            g["x"] = _encoder_layer(g["x"], g["p"][i], lw, *g["rope"], layer_idx=i,
                                    batch=g["batch"], seq_len=g["seq_len"])
    return tuple(g["x"].reshape(g["shape"]) for g in groups)
```

```python
import functools
import math

import jax
import jax.numpy as jnp
import numpy as np
from jax import lax
from jax.experimental import pallas as pl
from jax.experimental.pallas import tpu as pltpu

F32 = jnp.float32
BF16 = jnp.bfloat16

EPS = 1e-6
LOG2E = 1.4426950408889634
NEG_BIG = -1e30

D_MODEL = 2048
PLE_DIM = 256
GRID_W = 64
HEAD_DIM = 128
NA_HEADS = 6
NA_WIN_ROWS = 8
NA_WIN_COLS = 16
MLA_HEADS = 5
MLA_Q_RANK = 512
MLA_KV_RANK = 256
MLA_NOPE_DIM = 128
MLA_ROPE_DIM = 64
MLA_V_DIM = 128
MLA_ROPE_THETA = 10000.0
DIFF_HEADS = 5
DIFF_QK_DIM = 64
DIFF_V_DIM = 128
ROPE_THETA = 500000.0
ROPE_PART_DIM = DIFF_QK_DIM // 4
D_FF = 5632

NA_W = NA_HEADS * HEAD_DIM
MLA_QK_DIM = MLA_NOPE_DIM + MLA_ROPE_DIM
MLA_SLOT = 256
DIFF_W = DIFF_HEADS * 2 * DIFF_QK_DIM
DIFF_VW = DIFF_HEADS * DIFF_V_DIM
MLA_VW = MLA_HEADS * MLA_V_DIM
VT_ROWS = 128 + 16

C_NAQ = 0
C_NAK = C_NAQ + NA_W
C_NAV = C_NAK + NA_W
C_CQ = C_NAV + NA_W
C_CKV = C_CQ + MLA_Q_RANK
C_DQ = C_CKV + MLA_KV_RANK
C_DK = C_DQ + DIFF_W
C_DV = C_DK + DIFF_W
C_KPE = C_DV + DIFF_VW
IN_COLS_PAD = C_KPE + 128

LANES = 128
IN_TILE = 1024
FF_TILE = 512
VMEM_LIMIT = 56 * 1024 * 1024


def _cparams(sem):
    return pltpu.CompilerParams(dimension_semantics=sem, vmem_limit_bytes=VMEM_LIMIT)


def _rms(x, g, n):
    ms = jnp.sum(x * x, axis=-1, keepdims=True) * (1.0 / n)
    return x * lax.rsqrt(ms + EPS) * g


def _rms_matmul_kernel(x_ref, g_ref, w_ref, o_ref, xn_ref):
    @pl.when(pl.program_id(1) == 0)
    def _():
        xn_ref[...] = _rms(x_ref[...], g_ref[...], x_ref.shape[-1]).astype(BF16)

    o_ref[...] = jnp.dot(xn_ref[...], w_ref[...], preferred_element_type=F32).astype(o_ref.dtype)


def _col_tiles(w, tn):
    k, n = w.shape
    return w.reshape(k, n // tn, tn).transpose(1, 0, 2)


def rms_matmul(x, g, w, *, tm, out_dtype):
    M, K = x.shape
    nt, _, tn = w.shape
    N = nt * tn
    return pl.pallas_call(
        _rms_matmul_kernel,
        grid=(M // tm, nt),
        in_specs=[pl.BlockSpec((tm, K), lambda i, j: (i, 0)),
                  pl.BlockSpec((1, K), lambda i, j: (0, 0)),
                  pl.BlockSpec((None, K, tn), lambda i, j: (j, 0, 0))],
        out_specs=pl.BlockSpec((tm, tn), lambda i, j: (i, j)),
        out_shape=jax.ShapeDtypeStruct((M, N), out_dtype),
        scratch_shapes=[pltpu.VMEM((tm, K), BF16)],
        compiler_params=_cparams(("parallel", "arbitrary")),
        name="rms_matmul",
    )(x, g, w)


def _prep_kernel(proj_ref, rope_m_ref, rope_d_ref, naq_g, nak_g, qa_g, wqb_ref, kva_g, wkvb_ref,
                 qn_g, qpe_g, kn_g, kpe_g, dq_g, dk_g,
                 naq_o, nak_o, navT_o, mq_o, mk_o, mvT_o, dq_o, dk_o, dvT_o):
    tm = proj_ref.shape[0]
    lane = lax.broadcasted_iota(jnp.int32, (tm, LANES), 1)
    lo = lane < 64
    ones_row = (lax.broadcasted_iota(jnp.int32, (VT_ROWS - 128, tm), 0) == 0).astype(BF16)

    na_scale = HEAD_DIM ** -0.5 * LOG2E
    for h in range(NA_HEADS):
        sl = slice(h * HEAD_DIM, (h + 1) * HEAD_DIM)
        q = _rms(proj_ref[:, C_NAQ + h * HEAD_DIM:C_NAQ + (h + 1) * HEAD_DIM], naq_g[...], HEAD_DIM)
        naq_o[:, sl] = (q * na_scale).astype(BF16)
        k = _rms(proj_ref[:, C_NAK + h * HEAD_DIM:C_NAK + (h + 1) * HEAD_DIM], nak_g[...], HEAD_DIM)
        nak_o[:, sl] = k.astype(BF16)
        v = proj_ref[:, C_NAV + h * HEAD_DIM:C_NAV + (h + 1) * HEAD_DIM]
        navT_o[h * VT_ROWS:h * VT_ROWS + HEAD_DIM, :] = v.T.astype(BF16)
        navT_o[h * VT_ROWS + HEAD_DIM:(h + 1) * VT_ROWS, :] = ones_row

    cm, sam, sbm = rope_m_ref[:, 0:128], rope_m_ref[:, 128:256], rope_m_ref[:, 256:384]

    def rope_m(y):
        return y * cm + pltpu.roll(y, 96, 1) * sam + pltpu.roll(y, 32, 1) * sbm

    mla_scale = MLA_QK_DIM ** -0.5 * LOG2E
    cq = _rms(proj_ref[:, C_CQ:C_CQ + MLA_Q_RANK], qa_g[...], MLA_Q_RANK).astype(BF16)
    qm = jnp.dot(cq, wqb_ref[...], preferred_element_type=F32)
    for h in range(MLA_HEADS):
        b = h * MLA_SLOT
        nope = _rms(qm[:, b:b + 128], qn_g[...], MLA_NOPE_DIM)
        pe = rope_m(_rms(qm[:, b + 128:b + 256], qpe_g[...], MLA_ROPE_DIM))
        mq_o[:, b:b + 128] = (nope * mla_scale).astype(BF16)
        mq_o[:, b + 128:b + 256] = (pe * mla_scale).astype(BF16)
    ckv = _rms(proj_ref[:, C_CKV:C_CKV + MLA_KV_RANK], kva_g[...], MLA_KV_RANK).astype(BF16)
    kv = jnp.dot(ckv, wkvb_ref[...], preferred_element_type=F32)
    kpe = rope_m(_rms(proj_ref[:, C_KPE:C_KPE + 128], kpe_g[...], MLA_ROPE_DIM))
    kpe = kpe.astype(BF16)
    for h in range(MLA_HEADS):
        b = h * MLA_SLOT
        kn = _rms(kv[:, b:b + 128], kn_g[...], MLA_NOPE_DIM)
        mk_o[:, b:b + 128] = kn.astype(BF16)
        mk_o[:, b + 128:b + 256] = kpe
        mvT_o[h * VT_ROWS:h * VT_ROWS + MLA_V_DIM, :] = kv[:, b + 128:b + 256].T.astype(BF16)
        mvT_o[h * VT_ROWS + MLA_V_DIM:(h + 1) * VT_ROWS, :] = ones_row

    cd, sad, sbd = rope_d_ref[:, 0:128], rope_d_ref[:, 128:256], rope_d_ref[:, 256:384]

    def rope_d(y):
        return y * cd + pltpu.roll(y, 120, 1) * sad + pltpu.roll(y, 8, 1) * sbd

    def group_rms(x, g):
        x2 = x * x
        s_lo = jnp.sum(jnp.where(lo, x2, 0.0), axis=-1, keepdims=True)
        s_hi = jnp.sum(jnp.where(lo, 0.0, x2), axis=-1, keepdims=True)
        ms = jnp.where(lo, s_lo, s_hi) * (1.0 / DIFF_QK_DIM)
        return x * lax.rsqrt(ms + EPS) * g

    def split_components(y):
        return jnp.where(lo, y, 0.0), jnp.where(lo, pltpu.roll(y, 64, 1), 0.0)

    df_scale = DIFF_QK_DIM ** -0.5 * LOG2E
    for h in range(DIFF_HEADS):
        q = rope_d(group_rms(proj_ref[:, C_DQ + h * 128:C_DQ + (h + 1) * 128], dq_g[...])) * df_scale
        q0, q1 = split_components(q)
        dq_o[:, (2 * h) * 128:(2 * h + 1) * 128] = q0.astype(BF16)
        dq_o[:, (2 * h + 1) * 128:(2 * h + 2) * 128] = q1.astype(BF16)
        k = rope_d(group_rms(proj_ref[:, C_DK + h * 128:C_DK + (h + 1) * 128], dk_g[...]))
        k0, k1 = split_components(k)
        dk_o[:, (2 * h) * 128:(2 * h + 1) * 128] = k0.astype(BF16)
        dk_o[:, (2 * h + 1) * 128:(2 * h + 2) * 128] = k1.astype(BF16)
        v = proj_ref[:, C_DV + h * DIFF_V_DIM:C_DV + (h + 1) * DIFF_V_DIM]
        dvT_o[h * VT_ROWS:h * VT_ROWS + DIFF_V_DIM, :] = v.T.astype(BF16)
        dvT_o[h * VT_ROWS + DIFF_V_DIM:(h + 1) * VT_ROWS, :] = ones_row


def head_prep(proj, rope_m, rope_d, lw, *, seq_len, tm):
    M = proj.shape[0]
    nt = seq_len // tm
    row = lambda w: pl.BlockSpec((tm, w), lambda i: (i, 0))
    colT = lambda h: pl.BlockSpec((h, tm), lambda i: (0, i))
    full = lambda a: pl.BlockSpec(a.shape, lambda i: (0,) * a.ndim)
    rope = pl.BlockSpec((tm, 384), lambda i: (i % nt, 0))
    params = (lw["naq_g"], lw["nak_g"], lw["qa_g"], lw["wqb"], lw["kva_g"], lw["wkvb"],
              lw["qn_g"], lw["qpe_g"], lw["kn_g"], lw["kpe_g"], lw["dq_g"], lw["dk_g"])
    out_shapes = (
        jax.ShapeDtypeStruct((M, NA_W), BF16), jax.ShapeDtypeStruct((M, NA_W), BF16),
        jax.ShapeDtypeStruct((NA_HEADS * VT_ROWS, M), BF16),
        jax.ShapeDtypeStruct((M, MLA_HEADS * MLA_SLOT), BF16),
        jax.ShapeDtypeStruct((M, MLA_HEADS * MLA_SLOT), BF16),
        jax.ShapeDtypeStruct((MLA_HEADS * VT_ROWS, M), BF16),
        jax.ShapeDtypeStruct((M, 2 * DIFF_HEADS * 128), BF16),
        jax.ShapeDtypeStruct((M, 2 * DIFF_HEADS * 128), BF16),
        jax.ShapeDtypeStruct((DIFF_HEADS * VT_ROWS, M), BF16),
    )
    out_specs = (row(NA_W), row(NA_W), colT(NA_HEADS * VT_ROWS), row(MLA_HEADS * MLA_SLOT), row(MLA_HEADS * MLA_SLOT),
                 colT(MLA_HEADS * VT_ROWS), row(2 * DIFF_HEADS * 128), row(2 * DIFF_HEADS * 128),
                 colT(DIFF_HEADS * VT_ROWS))
    return pl.pallas_call(
        _prep_kernel,
        grid=(M // tm,),
        in_specs=[row(IN_COLS_PAD), rope, rope] + [full(a) for a in params],
        out_specs=out_specs,
        out_shape=out_shapes,
        compiler_params=_cparams(("parallel",)),
        name="head_prep",
    )(proj, rope_m, rope_d, *params)


def _flash_kernel(q_ref, k_ref, vT_ref, o_ref, sa_ref, sb_ref, acc_ref, *, tq, tk):
    seq_len = q_ref.shape[0]
    dv = o_ref.shape[1]
    nk = seq_len // tk
    total = (seq_len // tq) * nk
    sub = 16

    def scores(f):
        qoff = pl.multiple_of((f // nk) * tq, tq)
        koff = pl.multiple_of((f % nk) * tk, tk)
        s = lax.dot_general(k_ref[pl.ds(koff, tk), :], q_ref[pl.ds(qoff, tq), :],
                            (((1,), (1,)), ((), ())), preferred_element_type=F32)
        return s.astype(BF16)

    def softmax_pv(s_ref, c, m):
        s = s_ref[...]
        part = jnp.max(s.reshape(tk // sub, sub, tq), axis=0).astype(F32)
        m_new = jnp.maximum(m, jnp.max(part, axis=0, keepdims=True))
        alpha = jnp.exp2(m - m_new)
        p = jnp.exp2(s - m_new.astype(BF16))
        koff = pl.multiple_of(c * tk, tk)
        pv = jnp.dot(vT_ref[:, pl.ds(koff, tk)], p, preferred_element_type=F32)
        acc_ref[...] = alpha * acc_ref[...] + pv
        return m_new

    def pair(i, m):
        f = 2 * i
        c = f % nk
        m = jnp.where(c == 0, NEG_BIG, m)
        sb_ref[...] = scores(f + 1)
        m = softmax_pv(sa_ref, c, m)
        sa_ref[...] = scores(jnp.minimum(f + 2, total - 1))
        m = softmax_pv(sb_ref, c + 1, m)

        @pl.when(c + 2 == nk)
        def _():
            qoff = pl.multiple_of((f // nk) * tq, tq)
            out = acc_ref[0:dv, :] / acc_ref[dv:dv + 1, :]
            o_ref[pl.ds(qoff, tq), :] = out.T.astype(o_ref.dtype)

        return m

    acc_ref[...] = jnp.zeros_like(acc_ref)
    sa_ref[...] = scores(0)
    lax.fori_loop(0, total // 2, pair, jnp.full((1, tq), NEG_BIG, F32))


def flash_attention(q, k, vT, *, batch, seq_len, n_heads, dq, v_of_head, tq, tk, out_dtype):
    M = q.shape[0]
    dv = 128
    assert seq_len % (2 * tk) == 0 and seq_len % tq == 0
    once = pl.Buffered(1 if seq_len * dq * 2 > 2 * 1024 * 1024 else 2)
    return pl.pallas_call(
        functools.partial(_flash_kernel, tq=tq, tk=tk),
        grid=(batch, n_heads),
        in_specs=[pl.BlockSpec((seq_len, dq), lambda b, h: (b, h), pipeline_mode=once),
                  pl.BlockSpec((seq_len, dq), lambda b, h: (b, h), pipeline_mode=once),
                  pl.BlockSpec((VT_ROWS, seq_len), lambda b, h: (v_of_head(h), b), pipeline_mode=once)],
        out_specs=pl.BlockSpec((seq_len, dv), lambda b, h: (b, h)),
        out_shape=jax.ShapeDtypeStruct((M, n_heads * dv), out_dtype),
        scratch_shapes=[pltpu.VMEM((tk, tq), BF16), pltpu.VMEM((tk, tq), BF16), pltpu.VMEM((VT_ROWS, tq), F32)],
        compiler_params=_cparams(("parallel", "parallel")),
        name="flash_attention",
    )(q, k, vT)


NA_QROWS = 8
NA_KROWS = 2 * NA_WIN_ROWS


def _na_kernel(q_ref, k_ref, vT_ref, bias_ref, o_ref, *, rows):
    i = pl.program_id(2)
    ws = jnp.clip(i * NA_QROWS - NA_WIN_ROWS // 2, 0, rows - NA_KROWS)
    ks = pl.multiple_of(ws * GRID_W, (NA_WIN_ROWS // 2) * GRID_W)
    win = NA_KROWS * GRID_W
    dv = o_ref.shape[1]
    sub = 16
    s = lax.dot_general(k_ref[pl.ds(ks, win), :], q_ref[...], (((1,), (1,)), ((), ())),
                        preferred_element_type=F32)
    s = (s + bias_ref[...]).astype(BF16)
    part = jnp.max(s.reshape(win // sub, sub, s.shape[1]), axis=0).astype(F32)
    m = jnp.max(part, axis=0, keepdims=True)
    p = jnp.exp2(s - m.astype(BF16))
    acc = jnp.dot(vT_ref[:, pl.ds(ks, win)], p, preferred_element_type=F32)
    o_ref[...] = (acc[0:dv, :] / acc[dv:dv + 1, :]).T.astype(o_ref.dtype)


def na_attention(q, k, vT, bias, *, batch, seq_len):
    M = q.shape[0]
    rows = seq_len // GRID_W
    assert rows >= NA_KROWS and rows % NA_QROWS == 0
    nblk = rows // NA_QROWS
    tq = NA_QROWS * GRID_W
    pattern = lambda i: jnp.where(i == 0, 0, jnp.where(i == nblk - 1, 2, 1))
    return pl.pallas_call(
        functools.partial(_na_kernel, rows=rows),
        grid=(batch, NA_HEADS, nblk),
        in_specs=[pl.BlockSpec((tq, HEAD_DIM), lambda b, h, i: (b * nblk + i, h)),
                  pl.BlockSpec((seq_len, HEAD_DIM), lambda b, h, i: (b, h)),
                  pl.BlockSpec((VT_ROWS, seq_len), lambda b, h, i: (h, b)),
                  pl.BlockSpec((None, None, NA_KROWS * GRID_W, tq), lambda b, h, i: (h, pattern(i), 0, 0))],
        out_specs=pl.BlockSpec((tq, HEAD_DIM), lambda b, h, i: (b * nblk + i, h)),
        out_shape=jax.ShapeDtypeStruct((M, NA_W), BF16),
        compiler_params=_cparams(("parallel", "parallel", "arbitrary")),
        name="na_attention",
    )(q, k, vT, bias)


def _out_proj_kernel(x_ref, ona_ref, omla_ref, odf_ref, subln_ref, lq1, lk1, lq2, lk2,
                     w_na, w_mla, w_df, o_ref, *, lam_init):
    lam = (jnp.exp(jnp.sum(lq1[...] * lk1[...], axis=-1, keepdims=True))
           - jnp.exp(jnp.sum(lq2[...] * lk2[...], axis=-1, keepdims=True)) + lam_init)
    heads = []
    for h in range(DIFF_HEADS):
        o1 = odf_ref[:, (2 * h) * 128:(2 * h + 1) * 128]
        o2 = odf_ref[:, (2 * h + 1) * 128:(2 * h + 2) * 128]
        o = _rms(o1 - lam * o2, subln_ref[...], DIFF_V_DIM) * (1.0 - lam_init)
        heads.append(o.astype(BF16))
    odf = jnp.concatenate(heads, axis=-1)
    y = jnp.dot(ona_ref[...], w_na[...], preferred_element_type=F32)
    y = y + jnp.dot(omla_ref[...], w_mla[...], preferred_element_type=F32)
    y = y + jnp.dot(odf, w_df[...], preferred_element_type=F32)
    o_ref[...] = x_ref[...] + y


def out_proj(x, o_na, o_mla, o_df, lw, *, lam_init, tm):
    M, D = x.shape
    row = lambda w: pl.BlockSpec((tm, w), lambda i: (i, 0))
    full = lambda a: pl.BlockSpec(a.shape, lambda i: (0,) * a.ndim)
    params = (lw["subln_g"], lw["lq1"], lw["lk1"], lw["lq2"], lw["lk2"], lw["w_out_na"], lw["w_out_mla"], lw["w_out_df"])
    return pl.pallas_call(
        functools.partial(_out_proj_kernel, lam_init=lam_init),
        grid=(M // tm,),
        in_specs=[row(D), row(NA_W), row(MLA_VW), row(2 * DIFF_HEADS * 128)] + [full(a) for a in params],
        out_specs=row(D),
        out_shape=jax.ShapeDtypeStruct((M, D), F32),
        compiler_params=_cparams(("parallel",)),
        name="out_proj",
    )(x, o_na, o_mla, o_df, *params)


HALO = 16


def _ffn_kernel(x_ref, xp_ref, xn_ref, g_ref, wg_ref, wu_ref, cw_ref, cb_ref, wd_ref,
                p_ref, pg_ref, wpg_ref, wpp_ref, o_ref, xs_ref, acc_ref, *, seq_len):
    i = pl.program_id(0)
    f = pl.program_id(1)
    tm = x_ref.shape[0]

    @pl.when(f == 0)
    def _():
        d = x_ref.shape[-1]
        row0 = i * tm
        at_start = (row0 % seq_len) == 0
        at_end = ((row0 + tm) % seq_len) == 0
        prev = _rms(xp_ref[...], g_ref[...], d)
        nxt = _rms(xn_ref[...], g_ref[...], d)
        xs_ref[0:HALO, :] = jnp.where(at_start, 0.0, prev).astype(BF16)
        xs_ref[HALO:HALO + tm, :] = _rms(x_ref[...], g_ref[...], d).astype(BF16)
        xs_ref[HALO + tm:2 * HALO + tm, :] = jnp.where(at_end, 0.0, nxt).astype(BF16)
        acc_ref[...] = jnp.zeros_like(acc_ref)

    n_ext = tm + 2 * HALO
    gate = jnp.dot(xs_ref[...], wg_ref[...], preferred_element_type=F32)
    g_prev = pltpu.roll(gate, 1, 0)[HALO:HALO + tm, :]
    g_next = pltpu.roll(gate, n_ext - 1, 0)[HALO:HALO + tm, :]
    g_mid = gate[HALO:HALO + tm, :]
    g = g_prev * cw_ref[0:1, :] + g_mid * cw_ref[1:2, :] + g_next * cw_ref[2:3, :] + cb_ref[...]
    u = jnp.dot(xs_ref[HALO:HALO + tm, :], wu_ref[...], preferred_element_type=F32)
    a = (g * jax.nn.sigmoid(g) * u).astype(BF16)
    acc_ref[...] += jnp.dot(a, wd_ref[...], preferred_element_type=F32)

    @pl.when(f == pl.num_programs(1) - 1)
    def _():
        y = x_ref[...] + acc_ref[...]
        yn = _rms(y, pg_ref[...], y.shape[-1]).astype(BF16)
        gate = jax.nn.sigmoid(jnp.dot(yn, wpg_ref[...], preferred_element_type=F32))
        emb = jnp.dot(p_ref[...].astype(BF16), wpp_ref[...], preferred_element_type=F32)
        o_ref[...] = y + gate * emb


def ffn_ple(x, p, lw, *, seq_len, tm):
    M, D = x.shape
    tf = FF_TILE
    nf = lw["w_gate"].shape[1] // tf
    assert seq_len % tm == 0 and tm % HALO == 0
    hb = tm // HALO
    last = M // HALO - 1
    return pl.pallas_call(
        functools.partial(_ffn_kernel, seq_len=seq_len),
        grid=(M // tm, nf),
        in_specs=[pl.BlockSpec((tm, D), lambda i, f: (i, 0)),
                  pl.BlockSpec((HALO, D), lambda i, f: (jnp.maximum(i * hb - 1, 0), 0)),
                  pl.BlockSpec((HALO, D), lambda i, f: (jnp.minimum((i + 1) * hb, last), 0)),
                  pl.BlockSpec((1, D), lambda i, f: (0, 0)),
                  pl.BlockSpec((D, tf), lambda i, f: (0, f)),
                  pl.BlockSpec((D, tf), lambda i, f: (0, f)),
                  pl.BlockSpec((3, tf), lambda i, f: (0, f)),
                  pl.BlockSpec((1, tf), lambda i, f: (0, f)),
                  pl.BlockSpec((tf, D), lambda i, f: (f, 0)),
                  pl.BlockSpec((tm, p.shape[1]), lambda i, f: (i, 0)),
                  pl.BlockSpec((1, D), lambda i, f: (0, 0)),
                  pl.BlockSpec((D, D), lambda i, f: (0, 0), pipeline_mode=pl.Buffered(1)),
                  pl.BlockSpec((p.shape[1], D), lambda i, f: (0, 0), pipeline_mode=pl.Buffered(1))],
        out_specs=pl.BlockSpec((tm, D), lambda i, f: (i, 0)),
        out_shape=jax.ShapeDtypeStruct((M, D), F32),
        scratch_shapes=[pltpu.VMEM((tm + 2 * HALO, D), BF16), pltpu.VMEM((tm, D), F32)],
        compiler_params=_cparams(("parallel", "arbitrary")),
        name="ffn_ple",
    )(x, x, x, lw["ffn_g"], lw["w_gate"], lw["w_up"], lw["conv_w"], lw["conv_b"], lw["w_down"],
      p, lw["ple_g"], lw["w_ple_gate"], lw["w_ple_proj"])


def _rope_tables(seq_len):
    def angles(dim, theta):
        inv = 1.0 / (theta ** (jnp.arange(0, dim, 2, dtype=F32) / dim))
        ang = jnp.arange(seq_len, dtype=F32)[:, None] * inv[None, :]
        return jnp.cos(ang), jnp.sin(ang)

    z = lambda w: jnp.zeros((seq_len, w), F32)
    cos, sin = angles(MLA_ROPE_DIM, MLA_ROPE_THETA)
    half = MLA_ROPE_DIM // 2
    cm = jnp.concatenate([cos, cos, z(64)], axis=1)
    sam = jnp.concatenate([-sin, z(half), z(64)], axis=1)
    sbm = jnp.concatenate([z(half), sin, z(64)], axis=1)
    rope_m = jnp.concatenate([cm, sam, sbm], axis=1)

    cos, sin = angles(ROPE_PART_DIM, ROPE_THETA)
    half = ROPE_PART_DIM // 2
    rest = DIFF_QK_DIM - ROPE_PART_DIM
    c64 = jnp.concatenate([cos, cos, jnp.ones((seq_len, rest), F32)], axis=1)
    sa64 = jnp.concatenate([-sin, z(half), z(rest)], axis=1)
    sb64 = jnp.concatenate([z(half), sin, z(rest)], axis=1)
    rope_d = jnp.concatenate([c64, c64, sa64, sa64, sb64, sb64], axis=1)
    return rope_m, rope_d


def _na_bias_table(rpb):
    cols = jnp.arange(GRID_W)
    cs = jnp.clip(cols - NA_WIN_COLS // 2, 0, GRID_W - NA_WIN_COLS)
    kc = jnp.arange(GRID_W)
    valid = (kc[None, :] >= cs[:, None]) & (kc[None, :] < cs[:, None] + NA_WIN_COLS)
    nv = 2 * NA_WIN_COLS - 1
    onehot = (kc[None, None, :] - cols[None, :, None] + (NA_WIN_COLS - 1) == jnp.arange(nv)[:, None, None])
    toep = jnp.einsum("huv,vck->huck", rpb.astype(F32), onehot.astype(F32), precision=lax.Precision.HIGHEST)
    toep = jnp.where(valid[None, None], toep * LOG2E, NEG_BIG)
    n_h = rpb.shape[0]
    masked = 2 * NA_WIN_ROWS - 1
    toep = jnp.concatenate([toep, jnp.full((n_h, 1, GRID_W, GRID_W), NEG_BIG, F32)], axis=1)
    half = NA_WIN_ROWS // 2
    tile = np.full((3, NA_QROWS, NA_KROWS), masked, np.int32)
    for pat, delta in enumerate((0, -half, -NA_WIN_ROWS)):
        for a in range(NA_QROWS):
            first = (max(a - half, 0), a, min(a + half, NA_WIN_ROWS))[pat]
            for i in range(first, first + NA_WIN_ROWS):
                tile[pat, a, i] = delta + i - a + (NA_WIN_ROWS - 1)
    tab = jnp.take(toep, jnp.asarray(tile.reshape(-1)), axis=1)
    tab = tab.reshape(n_h, 3, NA_QROWS, NA_KROWS, GRID_W, GRID_W).transpose(0, 1, 3, 5, 2, 4)
    return tab.reshape(n_h, 3, NA_KROWS * GRID_W, NA_QROWS * GRID_W)


def _layer_params(i, norm_mix, w_in, na_q_norm, na_k_norm, na_rpb, mla_q_a_norm, mla_w_q_b, mla_kv_a_norm,
                  mla_w_kv_b, mla_q_nope_norm, mla_q_pe_norm, mla_k_nope_norm, mla_k_pe_norm,
                  diff_q_norm, diff_k_norm, diff_lambda_q1, diff_lambda_k1, diff_lambda_q2, diff_lambda_k2,
                  diff_subln, w_out, norm_ffn, w_gate, w_up, conv_w, conv_b, w_down,
                  ple_norm, w_ple_gate, w_ple_proj):
    r = lambda a: a[i].reshape(1, -1).astype(F32)
    pad_to = lambda a, n: jnp.pad(a, ((0, 0), (0, n - a.shape[1])))
    w = w_in[i]
    kpe0 = 3 * NA_W + MLA_Q_RANK + MLA_KV_RANK
    w_perm = jnp.concatenate([w[:, :kpe0], w[:, kpe0 + MLA_ROPE_DIM:], w[:, kpe0:kpe0 + MLA_ROPE_DIM],
                              jnp.zeros((w.shape[0], 128 - MLA_ROPE_DIM), w.dtype)], axis=1).astype(BF16)
    wqb = mla_w_q_b[i].reshape(MLA_Q_RANK, MLA_HEADS, MLA_QK_DIM)
    wqb = jnp.pad(wqb, ((0, 0), (0, 0), (0, MLA_SLOT - MLA_QK_DIM))).reshape(MLA_Q_RANK, MLA_HEADS * MLA_SLOT)
    wo = w_out[i].astype(BF16)
    return dict(
        mix_g=r(norm_mix), w_in=_col_tiles(w_perm, IN_TILE),
        naq_g=r(na_q_norm), nak_g=r(na_k_norm), na_bias=_na_bias_table(na_rpb[i]),
        qa_g=r(mla_q_a_norm), wqb=wqb.astype(BF16), kva_g=r(mla_kv_a_norm), wkvb=mla_w_kv_b[i].astype(BF16),
        qn_g=r(mla_q_nope_norm), qpe_g=pad_to(r(mla_q_pe_norm), 128),
        kn_g=r(mla_k_nope_norm), kpe_g=pad_to(r(mla_k_pe_norm), 128),
        dq_g=jnp.tile(r(diff_q_norm), (1, 2)), dk_g=jnp.tile(r(diff_k_norm), (1, 2)),
        lq1=r(diff_lambda_q1), lk1=r(diff_lambda_k1), lq2=r(diff_lambda_q2), lk2=r(diff_lambda_k2),
        subln_g=r(diff_subln),
        w_out_na=wo[:NA_W], w_out_mla=wo[NA_W:NA_W + MLA_VW], w_out_df=wo[NA_W + MLA_VW:],
        ffn_g=r(norm_ffn), w_gate=w_gate[i].astype(BF16), w_up=w_up[i].astype(BF16),
        conv_w=conv_w[i].astype(F32), conv_b=r(conv_b), w_down=w_down[i].astype(BF16),
        ple_g=r(ple_norm), w_ple_gate=w_ple_gate[i].astype(BF16), w_ple_proj=w_ple_proj[i].astype(BF16),
    )


def _tile(n, pref):
    t = min(pref, n)
    while n % t:
        t //= 2
    return t


def _encoder_layer(x, p_l, lw, rope_m, rope_d, *, layer_idx, batch, seq_len):
    tm = _tile(seq_len, 512)
    proj = rms_matmul(x, lw["mix_g"], lw["w_in"], tm=_tile(seq_len, 1024), out_dtype=F32)
    (na_q, na_k, na_vT, m_q, m_k, m_vT, d_q, d_k, d_vT) = head_prep(
        proj, rope_m, rope_d, lw, seq_len=seq_len, tm=tm)
    o_na = na_attention(na_q, na_k, na_vT, lw["na_bias"], batch=batch, seq_len=seq_len)
    tq = _tile(seq_len, 512)
    tk = _tile(seq_len // 2, 8192)
    o_mla = flash_attention(m_q, m_k, m_vT, batch=batch, seq_len=seq_len, n_heads=MLA_HEADS, dq=MLA_SLOT,
                            v_of_head=lambda h: h, tq=tq, tk=tk, out_dtype=BF16)
    o_df = flash_attention(d_q, d_k, d_vT, batch=batch, seq_len=seq_len, n_heads=2 * DIFF_HEADS, dq=128,
                           v_of_head=lambda h: h // 2, tq=tq, tk=tk, out_dtype=F32)
    lam_init = 0.8 - 0.6 * math.exp(-0.3 * layer_idx)
    x = out_proj(x, o_na, o_mla, o_df, lw, lam_init=lam_init, tm=tm)
    return ffn_ple(x, p_l, lw, seq_len=seq_len, tm=tm)


def kernel(x_prompt, x_sample, p_prompt, p_sample, norm_mix, w_in, na_q_norm, na_k_norm, na_rpb, mla_q_a_norm, mla_w_q_b, mla_kv_a_norm, mla_w_kv_b, mla_q_nope_norm, mla_q_pe_norm, mla_k_nope_norm, mla_k_pe_norm, diff_q_norm, diff_k_norm, diff_lambda_q1, diff_lambda_k1, diff_lambda_q2, diff_lambda_k2, diff_subln, w_out, norm_ffn, w_gate, w_up, conv_w, conv_b, w_down, ple_norm, w_ple_gate, w_ple_proj):
    weights = (norm_mix, w_in, na_q_norm, na_k_norm, na_rpb, mla_q_a_norm, mla_w_q_b, mla_kv_a_norm, mla_w_kv_b,
               mla_q_nope_norm, mla_q_pe_norm, mla_k_nope_norm, mla_k_pe_norm, diff_q_norm, diff_k_norm,
               diff_lambda_q1, diff_lambda_k1, diff_lambda_q2, diff_lambda_k2, diff_subln, w_out,
               norm_ffn, w_gate, w_up, conv_w, conv_b, w_down, ple_norm, w_ple_gate, w_ple_proj)
    depth = norm_mix.shape[0]
    groups = []
    for x, p in ((x_prompt, p_prompt), (x_sample, p_sample)):
        b, t, d = x.shape
        groups.append(dict(x=x.reshape(b * t, d), p=p.reshape(depth, b * t, p.shape[-1]), batch=b, seq_len=t,
                           rope=_rope_tables(t), shape=x.shape))
    for i in range(depth):
        lw = _layer_params(i, *weights)
        for g in groups:
            g["x"] = _encoder_layer(g["x"], g["p"][i], lw, *g["rope"], layer_idx=i,
                                    batch=g["batch"], seq_len=g["seq_len"])
    return tuple(g["x"].reshape(g["shape"]) for g in groups)
```

```python
import functools
import math

import jax
import jax.numpy as jnp
import numpy as np
from jax import lax
from jax.experimental import pallas as pl
from jax.experimental.pallas import tpu as pltpu

F32 = jnp.float32
BF16 = jnp.bfloat16

EPS = 1e-6
LOG2E = 1.4426950408889634
NEG_BIG = -1e30

D_MODEL = 2048
PLE_DIM = 256
GRID_W = 64
HEAD_DIM = 128
NA_HEADS = 6
NA_WIN_ROWS = 8
NA_WIN_COLS = 16
MLA_HEADS = 5
MLA_Q_RANK = 512
MLA_KV_RANK = 256
MLA_NOPE_DIM = 128
MLA_ROPE_DIM = 64
MLA_V_DIM = 128
MLA_ROPE_THETA = 10000.0
DIFF_HEADS = 5
DIFF_QK_DIM = 64
DIFF_V_DIM = 128
ROPE_THETA = 500000.0
ROPE_PART_DIM = DIFF_QK_DIM // 4
D_FF = 5632

NA_W = NA_HEADS * HEAD_DIM
MLA_QK_DIM = MLA_NOPE_DIM + MLA_ROPE_DIM
MLA_SLOT = 256
DIFF_W = DIFF_HEADS * 2 * DIFF_QK_DIM
DIFF_VW = DIFF_HEADS * DIFF_V_DIM
MLA_VW = MLA_HEADS * MLA_V_DIM
VT_ROWS = 128 + 16

C_NAQ = 0
C_NAK = C_NAQ + NA_W
C_NAV = C_NAK + NA_W
C_CQ = C_NAV + NA_W
C_CKV = C_CQ + MLA_Q_RANK
C_DQ = C_CKV + MLA_KV_RANK
C_DK = C_DQ + DIFF_W
C_DV = C_DK + DIFF_W
C_KPE = C_DV + DIFF_VW
IN_COLS_PAD = C_KPE + 128

LANES = 128
FF_TILE = 512
VMEM_LIMIT = 56 * 1024 * 1024


def _cparams(sem):
    return pltpu.CompilerParams(dimension_semantics=sem, vmem_limit_bytes=VMEM_LIMIT)


def _rms(x, g, n):
    ms = jnp.sum(x * x, axis=-1, keepdims=True) * (1.0 / n)
    return x * lax.rsqrt(ms + EPS) * g


def _proj_prep_kernel(x_ref, mix_g, w_ref, rope_m_ref, rope_d_ref, naq_g, nak_g, qa_g, wqb_ref, kva_g, wkvb_ref,
                      qn_g, qpe_g, kn_g, kpe_g, dq_g, dk_g,
                      naq_o, nak_o, navT_o, mq_o, mk_o, mvT_o, dq_o, dk_o, dvT_o):
    tm = x_ref.shape[0]
    xn = _rms(x_ref[...], mix_g[...], x_ref.shape[-1]).astype(BF16)

    def proj(col, width):
        return jnp.dot(xn, w_ref[:, col:col + width], preferred_element_type=F32)

    lane = lax.broadcasted_iota(jnp.int32, (tm, LANES), 1)
    lo = lane < 64
    ones_row = (lax.broadcasted_iota(jnp.int32, (VT_ROWS - 128, tm), 0) == 0).astype(BF16)

    na_scale = HEAD_DIM ** -0.5 * LOG2E
    na_q, na_k, na_v = proj(C_NAQ, NA_W), proj(C_NAK, NA_W), proj(C_NAV, NA_W)
    for h in range(NA_HEADS):
        sl = slice(h * HEAD_DIM, (h + 1) * HEAD_DIM)
        naq_o[:, sl] = (_rms(na_q[:, sl], naq_g[...], HEAD_DIM) * na_scale).astype(BF16)
        nak_o[:, sl] = _rms(na_k[:, sl], nak_g[...], HEAD_DIM).astype(BF16)
        navT_o[h * VT_ROWS:h * VT_ROWS + HEAD_DIM, :] = na_v[:, sl].T.astype(BF16)
        navT_o[h * VT_ROWS + HEAD_DIM:(h + 1) * VT_ROWS, :] = ones_row

    cm, sam, sbm = rope_m_ref[:, 0:128], rope_m_ref[:, 128:256], rope_m_ref[:, 256:384]

    def rope_m(y):
        return y * cm + pltpu.roll(y, 96, 1) * sam + pltpu.roll(y, 32, 1) * sbm

    mla_scale = MLA_QK_DIM ** -0.5 * LOG2E
    cq = _rms(proj(C_CQ, MLA_Q_RANK), qa_g[...], MLA_Q_RANK).astype(BF16)
    qm = jnp.dot(cq, wqb_ref[...], preferred_element_type=F32)
    for h in range(MLA_HEADS):
        b = h * MLA_SLOT
        nope = _rms(qm[:, b:b + 128], qn_g[...], MLA_NOPE_DIM)
        pe = rope_m(_rms(qm[:, b + 128:b + 256], qpe_g[...], MLA_ROPE_DIM))
        mq_o[:, b:b + 128] = (nope * mla_scale).astype(BF16)
        mq_o[:, b + 128:b + 256] = (pe * mla_scale).astype(BF16)
    ckv = _rms(proj(C_CKV, MLA_KV_RANK), kva_g[...], MLA_KV_RANK).astype(BF16)
    kv = jnp.dot(ckv, wkvb_ref[...], preferred_element_type=F32)
    kpe = rope_m(_rms(proj(C_KPE, 128), kpe_g[...], MLA_ROPE_DIM))
    kpe = kpe.astype(BF16)
    for h in range(MLA_HEADS):
        b = h * MLA_SLOT
        kn = _rms(kv[:, b:b + 128], kn_g[...], MLA_NOPE_DIM)
        mk_o[:, b:b + 128] = kn.astype(BF16)
        mk_o[:, b + 128:b + 256] = kpe
        mvT_o[h * VT_ROWS:h * VT_ROWS + MLA_V_DIM, :] = kv[:, b + 128:b + 256].T.astype(BF16)
        mvT_o[h * VT_ROWS + MLA_V_DIM:(h + 1) * VT_ROWS, :] = ones_row

    cd, sad, sbd = rope_d_ref[:, 0:128], rope_d_ref[:, 128:256], rope_d_ref[:, 256:384]

    def rope_d(y):
        return y * cd + pltpu.roll(y, 120, 1) * sad + pltpu.roll(y, 8, 1) * sbd

    def group_rms(x, g):
        x2 = x * x
        s_lo = jnp.sum(jnp.where(lo, x2, 0.0), axis=-1, keepdims=True)
        s_hi = jnp.sum(jnp.where(lo, 0.0, x2), axis=-1, keepdims=True)
        ms = jnp.where(lo, s_lo, s_hi) * (1.0 / DIFF_QK_DIM)
        return x * lax.rsqrt(ms + EPS) * g

    def split_components(y):
        return jnp.where(lo, y, 0.0), jnp.where(lo, pltpu.roll(y, 64, 1), 0.0)

    df_scale = DIFF_QK_DIM ** -0.5 * LOG2E
    df_q, df_k, df_v = proj(C_DQ, DIFF_W), proj(C_DK, DIFF_W), proj(C_DV, DIFF_VW)
    for h in range(DIFF_HEADS):
        q = rope_d(group_rms(df_q[:, h * 128:(h + 1) * 128], dq_g[...])) * df_scale
        q0, q1 = split_components(q)
        dq_o[:, (2 * h) * 128:(2 * h + 1) * 128] = q0.astype(BF16)
        dq_o[:, (2 * h + 1) * 128:(2 * h + 2) * 128] = q1.astype(BF16)
        k = rope_d(group_rms(df_k[:, h * 128:(h + 1) * 128], dk_g[...]))
        k0, k1 = split_components(k)
        dk_o[:, (2 * h) * 128:(2 * h + 1) * 128] = k0.astype(BF16)
        dk_o[:, (2 * h + 1) * 128:(2 * h + 2) * 128] = k1.astype(BF16)
        v = df_v[:, h * DIFF_V_DIM:(h + 1) * DIFF_V_DIM]
        dvT_o[h * VT_ROWS:h * VT_ROWS + DIFF_V_DIM, :] = v.T.astype(BF16)
        dvT_o[h * VT_ROWS + DIFF_V_DIM:(h + 1) * VT_ROWS, :] = ones_row


def proj_prep(x, rope_m, rope_d, lw, *, seq_len, tm):
    M, D = x.shape
    nt = seq_len // tm
    row = lambda w: pl.BlockSpec((tm, w), lambda i: (i, 0))
    colT = lambda h: pl.BlockSpec((h, tm), lambda i: (0, i))
    full = lambda a: pl.BlockSpec(a.shape, lambda i: (0,) * a.ndim)
    rope = pl.BlockSpec((tm, 384), lambda i: (i % nt, 0))
    w_in = pl.BlockSpec(lw["w_in"].shape, lambda i: (0, 0), pipeline_mode=pl.Buffered(1))
    params = (lw["naq_g"], lw["nak_g"], lw["qa_g"], lw["wqb"], lw["kva_g"], lw["wkvb"],
              lw["qn_g"], lw["qpe_g"], lw["kn_g"], lw["kpe_g"], lw["dq_g"], lw["dk_g"])
    out_shapes = (
        jax.ShapeDtypeStruct((M, NA_W), BF16), jax.ShapeDtypeStruct((M, NA_W), BF16),
        jax.ShapeDtypeStruct((NA_HEADS * VT_ROWS, M), BF16),
        jax.ShapeDtypeStruct((M, MLA_HEADS * MLA_SLOT), BF16),
        jax.ShapeDtypeStruct((M, MLA_HEADS * MLA_SLOT), BF16),
        jax.ShapeDtypeStruct((MLA_HEADS * VT_ROWS, M), BF16),
        jax.ShapeDtypeStruct((M, 2 * DIFF_HEADS * 128), BF16),
        jax.ShapeDtypeStruct((M, 2 * DIFF_HEADS * 128), BF16),
        jax.ShapeDtypeStruct((DIFF_HEADS * VT_ROWS, M), BF16),
    )
    out_specs = (row(NA_W), row(NA_W), colT(NA_HEADS * VT_ROWS), row(MLA_HEADS * MLA_SLOT), row(MLA_HEADS * MLA_SLOT),
                 colT(MLA_HEADS * VT_ROWS), row(2 * DIFF_HEADS * 128), row(2 * DIFF_HEADS * 128),
                 colT(DIFF_HEADS * VT_ROWS))
    return pl.pallas_call(
        _proj_prep_kernel,
        grid=(M // tm,),
        in_specs=[row(D), full(lw["mix_g"]), w_in, rope, rope] + [full(a) for a in params],
        out_specs=out_specs,
        out_shape=out_shapes,
        compiler_params=_cparams(("parallel",)),
        name="proj_prep",
    )(x, lw["mix_g"], lw["w_in"], rope_m, rope_d, *params)


def _flash_kernel(q_ref, k_ref, vT_ref, o_ref, sa_ref, sb_ref, acc_ref, *, tq, tk):
    seq_len = q_ref.shape[0]
    dv = o_ref.shape[1]
    nk = seq_len // tk
    total = (seq_len // tq) * nk
    sub = 16

    def scores(f):
        qoff = pl.multiple_of((f // nk) * tq, tq)
        koff = pl.multiple_of((f % nk) * tk, tk)
        s = lax.dot_general(k_ref[pl.ds(koff, tk), :], q_ref[pl.ds(qoff, tq), :],
                            (((1,), (1,)), ((), ())), preferred_element_type=F32)
        return s.astype(BF16)

    def softmax_pv(s_ref, c, m):
        s = s_ref[...]
        part = jnp.max(s.reshape(tk // sub, sub, tq), axis=0).astype(F32)
        m_new = jnp.maximum(m, jnp.max(part, axis=0, keepdims=True))
        alpha = jnp.exp2(m - m_new)
        p = jnp.exp2(s - m_new.astype(BF16))
        koff = pl.multiple_of(c * tk, tk)
        pv = jnp.dot(vT_ref[:, pl.ds(koff, tk)], p, preferred_element_type=F32)
        acc_ref[...] = alpha * acc_ref[...] + pv
        return m_new

    def pair(i, m):
        f = 2 * i
        c = f % nk
        m = jnp.where(c == 0, NEG_BIG, m)
        sb_ref[...] = scores(f + 1)
        m = softmax_pv(sa_ref, c, m)
        sa_ref[...] = scores(jnp.minimum(f + 2, total - 1))
        m = softmax_pv(sb_ref, c + 1, m)

        @pl.when(c + 2 == nk)
        def _():
            qoff = pl.multiple_of((f // nk) * tq, tq)
            out = acc_ref[0:dv, :] / acc_ref[dv:dv + 1, :]
            o_ref[pl.ds(qoff, tq), :] = out.T.astype(o_ref.dtype)

        return m

    acc_ref[...] = jnp.zeros_like(acc_ref)
    sa_ref[...] = scores(0)
    lax.fori_loop(0, total // 2, pair, jnp.full((1, tq), NEG_BIG, F32))


def flash_attention(q, k, vT, *, batch, seq_len, n_heads, dq, v_of_head, tq, tk, out_dtype):
    M = q.shape[0]
    dv = 128
    assert seq_len % (2 * tk) == 0 and seq_len % tq == 0
    once = pl.Buffered(1 if seq_len * dq * 2 > 2 * 1024 * 1024 else 2)
    return pl.pallas_call(
        functools.partial(_flash_kernel, tq=tq, tk=tk),
        grid=(batch, n_heads),
        in_specs=[pl.BlockSpec((seq_len, dq), lambda b, h: (b, h), pipeline_mode=once),
                  pl.BlockSpec((seq_len, dq), lambda b, h: (b, h), pipeline_mode=once),
                  pl.BlockSpec((VT_ROWS, seq_len), lambda b, h: (v_of_head(h), b), pipeline_mode=once)],
        out_specs=pl.BlockSpec((seq_len, dv), lambda b, h: (b, h)),
        out_shape=jax.ShapeDtypeStruct((M, n_heads * dv), out_dtype),
        scratch_shapes=[pltpu.VMEM((tk, tq), BF16), pltpu.VMEM((tk, tq), BF16), pltpu.VMEM((VT_ROWS, tq), F32)],
        compiler_params=_cparams(("parallel", "parallel")),
        name="flash_attention",
    )(q, k, vT)


NA_QROWS = 8
NA_KROWS = 2 * NA_WIN_ROWS


def _na_kernel(q_ref, k_ref, vT_ref, bias_ref, o_ref, *, rows):
    i = pl.program_id(2)
    ws = jnp.clip(i * NA_QROWS - NA_WIN_ROWS // 2, 0, rows - NA_KROWS)
    ks = pl.multiple_of(ws * GRID_W, (NA_WIN_ROWS // 2) * GRID_W)
    win = NA_KROWS * GRID_W
    dv = o_ref.shape[1]
    sub = 16
    s = lax.dot_general(k_ref[pl.ds(ks, win), :], q_ref[...], (((1,), (1,)), ((), ())),
                        preferred_element_type=F32)
    s = (s + bias_ref[...]).astype(BF16)
    part = jnp.max(s.reshape(win // sub, sub, s.shape[1]), axis=0).astype(F32)
    m = jnp.max(part, axis=0, keepdims=True)
    p = jnp.exp2(s - m.astype(BF16))
    acc = jnp.dot(vT_ref[:, pl.ds(ks, win)], p, preferred_element_type=F32)
    o_ref[...] = (acc[0:dv, :] / acc[dv:dv + 1, :]).T.astype(o_ref.dtype)


def na_attention(q, k, vT, bias, *, batch, seq_len):
    M = q.shape[0]
    rows = seq_len // GRID_W
    assert rows >= NA_KROWS and rows % NA_QROWS == 0
    nblk = rows // NA_QROWS
    tq = NA_QROWS * GRID_W
    pattern = lambda i: jnp.where(i == 0, 0, jnp.where(i == nblk - 1, 2, 1))
    return pl.pallas_call(
        functools.partial(_na_kernel, rows=rows),
        grid=(batch, NA_HEADS, nblk),
        in_specs=[pl.BlockSpec((tq, HEAD_DIM), lambda b, h, i: (b * nblk + i, h)),
                  pl.BlockSpec((seq_len, HEAD_DIM), lambda b, h, i: (b, h)),
                  pl.BlockSpec((VT_ROWS, seq_len), lambda b, h, i: (h, b)),
                  pl.BlockSpec((None, None, NA_KROWS * GRID_W, tq), lambda b, h, i: (h, pattern(i), 0, 0))],
        out_specs=pl.BlockSpec((tq, HEAD_DIM), lambda b, h, i: (b * nblk + i, h)),
        out_shape=jax.ShapeDtypeStruct((M, NA_W), BF16),
        compiler_params=_cparams(("parallel", "parallel", "arbitrary")),
        name="na_attention",
    )(q, k, vT, bias)


def _out_proj_kernel(x_ref, ona_ref, omla_ref, odf_ref, subln_ref, lq1, lk1, lq2, lk2,
                     w_na, w_mla, w_df, o_ref, *, lam_init):
    lam = (jnp.exp(jnp.sum(lq1[...] * lk1[...], axis=-1, keepdims=True))
           - jnp.exp(jnp.sum(lq2[...] * lk2[...], axis=-1, keepdims=True)) + lam_init)
    heads = []
    for h in range(DIFF_HEADS):
        o1 = odf_ref[:, (2 * h) * 128:(2 * h + 1) * 128]
        o2 = odf_ref[:, (2 * h + 1) * 128:(2 * h + 2) * 128]
        o = _rms(o1 - lam * o2, subln_ref[...], DIFF_V_DIM) * (1.0 - lam_init)
        heads.append(o.astype(BF16))
    odf = jnp.concatenate(heads, axis=-1)
    y = jnp.dot(ona_ref[...], w_na[...], preferred_element_type=F32)
    y = y + jnp.dot(omla_ref[...], w_mla[...], preferred_element_type=F32)
    y = y + jnp.dot(odf, w_df[...], preferred_element_type=F32)
    o_ref[...] = x_ref[...] + y


def out_proj(x, o_na, o_mla, o_df, lw, *, lam_init, tm):
    M, D = x.shape
    row = lambda w: pl.BlockSpec((tm, w), lambda i: (i, 0))
    full = lambda a: pl.BlockSpec(a.shape, lambda i: (0,) * a.ndim)
    params = (lw["subln_g"], lw["lq1"], lw["lk1"], lw["lq2"], lw["lk2"], lw["w_out_na"], lw["w_out_mla"], lw["w_out_df"])
    return pl.pallas_call(
        functools.partial(_out_proj_kernel, lam_init=lam_init),
        grid=(M // tm,),
        in_specs=[row(D), row(NA_W), row(MLA_VW), row(2 * DIFF_HEADS * 128)] + [full(a) for a in params],
        out_specs=row(D),
        out_shape=jax.ShapeDtypeStruct((M, D), F32),
        compiler_params=_cparams(("parallel",)),
        name="out_proj",
    )(x, o_na, o_mla, o_df, *params)


HALO = 16


def _ffn_kernel(x_ref, xp_ref, xn_ref, g_ref, wg_ref, wu_ref, cw_ref, cb_ref, wd_ref,
                p_ref, pg_ref, wpg_ref, wpp_ref, o_ref, xs_ref, acc_ref, *, seq_len):
    i = pl.program_id(0)
    f = pl.program_id(1)
    tm = x_ref.shape[0]

    @pl.when(f == 0)
    def _():
        d = x_ref.shape[-1]
        row0 = i * tm
        at_start = (row0 % seq_len) == 0
        at_end = ((row0 + tm) % seq_len) == 0
        prev = _rms(xp_ref[...], g_ref[...], d)
        nxt = _rms(xn_ref[...], g_ref[...], d)
        xs_ref[0:HALO, :] = jnp.where(at_start, 0.0, prev).astype(BF16)
        xs_ref[HALO:HALO + tm, :] = _rms(x_ref[...], g_ref[...], d).astype(BF16)
        xs_ref[HALO + tm:2 * HALO + tm, :] = jnp.where(at_end, 0.0, nxt).astype(BF16)
        acc_ref[...] = jnp.zeros_like(acc_ref)

    n_ext = tm + 2 * HALO
    gate = jnp.dot(xs_ref[...], wg_ref[...], preferred_element_type=F32)
    g_prev = pltpu.roll(gate, 1, 0)[HALO:HALO + tm, :]
    g_next = pltpu.roll(gate, n_ext - 1, 0)[HALO:HALO + tm, :]
    g_mid = gate[HALO:HALO + tm, :]
    g = g_prev * cw_ref[0:1, :] + g_mid * cw_ref[1:2, :] + g_next * cw_ref[2:3, :] + cb_ref[...]
    u = jnp.dot(xs_ref[HALO:HALO + tm, :], wu_ref[...], preferred_element_type=F32)
    a = (g * jax.nn.sigmoid(g) * u).astype(BF16)
    acc_ref[...] += jnp.dot(a, wd_ref[...], preferred_element_type=F32)

    @pl.when(f == pl.num_programs(1) - 1)
    def _():
        y = x_ref[...] + acc_ref[...]
        yn = _rms(y, pg_ref[...], y.shape[-1]).astype(BF16)
        gate = jax.nn.sigmoid(jnp.dot(yn, wpg_ref[...], preferred_element_type=F32))
        emb = jnp.dot(p_ref[...].astype(BF16), wpp_ref[...], preferred_element_type=F32)
        o_ref[...] = y + gate * emb


def ffn_ple(x, p, lw, *, seq_len, tm):
    M, D = x.shape
    tf = FF_TILE
    nf = lw["w_gate"].shape[1] // tf
    assert seq_len % tm == 0 and tm % HALO == 0
    hb = tm // HALO
    last = M // HALO - 1
    return pl.pallas_call(
        functools.partial(_ffn_kernel, seq_len=seq_len),
        grid=(M // tm, nf),
        in_specs=[pl.BlockSpec((tm, D), lambda i, f: (i, 0)),
                  pl.BlockSpec((HALO, D), lambda i, f: (jnp.maximum(i * hb - 1, 0), 0)),
                  pl.BlockSpec((HALO, D), lambda i, f: (jnp.minimum((i + 1) * hb, last), 0)),
                  pl.BlockSpec((1, D), lambda i, f: (0, 0)),
                  pl.BlockSpec((D, tf), lambda i, f: (0, f)),
                  pl.BlockSpec((D, tf), lambda i, f: (0, f)),
                  pl.BlockSpec((3, tf), lambda i, f: (0, f)),
                  pl.BlockSpec((1, tf), lambda i, f: (0, f)),
                  pl.BlockSpec((tf, D), lambda i, f: (f, 0)),
                  pl.BlockSpec((tm, p.shape[1]), lambda i, f: (i, 0)),
                  pl.BlockSpec((1, D), lambda i, f: (0, 0)),
                  pl.BlockSpec((D, D), lambda i, f: (0, 0), pipeline_mode=pl.Buffered(1)),
                  pl.BlockSpec((p.shape[1], D), lambda i, f: (0, 0), pipeline_mode=pl.Buffered(1))],
        out_specs=pl.BlockSpec((tm, D), lambda i, f: (i, 0)),
        out_shape=jax.ShapeDtypeStruct((M, D), F32),
        scratch_shapes=[pltpu.VMEM((tm + 2 * HALO, D), BF16), pltpu.VMEM((tm, D), F32)],
        compiler_params=_cparams(("parallel", "arbitrary")),
        name="ffn_ple",
    )(x, x, x, lw["ffn_g"], lw["w_gate"], lw["w_up"], lw["conv_w"], lw["conv_b"], lw["w_down"],
      p, lw["ple_g"], lw["w_ple_gate"], lw["w_ple_proj"])


def _rope_tables(seq_len):
    def angles(dim, theta):
        inv = 1.0 / (theta ** (jnp.arange(0, dim, 2, dtype=F32) / dim))
        ang = jnp.arange(seq_len, dtype=F32)[:, None] * inv[None, :]
        return jnp.cos(ang), jnp.sin(ang)

    z = lambda w: jnp.zeros((seq_len, w), F32)
    cos, sin = angles(MLA_ROPE_DIM, MLA_ROPE_THETA)
    half = MLA_ROPE_DIM // 2
    cm = jnp.concatenate([cos, cos, z(64)], axis=1)
    sam = jnp.concatenate([-sin, z(half), z(64)], axis=1)
    sbm = jnp.concatenate([z(half), sin, z(64)], axis=1)
    rope_m = jnp.concatenate([cm, sam, sbm], axis=1)

    cos, sin = angles(ROPE_PART_DIM, ROPE_THETA)
    half = ROPE_PART_DIM // 2
    rest = DIFF_QK_DIM - ROPE_PART_DIM
    c64 = jnp.concatenate([cos, cos, jnp.ones((seq_len, rest), F32)], axis=1)
    sa64 = jnp.concatenate([-sin, z(half), z(rest)], axis=1)
    sb64 = jnp.concatenate([z(half), sin, z(rest)], axis=1)
    rope_d = jnp.concatenate([c64, c64, sa64, sa64, sb64, sb64], axis=1)
    return rope_m, rope_d


def _na_bias_table(rpb):
    cols = jnp.arange(GRID_W)
    cs = jnp.clip(cols - NA_WIN_COLS // 2, 0, GRID_W - NA_WIN_COLS)
    kc = jnp.arange(GRID_W)
    valid = (kc[None, :] >= cs[:, None]) & (kc[None, :] < cs[:, None] + NA_WIN_COLS)
    nv = 2 * NA_WIN_COLS - 1
    onehot = (kc[None, None, :] - cols[None, :, None] + (NA_WIN_COLS - 1) == jnp.arange(nv)[:, None, None])
    toep = jnp.einsum("huv,vck->huck", rpb.astype(F32), onehot.astype(F32), precision=lax.Precision.HIGHEST)
    toep = jnp.where(valid[None, None], toep * LOG2E, NEG_BIG)
    n_h = rpb.shape[0]
    masked = 2 * NA_WIN_ROWS - 1
    toep = jnp.concatenate([toep, jnp.full((n_h, 1, GRID_W, GRID_W), NEG_BIG, F32)], axis=1)
    half = NA_WIN_ROWS // 2
    tile = np.full((3, NA_QROWS, NA_KROWS), masked, np.int32)
    for pat, delta in enumerate((0, -half, -NA_WIN_ROWS)):
        for a in range(NA_QROWS):
            first = (max(a - half, 0), a, min(a + half, NA_WIN_ROWS))[pat]
            for i in range(first, first + NA_WIN_ROWS):
                tile[pat, a, i] = delta + i - a + (NA_WIN_ROWS - 1)
    tab = jnp.take(toep, jnp.asarray(tile.reshape(-1)), axis=1)
    tab = tab.reshape(n_h, 3, NA_QROWS, NA_KROWS, GRID_W, GRID_W).transpose(0, 1, 3, 5, 2, 4)
    return tab.reshape(n_h, 3, NA_KROWS * GRID_W, NA_QROWS * GRID_W)


def _layer_params(i, norm_mix, w_in, na_q_norm, na_k_norm, na_rpb, mla_q_a_norm, mla_w_q_b, mla_kv_a_norm,
                  mla_w_kv_b, mla_q_nope_norm, mla_q_pe_norm, mla_k_nope_norm, mla_k_pe_norm,
                  diff_q_norm, diff_k_norm, diff_lambda_q1, diff_lambda_k1, diff_lambda_q2, diff_lambda_k2,
                  diff_subln, w_out, norm_ffn, w_gate, w_up, conv_w, conv_b, w_down,
                  ple_norm, w_ple_gate, w_ple_proj):
    r = lambda a: a[i].reshape(1, -1).astype(F32)
    pad_to = lambda a, n: jnp.pad(a, ((0, 0), (0, n - a.shape[1])))
    w = w_in[i]
    kpe0 = 3 * NA_W + MLA_Q_RANK + MLA_KV_RANK
    w_perm = jnp.concatenate([w[:, :kpe0], w[:, kpe0 + MLA_ROPE_DIM:], w[:, kpe0:kpe0 + MLA_ROPE_DIM],
                              jnp.zeros((w.shape[0], 128 - MLA_ROPE_DIM), w.dtype)], axis=1).astype(BF16)
    wqb = mla_w_q_b[i].reshape(MLA_Q_RANK, MLA_HEADS, MLA_QK_DIM)
    wqb = jnp.pad(wqb, ((0, 0), (0, 0), (0, MLA_SLOT - MLA_QK_DIM))).reshape(MLA_Q_RANK, MLA_HEADS * MLA_SLOT)
    wo = w_out[i].astype(BF16)
    return dict(
        mix_g=r(norm_mix), w_in=w_perm,
        naq_g=r(na_q_norm), nak_g=r(na_k_norm), na_bias=_na_bias_table(na_rpb[i]),
        qa_g=r(mla_q_a_norm), wqb=wqb.astype(BF16), kva_g=r(mla_kv_a_norm), wkvb=mla_w_kv_b[i].astype(BF16),
        qn_g=r(mla_q_nope_norm), qpe_g=pad_to(r(mla_q_pe_norm), 128),
        kn_g=r(mla_k_nope_norm), kpe_g=pad_to(r(mla_k_pe_norm), 128),
        dq_g=jnp.tile(r(diff_q_norm), (1, 2)), dk_g=jnp.tile(r(diff_k_norm), (1, 2)),
        lq1=r(diff_lambda_q1), lk1=r(diff_lambda_k1), lq2=r(diff_lambda_q2), lk2=r(diff_lambda_k2),
        subln_g=r(diff_subln),
        w_out_na=wo[:NA_W], w_out_mla=wo[NA_W:NA_W + MLA_VW], w_out_df=wo[NA_W + MLA_VW:],
        ffn_g=r(norm_ffn), w_gate=w_gate[i].astype(BF16), w_up=w_up[i].astype(BF16),
        conv_w=conv_w[i].astype(F32), conv_b=r(conv_b), w_down=w_down[i].astype(BF16),
        ple_g=r(ple_norm), w_ple_gate=w_ple_gate[i].astype(BF16), w_ple_proj=w_ple_proj[i].astype(BF16),
    )


def _tile(n, pref):
    t = min(pref, n)
    while n % t:
        t //= 2
    return t


def _encoder_layer(x, p_l, lw, rope_m, rope_d, *, layer_idx, batch, seq_len):
    tm = _tile(seq_len, 512)
    (na_q, na_k, na_vT, m_q, m_k, m_vT, d_q, d_k, d_vT) = proj_prep(
        x, rope_m, rope_d, lw, seq_len=seq_len, tm=_tile(seq_len, 256))
    o_na = na_attention(na_q, na_k, na_vT, lw["na_bias"], batch=batch, seq_len=seq_len)
    tq = _tile(seq_len, 512)
    tk = _tile(seq_len // 2, 8192)
    o_mla = flash_attention(m_q, m_k, m_vT, batch=batch, seq_len=seq_len, n_heads=MLA_HEADS, dq=MLA_SLOT,
                            v_of_head=lambda h: h, tq=tq, tk=tk, out_dtype=BF16)
    o_df = flash_attention(d_q, d_k, d_vT, batch=batch, seq_len=seq_len, n_heads=2 * DIFF_HEADS, dq=128,
                           v_of_head=lambda h: h // 2, tq=tq, tk=tk, out_dtype=F32)
    lam_init = 0.8 - 0.6 * math.exp(-0.3 * layer_idx)
    x = out_proj(x, o_na, o_mla, o_df, lw, lam_init=lam_init, tm=tm)
    return ffn_ple(x, p_l, lw, seq_len=seq_len, tm=tm)


def kernel(x_prompt, x_sample, p_prompt, p_sample, norm_mix, w_in, na_q_norm, na_k_norm, na_rpb, mla_q_a_norm, mla_w_q_b, mla_kv_a_norm, mla_w_kv_b, mla_q_nope_norm, mla_q_pe_norm, mla_k_nope_norm, mla_k_pe_norm, diff_q_norm, diff_k_norm, diff_lambda_q1, diff_lambda_k1, diff_lambda_q2, diff_lambda_k2, diff_subln, w_out, norm_ffn, w_gate, w_up, conv_w, conv_b, w_down, ple_norm, w_ple_gate, w_ple_proj):
    weights = (norm_mix, w_in, na_q_norm, na_k_norm, na_rpb, mla_q_a_norm, mla_w_q_b, mla_kv_a_norm, mla_w_kv_b,
               mla_q_nope_norm, mla_q_pe_norm, mla_k_nope_norm, mla_k_pe_norm, diff_q_norm, diff_k_norm,
               diff_lambda_q1, diff_lambda_k1, diff_lambda_q2, diff_lambda_k2, diff_subln, w_out,
               norm_ffn, w_gate, w_up, conv_w, conv_b, w_down, ple_norm, w_ple_gate, w_ple_proj)
    depth = norm_mix.shape[0]
    groups = []
    for x, p in ((x_prompt, p_prompt), (x_sample, p_sample)):
        b, t, d = x.shape
        groups.append(dict(x=x.reshape(b * t, d), p=p.reshape(depth, b * t, p.shape[-1]), batch=b, seq_len=t,
                           rope=_rope_tables(t), shape=x.shape))
    for i in range(depth):
        lw = _layer_params(i, *weights)
        for g in groups:
            g["x"] = _encoder_layer(g["x"], g["p"][i], lw, *g["rope"], layer_idx=i,
                                    batch=g["batch"], seq_len=g["seq_len"])
    return tuple(g["x"].reshape(g["shape"]) for g in groups)
```

```python
import functools
import math

import jax
import jax.numpy as jnp
import numpy as np
from jax import lax
from jax.experimental import pallas as pl
from jax.experimental.pallas import tpu as pltpu

F32 = jnp.float32
BF16 = jnp.bfloat16

EPS = 1e-6
LOG2E = 1.4426950408889634
NEG_BIG = -1e30

D_MODEL = 2048
PLE_DIM = 256
GRID_W = 64
HEAD_DIM = 128
NA_HEADS = 6
NA_WIN_ROWS = 8
NA_WIN_COLS = 16
MLA_HEADS = 5
MLA_Q_RANK = 512
MLA_KV_RANK = 256
MLA_NOPE_DIM = 128
MLA_ROPE_DIM = 64
MLA_V_DIM = 128
MLA_ROPE_THETA = 10000.0
DIFF_HEADS = 5
DIFF_QK_DIM = 64
DIFF_V_DIM = 128
ROPE_THETA = 500000.0
ROPE_PART_DIM = DIFF_QK_DIM // 4
D_FF = 5632

NA_W = NA_HEADS * HEAD_DIM
MLA_QK_DIM = MLA_NOPE_DIM + MLA_ROPE_DIM
MLA_SLOT = 256
DIFF_W = DIFF_HEADS * 2 * DIFF_QK_DIM
DIFF_VW = DIFF_HEADS * DIFF_V_DIM
MLA_VW = MLA_HEADS * MLA_V_DIM
VT_ROWS = 128 + 16

C_NAQ = 0
C_NAK = C_NAQ + NA_W
C_NAV = C_NAK + NA_W
C_CQ = C_NAV + NA_W
C_CKV = C_CQ + MLA_Q_RANK
C_DQ = C_CKV + MLA_KV_RANK
C_DK = C_DQ + DIFF_W
C_DV = C_DK + DIFF_W
C_KPE = C_DV + DIFF_VW
IN_COLS_PAD = C_KPE + 128

LANES = 128
FF_TILE = 512
VMEM_LIMIT = 56 * 1024 * 1024


def _cparams(sem):
    return pltpu.CompilerParams(dimension_semantics=sem, vmem_limit_bytes=VMEM_LIMIT)


def _rms(x, g, n):
    ms = jnp.sum(x * x, axis=-1, keepdims=True) * (1.0 / n)
    return x * lax.rsqrt(ms + EPS) * g


def _proj_prep_kernel(x_ref, mix_g, w_ref, rope_m_ref, rope_d_ref, naq_g, nak_g, qa_g, wqb_ref, kva_g, wkvb_ref,
                      qn_g, qpe_g, kn_g, kpe_g, dq_g, dk_g,
                      naq_o, nak_o, navT_o, mq_o, mk_o, mvT_o, dq_o, dk_o, dvT_o):
    tm = x_ref.shape[0]
    xn = _rms(x_ref[...], mix_g[...], x_ref.shape[-1]).astype(BF16)

    def proj(col, width):
        return jnp.dot(xn, w_ref[:, col:col + width], preferred_element_type=F32)

    lane = lax.broadcasted_iota(jnp.int32, (tm, LANES), 1)
    lo = lane < 64
    ones_row = (lax.broadcasted_iota(jnp.int32, (VT_ROWS - 128, tm), 0) == 0).astype(BF16)

    na_scale = HEAD_DIM ** -0.5 * LOG2E
    na_q, na_k, na_v = proj(C_NAQ, NA_W), proj(C_NAK, NA_W), proj(C_NAV, NA_W)
    for h in range(NA_HEADS):
        sl = slice(h * HEAD_DIM, (h + 1) * HEAD_DIM)
        naq_o[:, sl] = (_rms(na_q[:, sl], naq_g[...], HEAD_DIM) * na_scale).astype(BF16)
        nak_o[:, sl] = _rms(na_k[:, sl], nak_g[...], HEAD_DIM).astype(BF16)
        navT_o[h * VT_ROWS:h * VT_ROWS + HEAD_DIM, :] = na_v[:, sl].T.astype(BF16)
        navT_o[h * VT_ROWS + HEAD_DIM:(h + 1) * VT_ROWS, :] = ones_row

    cm, sam, sbm = rope_m_ref[:, 0:128], rope_m_ref[:, 128:256], rope_m_ref[:, 256:384]

    def rope_m(y):
        return y * cm + pltpu.roll(y, 96, 1) * sam + pltpu.roll(y, 32, 1) * sbm

    mla_scale = MLA_QK_DIM ** -0.5 * LOG2E
    cq = _rms(proj(C_CQ, MLA_Q_RANK), qa_g[...], MLA_Q_RANK).astype(BF16)
    qm = jnp.dot(cq, wqb_ref[...], preferred_element_type=F32)
    for h in range(MLA_HEADS):
        b = h * MLA_SLOT
        nope = _rms(qm[:, b:b + 128], qn_g[...], MLA_NOPE_DIM)
        pe = rope_m(_rms(qm[:, b + 128:b + 256], qpe_g[...], MLA_ROPE_DIM))
        mq_o[:, b:b + 128] = (nope * mla_scale).astype(BF16)
        mq_o[:, b + 128:b + 256] = (pe * mla_scale).astype(BF16)
    ckv = _rms(proj(C_CKV, MLA_KV_RANK), kva_g[...], MLA_KV_RANK).astype(BF16)
    kv = jnp.dot(ckv, wkvb_ref[...], preferred_element_type=F32)
    kpe = rope_m(_rms(proj(C_KPE, 128), kpe_g[...], MLA_ROPE_DIM))
    kpe = kpe.astype(BF16)
    for h in range(MLA_HEADS):
        b = h * MLA_SLOT
        kn = _rms(kv[:, b:b + 128], kn_g[...], MLA_NOPE_DIM)
        mk_o[:, b:b + 128] = kn.astype(BF16)
        mk_o[:, b + 128:b + 256] = kpe
        mvT_o[h * VT_ROWS:h * VT_ROWS + MLA_V_DIM, :] = kv[:, b + 128:b + 256].T.astype(BF16)
        mvT_o[h * VT_ROWS + MLA_V_DIM:(h + 1) * VT_ROWS, :] = ones_row

    cd, sad, sbd = rope_d_ref[:, 0:128], rope_d_ref[:, 128:256], rope_d_ref[:, 256:384]

    def rope_d(y):
        return y * cd + pltpu.roll(y, 120, 1) * sad + pltpu.roll(y, 8, 1) * sbd

    def group_rms(x, g):
        x2 = x * x
        s_lo = jnp.sum(jnp.where(lo, x2, 0.0), axis=-1, keepdims=True)
        s_hi = jnp.sum(jnp.where(lo, 0.0, x2), axis=-1, keepdims=True)
        ms = jnp.where(lo, s_lo, s_hi) * (1.0 / DIFF_QK_DIM)
        return x * lax.rsqrt(ms + EPS) * g

    def split_components(y):
        return jnp.where(lo, y, 0.0), jnp.where(lo, pltpu.roll(y, 64, 1), 0.0)

    df_scale = DIFF_QK_DIM ** -0.5 * LOG2E
    df_q, df_k, df_v = proj(C_DQ, DIFF_W), proj(C_DK, DIFF_W), proj(C_DV, DIFF_VW)
    for h in range(DIFF_HEADS):
        q = rope_d(group_rms(df_q[:, h * 128:(h + 1) * 128], dq_g[...])) * df_scale
        q0, q1 = split_components(q)
        dq_o[:, (2 * h) * 128:(2 * h + 1) * 128] = q0.astype(BF16)
        dq_o[:, (2 * h + 1) * 128:(2 * h + 2) * 128] = q1.astype(BF16)
        k = rope_d(group_rms(df_k[:, h * 128:(h + 1) * 128], dk_g[...]))
        k0, k1 = split_components(k)
        dk_o[:, (2 * h) * 128:(2 * h + 1) * 128] = k0.astype(BF16)
        dk_o[:, (2 * h + 1) * 128:(2 * h + 2) * 128] = k1.astype(BF16)
        v = df_v[:, h * DIFF_V_DIM:(h + 1) * DIFF_V_DIM]
        dvT_o[h * VT_ROWS:h * VT_ROWS + DIFF_V_DIM, :] = v.T.astype(BF16)
        dvT_o[h * VT_ROWS + DIFF_V_DIM:(h + 1) * VT_ROWS, :] = ones_row


def proj_prep(x, rope_m, rope_d, lw, *, seq_len, tm):
    M, D = x.shape
    nt = seq_len // tm
    row = lambda w: pl.BlockSpec((tm, w), lambda i: (i, 0))
    colT = lambda h: pl.BlockSpec((h, tm), lambda i: (0, i))
    full = lambda a: pl.BlockSpec(a.shape, lambda i: (0,) * a.ndim)
    rope = pl.BlockSpec((tm, 384), lambda i: (i % nt, 0))
    w_in = pl.BlockSpec(lw["w_in"].shape, lambda i: (0, 0), pipeline_mode=pl.Buffered(1))
    params = (lw["naq_g"], lw["nak_g"], lw["qa_g"], lw["wqb"], lw["kva_g"], lw["wkvb"],
              lw["qn_g"], lw["qpe_g"], lw["kn_g"], lw["kpe_g"], lw["dq_g"], lw["dk_g"])
    out_shapes = (
        jax.ShapeDtypeStruct((M, NA_W), BF16), jax.ShapeDtypeStruct((M, NA_W), BF16),
        jax.ShapeDtypeStruct((NA_HEADS * VT_ROWS, M), BF16),
        jax.ShapeDtypeStruct((M, MLA_HEADS * MLA_SLOT), BF16),
        jax.ShapeDtypeStruct((M, MLA_HEADS * MLA_SLOT), BF16),
        jax.ShapeDtypeStruct((MLA_HEADS * VT_ROWS, M), BF16),
        jax.ShapeDtypeStruct((M, 2 * DIFF_HEADS * 128), BF16),
        jax.ShapeDtypeStruct((M, 2 * DIFF_HEADS * 128), BF16),
        jax.ShapeDtypeStruct((DIFF_HEADS * VT_ROWS, M), BF16),
    )
    out_specs = (row(NA_W), row(NA_W), colT(NA_HEADS * VT_ROWS), row(MLA_HEADS * MLA_SLOT), row(MLA_HEADS * MLA_SLOT),
                 colT(MLA_HEADS * VT_ROWS), row(2 * DIFF_HEADS * 128), row(2 * DIFF_HEADS * 128),
                 colT(DIFF_HEADS * VT_ROWS))
    return pl.pallas_call(
        _proj_prep_kernel,
        grid=(M // tm,),
        in_specs=[row(D), full(lw["mix_g"]), w_in, rope, rope] + [full(a) for a in params],
        out_specs=out_specs,
        out_shape=out_shapes,
        compiler_params=_cparams(("parallel",)),
        name="proj_prep",
    )(x, lw["mix_g"], lw["w_in"], rope_m, rope_d, *params)


def _flash_kernel(q_ref, k_ref, vT_ref, o_ref, sa_ref, sb_ref, acc_ref, *, tq, tk):
    seq_len = q_ref.shape[0]
    dv = o_ref.shape[1]
    nk = seq_len // tk
    total = (seq_len // tq) * nk
    sub = 16

    def scores(f):
        qoff = pl.multiple_of((f // nk) * tq, tq)
        koff = pl.multiple_of((f % nk) * tk, tk)
        s = lax.dot_general(k_ref[pl.ds(koff, tk), :], q_ref[pl.ds(qoff, tq), :],
                            (((1,), (1,)), ((), ())), preferred_element_type=F32)
        return s.astype(BF16)

    def softmax_pv(s_ref, c, m):
        s = s_ref[...]
        part = jnp.max(s.reshape(tk // sub, sub, tq), axis=0).astype(F32)
        m_new = jnp.maximum(m, jnp.max(part, axis=0, keepdims=True))
        alpha = jnp.exp2(m - m_new)
        p = jnp.exp2(s - m_new.astype(BF16))
        koff = pl.multiple_of(c * tk, tk)
        pv = jnp.dot(vT_ref[:, pl.ds(koff, tk)], p, preferred_element_type=F32)
        acc_ref[...] = alpha * acc_ref[...] + pv
        return m_new

    def write_block(qb):
        qoff = pl.multiple_of(qb * tq, tq)
        out = acc_ref[0:dv, :] / acc_ref[dv:dv + 1, :]
        o_ref[pl.ds(qoff, tq), :] = out.T.astype(o_ref.dtype)

    def pair(i, m):
        f = 2 * i
        if nk == 1:
            sb_ref[...] = scores(f + 1)
            softmax_pv(sa_ref, 0, m)
            write_block(f)
            sa_ref[...] = scores(jnp.minimum(f + 2, total - 1))
            softmax_pv(sb_ref, 0, m)
            write_block(f + 1)
            return m
        c = f % nk
        m = jnp.where(c == 0, NEG_BIG, m)
        sb_ref[...] = scores(f + 1)
        m = softmax_pv(sa_ref, c, m)
        sa_ref[...] = scores(jnp.minimum(f + 2, total - 1))
        m = softmax_pv(sb_ref, c + 1, m)

        @pl.when(c + 2 == nk)
        def _():
            write_block(f // nk)

        return m

    acc_ref[...] = jnp.zeros_like(acc_ref)
    sa_ref[...] = scores(0)
    lax.fori_loop(0, total // 2, pair, jnp.full((1, tq), NEG_BIG, F32))


def flash_attention(q, k, vT, *, batch, seq_len, n_heads, dq, v_of_head, tq, tk, out_dtype):
    M = q.shape[0]
    dv = 128
    nk = seq_len // tk
    assert seq_len % tk == 0 and seq_len % tq == 0 and (nk % 2 == 0 or (nk == 1 and (seq_len // tq) % 2 == 0))
    once = pl.Buffered(1 if seq_len * dq * 2 > 2 * 1024 * 1024 else 2)
    return pl.pallas_call(
        functools.partial(_flash_kernel, tq=tq, tk=tk),
        grid=(batch, n_heads),
        in_specs=[pl.BlockSpec((seq_len, dq), lambda b, h: (b, h), pipeline_mode=once),
                  pl.BlockSpec((seq_len, dq), lambda b, h: (b, h), pipeline_mode=once),
                  pl.BlockSpec((VT_ROWS, seq_len), lambda b, h: (v_of_head(h), b), pipeline_mode=once)],
        out_specs=pl.BlockSpec((seq_len, dv), lambda b, h: (b, h)),
        out_shape=jax.ShapeDtypeStruct((M, n_heads * dv), out_dtype),
        scratch_shapes=[pltpu.VMEM((tk, tq), BF16), pltpu.VMEM((tk, tq), BF16), pltpu.VMEM((VT_ROWS, tq), F32)],
        compiler_params=_cparams(("parallel", "parallel")),
        name="flash_attention",
    )(q, k, vT)


NA_QROWS = 8
NA_KROWS = 2 * NA_WIN_ROWS


def _na_kernel(q_ref, k_ref, vT_ref, bias_ref, o_ref, *, rows):
    i = pl.program_id(2)
    ws = jnp.clip(i * NA_QROWS - NA_WIN_ROWS // 2, 0, rows - NA_KROWS)
    ks = pl.multiple_of(ws * GRID_W, (NA_WIN_ROWS // 2) * GRID_W)
    win = NA_KROWS * GRID_W
    dv = o_ref.shape[1]
    sub = 16
    s = lax.dot_general(k_ref[pl.ds(ks, win), :], q_ref[...], (((1,), (1,)), ((), ())),
                        preferred_element_type=F32)
    s = (s + bias_ref[...]).astype(BF16)
    part = jnp.max(s.reshape(win // sub, sub, s.shape[1]), axis=0).astype(F32)
    m = jnp.max(part, axis=0, keepdims=True)
    p = jnp.exp2(s - m.astype(BF16))
    acc = jnp.dot(vT_ref[:, pl.ds(ks, win)], p, preferred_element_type=F32)
    o_ref[...] = (acc[0:dv, :] / acc[dv:dv + 1, :]).T.astype(o_ref.dtype)


def na_attention(q, k, vT, bias, *, batch, seq_len):
    M = q.shape[0]
    rows = seq_len // GRID_W
    assert rows >= NA_KROWS and rows % NA_QROWS == 0
    nblk = rows // NA_QROWS
    tq = NA_QROWS * GRID_W
    pattern = lambda i: jnp.where(i == 0, 0, jnp.where(i == nblk - 1, 2, 1))
    return pl.pallas_call(
        functools.partial(_na_kernel, rows=rows),
        grid=(batch, NA_HEADS, nblk),
        in_specs=[pl.BlockSpec((tq, HEAD_DIM), lambda b, h, i: (b * nblk + i, h)),
                  pl.BlockSpec((seq_len, HEAD_DIM), lambda b, h, i: (b, h)),
                  pl.BlockSpec((VT_ROWS, seq_len), lambda b, h, i: (h, b)),
                  pl.BlockSpec((None, None, NA_KROWS * GRID_W, tq), lambda b, h, i: (h, pattern(i), 0, 0))],
        out_specs=pl.BlockSpec((tq, HEAD_DIM), lambda b, h, i: (b * nblk + i, h)),
        out_shape=jax.ShapeDtypeStruct((M, NA_W), BF16),
        compiler_params=_cparams(("parallel", "parallel", "arbitrary")),
        name="na_attention",
    )(q, k, vT, bias)


def _out_proj_kernel(x_ref, ona_ref, omla_ref, odf_ref, subln_ref, lq1, lk1, lq2, lk2,
                     w_na, w_mla, w_df, ffn_g, o_ref, h_ref, *, lam_init):
    lam = (jnp.exp(jnp.sum(lq1[...] * lk1[...], axis=-1, keepdims=True))
           - jnp.exp(jnp.sum(lq2[...] * lk2[...], axis=-1, keepdims=True)) + lam_init)
    heads = []
    for h in range(DIFF_HEADS):
        o1 = odf_ref[:, (2 * h) * 128:(2 * h + 1) * 128]
        o2 = odf_ref[:, (2 * h + 1) * 128:(2 * h + 2) * 128]
        o = _rms(o1 - lam * o2, subln_ref[...], DIFF_V_DIM) * (1.0 - lam_init)
        heads.append(o.astype(BF16))
    odf = jnp.concatenate(heads, axis=-1)
    y = jnp.dot(ona_ref[...], w_na[...], preferred_element_type=F32)
    y = y + jnp.dot(omla_ref[...], w_mla[...], preferred_element_type=F32)
    y = y + jnp.dot(odf, w_df[...], preferred_element_type=F32)
    x1 = x_ref[...] + y
    o_ref[...] = x1
    h_ref[...] = _rms(x1, ffn_g[...], x1.shape[-1]).astype(BF16)


def out_proj(x, o_na, o_mla, o_df, lw, *, lam_init, tm):
    M, D = x.shape
    row = lambda w: pl.BlockSpec((tm, w), lambda i: (i, 0))
    full = lambda a: pl.BlockSpec(a.shape, lambda i: (0,) * a.ndim)
    params = (lw["subln_g"], lw["lq1"], lw["lk1"], lw["lq2"], lw["lk2"], lw["w_out_na"], lw["w_out_mla"],
              lw["w_out_df"], lw["ffn_g"])
    return pl.pallas_call(
        functools.partial(_out_proj_kernel, lam_init=lam_init),
        grid=(M // tm,),
        in_specs=[row(D), row(NA_W), row(MLA_VW), row(2 * DIFF_HEADS * 128)] + [full(a) for a in params],
        out_specs=(row(D), row(D)),
        out_shape=(jax.ShapeDtypeStruct((M, D), F32), jax.ShapeDtypeStruct((M, D), BF16)),
        compiler_params=_cparams(("parallel",)),
        name="out_proj",
    )(x, o_na, o_mla, o_df, *params)


HALO = 16


def _ffn_kernel(x_ref, h_ref, hp_ref, hn_ref, wg_ref, wu_ref, cw_ref, cb_ref, wd_ref,
                p_ref, pg_ref, wpg_ref, wpp_ref, o_ref, xs_ref, acc_ref, *, seq_len):
    i = pl.program_id(0)
    f = pl.program_id(1)
    tm = x_ref.shape[0]

    @pl.when(f == 0)
    def _():
        row0 = i * tm
        at_start = (row0 % seq_len) == 0
        at_end = ((row0 + tm) % seq_len) == 0
        prev, nxt = hp_ref[...], hn_ref[...]
        xs_ref[0:HALO, :] = jnp.where(at_start, jnp.zeros_like(prev), prev)
        xs_ref[HALO:HALO + tm, :] = h_ref[...]
        xs_ref[HALO + tm:2 * HALO + tm, :] = jnp.where(at_end, jnp.zeros_like(nxt), nxt)
        acc_ref[...] = jnp.zeros_like(acc_ref)

    n_ext = tm + 2 * HALO
    gate = jnp.dot(xs_ref[...], wg_ref[...], preferred_element_type=F32)
    g_prev = pltpu.roll(gate, 1, 0)[HALO:HALO + tm, :]
    g_next = pltpu.roll(gate, n_ext - 1, 0)[HALO:HALO + tm, :]
    g_mid = gate[HALO:HALO + tm, :]
    g = g_prev * cw_ref[0:1, :] + g_mid * cw_ref[1:2, :] + g_next * cw_ref[2:3, :] + cb_ref[...]
    u = jnp.dot(xs_ref[HALO:HALO + tm, :], wu_ref[...], preferred_element_type=F32)
    a = (g * jax.nn.sigmoid(g) * u).astype(BF16)
    acc_ref[...] += jnp.dot(a, wd_ref[...], preferred_element_type=F32)

    @pl.when(f == pl.num_programs(1) - 1)
    def _():
        y = x_ref[...] + acc_ref[...]
        yn = _rms(y, pg_ref[...], y.shape[-1]).astype(BF16)
        gate = jax.nn.sigmoid(jnp.dot(yn, wpg_ref[...], preferred_element_type=F32))
        emb = jnp.dot(p_ref[...].astype(BF16), wpp_ref[...], preferred_element_type=F32)
        o_ref[...] = y + gate * emb


def ffn_ple(x, h, p, lw, *, seq_len, tm):
    M, D = x.shape
    tf = FF_TILE
    nf = lw["w_gate"].shape[1] // tf
    assert seq_len % tm == 0 and tm % HALO == 0
    hb = tm // HALO
    last = M // HALO - 1
    return pl.pallas_call(
        functools.partial(_ffn_kernel, seq_len=seq_len),
        grid=(M // tm, nf),
        in_specs=[pl.BlockSpec((tm, D), lambda i, f: (i, 0), pipeline_mode=pl.Buffered(1)),
                  pl.BlockSpec((tm, D), lambda i, f: (i, 0)),
                  pl.BlockSpec((HALO, D), lambda i, f: (jnp.maximum(i * hb - 1, 0), 0)),
                  pl.BlockSpec((HALO, D), lambda i, f: (jnp.minimum((i + 1) * hb, last), 0)),
                  pl.BlockSpec((D, tf), lambda i, f: (0, f)),
                  pl.BlockSpec((D, tf), lambda i, f: (0, f)),
                  pl.BlockSpec((3, tf), lambda i, f: (0, f)),
                  pl.BlockSpec((1, tf), lambda i, f: (0, f)),
                  pl.BlockSpec((tf, D), lambda i, f: (f, 0)),
                  pl.BlockSpec((tm, p.shape[1]), lambda i, f: (i, 0)),
                  pl.BlockSpec((1, D), lambda i, f: (0, 0)),
                  pl.BlockSpec((D, D), lambda i, f: (0, 0), pipeline_mode=pl.Buffered(1)),
                  pl.BlockSpec((p.shape[1], D), lambda i, f: (0, 0), pipeline_mode=pl.Buffered(1))],
        out_specs=pl.BlockSpec((tm, D), lambda i, f: (i, 0)),
        out_shape=jax.ShapeDtypeStruct((M, D), F32),
        scratch_shapes=[pltpu.VMEM((tm + 2 * HALO, D), BF16), pltpu.VMEM((tm, D), F32)],
        compiler_params=_cparams(("parallel", "arbitrary")),
        name="ffn_ple",
    )(x, h, h, h, lw["w_gate"], lw["w_up"], lw["conv_w"], lw["conv_b"], lw["w_down"],
      p, lw["ple_g"], lw["w_ple_gate"], lw["w_ple_proj"])


def _rope_tables(seq_len):
    def angles(dim, theta):
        inv = 1.0 / (theta ** (jnp.arange(0, dim, 2, dtype=F32) / dim))
        ang = jnp.arange(seq_len, dtype=F32)[:, None] * inv[None, :]
        return jnp.cos(ang), jnp.sin(ang)

    z = lambda w: jnp.zeros((seq_len, w), F32)
    cos, sin = angles(MLA_ROPE_DIM, MLA_ROPE_THETA)
    half = MLA_ROPE_DIM // 2
    cm = jnp.concatenate([cos, cos, z(64)], axis=1)
    sam = jnp.concatenate([-sin, z(half), z(64)], axis=1)
    sbm = jnp.concatenate([z(half), sin, z(64)], axis=1)
    rope_m = jnp.concatenate([cm, sam, sbm], axis=1)

    cos, sin = angles(ROPE_PART_DIM, ROPE_THETA)
    half = ROPE_PART_DIM // 2
    rest = DIFF_QK_DIM - ROPE_PART_DIM
    c64 = jnp.concatenate([cos, cos, jnp.ones((seq_len, rest), F32)], axis=1)
    sa64 = jnp.concatenate([-sin, z(half), z(rest)], axis=1)
    sb64 = jnp.concatenate([z(half), sin, z(rest)], axis=1)
    rope_d = jnp.concatenate([c64, c64, sa64, sa64, sb64, sb64], axis=1)
    return rope_m, rope_d


def _na_bias_table(rpb):
    cols = jnp.arange(GRID_W)
    cs = jnp.clip(cols - NA_WIN_COLS // 2, 0, GRID_W - NA_WIN_COLS)
    kc = jnp.arange(GRID_W)
    valid = (kc[None, :] >= cs[:, None]) & (kc[None, :] < cs[:, None] + NA_WIN_COLS)
    nv = 2 * NA_WIN_COLS - 1
    onehot = (kc[None, None, :] - cols[None, :, None] + (NA_WIN_COLS - 1) == jnp.arange(nv)[:, None, None])
    toep = jnp.einsum("huv,vck->huck", rpb.astype(F32), onehot.astype(F32), precision=lax.Precision.HIGHEST)
    toep = jnp.where(valid[None, None], toep * LOG2E, NEG_BIG)
    n_h = rpb.shape[0]
    masked = 2 * NA_WIN_ROWS - 1
    toep = jnp.concatenate([toep, jnp.full((n_h, 1, GRID_W, GRID_W), NEG_BIG, F32)], axis=1)
    half = NA_WIN_ROWS // 2
    tile = np.full((3, NA_QROWS, NA_KROWS), masked, np.int32)
    for pat, delta in enumerate((0, -half, -NA_WIN_ROWS)):
        for a in range(NA_QROWS):
            first = (max(a - half, 0), a, min(a + half, NA_WIN_ROWS))[pat]
            for i in range(first, first + NA_WIN_ROWS):
                tile[pat, a, i] = delta + i - a + (NA_WIN_ROWS - 1)
    tab = jnp.take(toep, jnp.asarray(tile.reshape(-1)), axis=1)
    tab = tab.reshape(n_h, 3, NA_QROWS, NA_KROWS, GRID_W, GRID_W).transpose(0, 1, 3, 5, 2, 4)
    return tab.reshape(n_h, 3, NA_KROWS * GRID_W, NA_QROWS * GRID_W)


def _layer_params(i, norm_mix, w_in, na_q_norm, na_k_norm, na_rpb, mla_q_a_norm, mla_w_q_b, mla_kv_a_norm,
                  mla_w_kv_b, mla_q_nope_norm, mla_q_pe_norm, mla_k_nope_norm, mla_k_pe_norm,
                  diff_q_norm, diff_k_norm, diff_lambda_q1, diff_lambda_k1, diff_lambda_q2, diff_lambda_k2,
                  diff_subln, w_out, norm_ffn, w_gate, w_up, conv_w, conv_b, w_down,
                  ple_norm, w_ple_gate, w_ple_proj):
    r = lambda a: a[i].reshape(1, -1).astype(F32)
    pad_to = lambda a, n: jnp.pad(a, ((0, 0), (0, n - a.shape[1])))
    w = w_in[i]
    kpe0 = 3 * NA_W + MLA_Q_RANK + MLA_KV_RANK
    w_perm = jnp.concatenate([w[:, :kpe0], w[:, kpe0 + MLA_ROPE_DIM:], w[:, kpe0:kpe0 + MLA_ROPE_DIM],
                              jnp.zeros((w.shape[0], 128 - MLA_ROPE_DIM), w.dtype)], axis=1).astype(BF16)
    wqb = mla_w_q_b[i].reshape(MLA_Q_RANK, MLA_HEADS, MLA_QK_DIM)
    wqb = jnp.pad(wqb, ((0, 0), (0, 0), (0, MLA_SLOT - MLA_QK_DIM))).reshape(MLA_Q_RANK, MLA_HEADS * MLA_SLOT)
    wo = w_out[i].astype(BF16)
    return dict(
        mix_g=r(norm_mix), w_in=w_perm,
        naq_g=r(na_q_norm), nak_g=r(na_k_norm), na_bias=_na_bias_table(na_rpb[i]),
        qa_g=r(mla_q_a_norm), wqb=wqb.astype(BF16), kva_g=r(mla_kv_a_norm), wkvb=mla_w_kv_b[i].astype(BF16),
        qn_g=r(mla_q_nope_norm), qpe_g=pad_to(r(mla_q_pe_norm), 128),
        kn_g=r(mla_k_nope_norm), kpe_g=pad_to(r(mla_k_pe_norm), 128),
        dq_g=jnp.tile(r(diff_q_norm), (1, 2)), dk_g=jnp.tile(r(diff_k_norm), (1, 2)),
        lq1=r(diff_lambda_q1), lk1=r(diff_lambda_k1), lq2=r(diff_lambda_q2), lk2=r(diff_lambda_k2),
        subln_g=r(diff_subln),
        w_out_na=wo[:NA_W], w_out_mla=wo[NA_W:NA_W + MLA_VW], w_out_df=wo[NA_W + MLA_VW:],
        ffn_g=r(norm_ffn), w_gate=w_gate[i].astype(BF16), w_up=w_up[i].astype(BF16),
        conv_w=conv_w[i].astype(F32), conv_b=r(conv_b), w_down=w_down[i].astype(BF16),
        ple_g=r(ple_norm), w_ple_gate=w_ple_gate[i].astype(BF16), w_ple_proj=w_ple_proj[i].astype(BF16),
    )


def _tile(n, pref):
    t = min(pref, n)
    while n % t:
        t //= 2
    return t


def _encoder_layer(x, p_l, lw, rope_m, rope_d, *, layer_idx, batch, seq_len):
    tm = _tile(seq_len, 512)
    (na_q, na_k, na_vT, m_q, m_k, m_vT, d_q, d_k, d_vT) = proj_prep(
        x, rope_m, rope_d, lw, seq_len=seq_len, tm=_tile(seq_len, 256))
    o_na = na_attention(na_q, na_k, na_vT, lw["na_bias"], batch=batch, seq_len=seq_len)
    tq = _tile(seq_len, 512)
    tk = seq_len if seq_len <= 8192 else 8192
    o_mla = flash_attention(m_q, m_k, m_vT, batch=batch, seq_len=seq_len, n_heads=MLA_HEADS, dq=MLA_SLOT,
                            v_of_head=lambda h: h, tq=tq, tk=tk, out_dtype=BF16)
    o_df = flash_attention(d_q, d_k, d_vT, batch=batch, seq_len=seq_len, n_heads=2 * DIFF_HEADS, dq=128,
                           v_of_head=lambda h: h // 2, tq=tq, tk=tk, out_dtype=F32)
    lam_init = 0.8 - 0.6 * math.exp(-0.3 * layer_idx)
    x, h = out_proj(x, o_na, o_mla, o_df, lw, lam_init=lam_init, tm=tm)
    return ffn_ple(x, h, p_l, lw, seq_len=seq_len, tm=tm)


def kernel(x_prompt, x_sample, p_prompt, p_sample, norm_mix, w_in, na_q_norm, na_k_norm, na_rpb, mla_q_a_norm, mla_w_q_b, mla_kv_a_norm, mla_w_kv_b, mla_q_nope_norm, mla_q_pe_norm, mla_k_nope_norm, mla_k_pe_norm, diff_q_norm, diff_k_norm, diff_lambda_q1, diff_lambda_k1, diff_lambda_q2, diff_lambda_k2, diff_subln, w_out, norm_ffn, w_gate, w_up, conv_w, conv_b, w_down, ple_norm, w_ple_gate, w_ple_proj):
    weights = (norm_mix, w_in, na_q_norm, na_k_norm, na_rpb, mla_q_a_norm, mla_w_q_b, mla_kv_a_norm, mla_w_kv_b,
               mla_q_nope_norm, mla_q_pe_norm, mla_k_nope_norm, mla_k_pe_norm, diff_q_norm, diff_k_norm,
               diff_lambda_q1, diff_lambda_k1, diff_lambda_q2, diff_lambda_k2, diff_subln, w_out,
               norm_ffn, w_gate, w_up, conv_w, conv_b, w_down, ple_norm, w_ple_gate, w_ple_proj)
    depth = norm_mix.shape[0]
    groups = []
    for x, p in ((x_prompt, p_prompt), (x_sample, p_sample)):
        b, t, d = x.shape
        groups.append(dict(x=x.reshape(b * t, d), p=p.reshape(depth, b * t, p.shape[-1]), batch=b, seq_len=t,
                           rope=_rope_tables(t), shape=x.shape))
    for i in range(depth):
        lw = _layer_params(i, *weights)
        for g in groups:
            g["x"] = _encoder_layer(g["x"], g["p"][i], lw, *g["rope"], layer_idx=i,
                                    batch=g["batch"], seq_len=g["seq_len"])
    return tuple(g["x"].reshape(g["shape"]) for g in groups)
```

```python
import functools
import math

import jax
import jax.numpy as jnp
import numpy as np
from jax import lax
from jax.experimental import pallas as pl
from jax.experimental.pallas import tpu as pltpu

F32 = jnp.float32
BF16 = jnp.bfloat16

EPS = 1e-6
LOG2E = 1.4426950408889634
NEG_BIG = -1e30

D_MODEL = 2048
PLE_DIM = 256
GRID_W = 64
HEAD_DIM = 128
NA_HEADS = 6
NA_WIN_ROWS = 8
NA_WIN_COLS = 16
MLA_HEADS = 5
MLA_Q_RANK = 512
MLA_KV_RANK = 256
MLA_NOPE_DIM = 128
MLA_ROPE_DIM = 64
MLA_V_DIM = 128
MLA_ROPE_THETA = 10000.0
DIFF_HEADS = 5
DIFF_QK_DIM = 64
DIFF_V_DIM = 128
ROPE_THETA = 500000.0
ROPE_PART_DIM = DIFF_QK_DIM // 4
D_FF = 5632

NA_W = NA_HEADS * HEAD_DIM
MLA_QK_DIM = MLA_NOPE_DIM + MLA_ROPE_DIM
MLA_SLOT = 256
DIFF_W = DIFF_HEADS * 2 * DIFF_QK_DIM
DIFF_VW = DIFF_HEADS * DIFF_V_DIM
MLA_VW = MLA_HEADS * MLA_V_DIM
VT_ROWS = 128 + 16

C_NAQ = 0
C_NAK = C_NAQ + NA_W
C_NAV = C_NAK + NA_W
C_CQ = C_NAV + NA_W
C_CKV = C_CQ + MLA_Q_RANK
C_DQ = C_CKV + MLA_KV_RANK
C_DK = C_DQ + DIFF_W
C_DV = C_DK + DIFF_W
C_KPE = C_DV + DIFF_VW
IN_COLS_PAD = C_KPE + 128

LANES = 128
FF_TILE = 512
VMEM_LIMIT = 56 * 1024 * 1024
VMEM_LIMIT_FFN = 60 * 1024 * 1024


def _cparams(sem, limit=VMEM_LIMIT):
    return pltpu.CompilerParams(dimension_semantics=sem, vmem_limit_bytes=limit)


def _rms(x, g, n):
    ms = jnp.sum(x * x, axis=-1, keepdims=True) * (1.0 / n)
    return x * lax.rsqrt(ms + EPS) * g


def _proj_prep_kernel(x_ref, mix_g, w_ref, rope_m_ref, rope_d_ref, naq_g, nak_g, qa_g, wqb_ref, kva_g, wkvb_ref,
                      qn_g, qpe_g, kn_g, kpe_g, dq_g, dk_g,
                      naq_o, nak_o, navT_o, mq_o, mk_o, mvT_o, dq_o, dk_o, dvT_o):
    tm = x_ref.shape[0]
    xn = _rms(x_ref[...], mix_g[...], x_ref.shape[-1]).astype(BF16)

    def proj(col, width):
        return jnp.dot(xn, w_ref[:, col:col + width], preferred_element_type=F32)

    lane = lax.broadcasted_iota(jnp.int32, (tm, LANES), 1)
    lo = lane < 64
    ones_row = (lax.broadcasted_iota(jnp.int32, (VT_ROWS - 128, tm), 0) == 0).astype(BF16)

    na_scale = HEAD_DIM ** -0.5 * LOG2E
    na_q, na_k, na_v = proj(C_NAQ, NA_W), proj(C_NAK, NA_W), proj(C_NAV, NA_W)
    for h in range(NA_HEADS):
        sl = slice(h * HEAD_DIM, (h + 1) * HEAD_DIM)
        naq_o[:, sl] = (_rms(na_q[:, sl], naq_g[...], HEAD_DIM) * na_scale).astype(BF16)
        nak_o[:, sl] = _rms(na_k[:, sl], nak_g[...], HEAD_DIM).astype(BF16)
        navT_o[h * VT_ROWS:h * VT_ROWS + HEAD_DIM, :] = na_v[:, sl].T.astype(BF16)
        navT_o[h * VT_ROWS + HEAD_DIM:(h + 1) * VT_ROWS, :] = ones_row

    cm, sam, sbm = rope_m_ref[:, 0:128], rope_m_ref[:, 128:256], rope_m_ref[:, 256:384]

    def rope_m(y):
        return y * cm + pltpu.roll(y, 96, 1) * sam + pltpu.roll(y, 32, 1) * sbm

    mla_scale = MLA_QK_DIM ** -0.5 * LOG2E
    cq = _rms(proj(C_CQ, MLA_Q_RANK), qa_g[...], MLA_Q_RANK).astype(BF16)
    qm = jnp.dot(cq, wqb_ref[...], preferred_element_type=F32)
    for h in range(MLA_HEADS):
        b = h * MLA_SLOT
        nope = _rms(qm[:, b:b + 128], qn_g[...], MLA_NOPE_DIM)
        pe = rope_m(_rms(qm[:, b + 128:b + 256], qpe_g[...], MLA_ROPE_DIM))
        mq_o[:, b:b + 128] = (nope * mla_scale).astype(BF16)
        mq_o[:, b + 128:b + 256] = (pe * mla_scale).astype(BF16)
    ckv = _rms(proj(C_CKV, MLA_KV_RANK), kva_g[...], MLA_KV_RANK).astype(BF16)
    kv = jnp.dot(ckv, wkvb_ref[...], preferred_element_type=F32)
    kpe = rope_m(_rms(proj(C_KPE, 128), kpe_g[...], MLA_ROPE_DIM))
    kpe = kpe.astype(BF16)
    for h in range(MLA_HEADS):
        b = h * MLA_SLOT
        kn = _rms(kv[:, b:b + 128], kn_g[...], MLA_NOPE_DIM)
        mk_o[:, b:b + 128] = kn.astype(BF16)
        mk_o[:, b + 128:b + 256] = kpe
        mvT_o[h * VT_ROWS:h * VT_ROWS + MLA_V_DIM, :] = kv[:, b + 128:b + 256].T.astype(BF16)
        mvT_o[h * VT_ROWS + MLA_V_DIM:(h + 1) * VT_ROWS, :] = ones_row

    cd, sad, sbd = rope_d_ref[:, 0:128], rope_d_ref[:, 128:256], rope_d_ref[:, 256:384]

    def rope_d(y):
        return y * cd + pltpu.roll(y, 120, 1) * sad + pltpu.roll(y, 8, 1) * sbd

    def group_rms(x, g):
        x2 = x * x
        s_lo = jnp.sum(jnp.where(lo, x2, 0.0), axis=-1, keepdims=True)
        s_hi = jnp.sum(jnp.where(lo, 0.0, x2), axis=-1, keepdims=True)
        ms = jnp.where(lo, s_lo, s_hi) * (1.0 / DIFF_QK_DIM)
        return x * lax.rsqrt(ms + EPS) * g

    def split_components(y):
        return jnp.where(lo, y, 0.0), jnp.where(lo, pltpu.roll(y, 64, 1), 0.0)

    df_scale = DIFF_QK_DIM ** -0.5 * LOG2E
    df_q, df_k, df_v = proj(C_DQ, DIFF_W), proj(C_DK, DIFF_W), proj(C_DV, DIFF_VW)
    for h in range(DIFF_HEADS):
        q = rope_d(group_rms(df_q[:, h * 128:(h + 1) * 128], dq_g[...])) * df_scale
        q0, q1 = split_components(q)
        dq_o[:, (2 * h) * 128:(2 * h + 1) * 128] = q0.astype(BF16)
        dq_o[:, (2 * h + 1) * 128:(2 * h + 2) * 128] = q1.astype(BF16)
        k = rope_d(group_rms(df_k[:, h * 128:(h + 1) * 128], dk_g[...]))
        k0, k1 = split_components(k)
        dk_o[:, (2 * h) * 128:(2 * h + 1) * 128] = k0.astype(BF16)
        dk_o[:, (2 * h + 1) * 128:(2 * h + 2) * 128] = k1.astype(BF16)
        v = df_v[:, h * DIFF_V_DIM:(h + 1) * DIFF_V_DIM]
        dvT_o[h * VT_ROWS:h * VT_ROWS + DIFF_V_DIM, :] = v.T.astype(BF16)
        dvT_o[h * VT_ROWS + DIFF_V_DIM:(h + 1) * VT_ROWS, :] = ones_row


def proj_prep(x, rope_m, rope_d, lw, *, seq_len, tm):
    M, D = x.shape
    nt = seq_len // tm
    row = lambda w: pl.BlockSpec((tm, w), lambda i: (i, 0))
    colT = lambda h: pl.BlockSpec((h, tm), lambda i: (0, i))
    full = lambda a: pl.BlockSpec(a.shape, lambda i: (0,) * a.ndim)
    rope = pl.BlockSpec((tm, 384), lambda i: (i % nt, 0))
    w_in = pl.BlockSpec(lw["w_in"].shape, lambda i: (0, 0), pipeline_mode=pl.Buffered(1))
    params = (lw["naq_g"], lw["nak_g"], lw["qa_g"], lw["wqb"], lw["kva_g"], lw["wkvb"],
              lw["qn_g"], lw["qpe_g"], lw["kn_g"], lw["kpe_g"], lw["dq_g"], lw["dk_g"])
    out_shapes = (
        jax.ShapeDtypeStruct((M, NA_W), BF16), jax.ShapeDtypeStruct((M, NA_W), BF16),
        jax.ShapeDtypeStruct((NA_HEADS * VT_ROWS, M), BF16),
        jax.ShapeDtypeStruct((M, MLA_HEADS * MLA_SLOT), BF16),
        jax.ShapeDtypeStruct((M, MLA_HEADS * MLA_SLOT), BF16),
        jax.ShapeDtypeStruct((MLA_HEADS * VT_ROWS, M), BF16),
        jax.ShapeDtypeStruct((M, 2 * DIFF_HEADS * 128), BF16),
        jax.ShapeDtypeStruct((M, 2 * DIFF_HEADS * 128), BF16),
        jax.ShapeDtypeStruct((DIFF_HEADS * VT_ROWS, M), BF16),
    )
    out_specs = (row(NA_W), row(NA_W), colT(NA_HEADS * VT_ROWS), row(MLA_HEADS * MLA_SLOT), row(MLA_HEADS * MLA_SLOT),
                 colT(MLA_HEADS * VT_ROWS), row(2 * DIFF_HEADS * 128), row(2 * DIFF_HEADS * 128),
                 colT(DIFF_HEADS * VT_ROWS))
    return pl.pallas_call(
        _proj_prep_kernel,
        grid=(M // tm,),
        in_specs=[row(D), full(lw["mix_g"]), w_in, rope, rope] + [full(a) for a in params],
        out_specs=out_specs,
        out_shape=out_shapes,
        compiler_params=_cparams(("parallel",)),
        name="proj_prep",
    )(x, lw["mix_g"], lw["w_in"], rope_m, rope_d, *params)


def _flash_kernel(q_ref, k_ref, vT_ref, o_ref, sa_ref, sb_ref, acc_ref, *, tq, tk):
    seq_len = q_ref.shape[0]
    dv = o_ref.shape[1]
    nk = seq_len // tk
    total = (seq_len // tq) * nk
    sub = 16

    def scores(f):
        qoff = pl.multiple_of((f // nk) * tq, tq)
        koff = pl.multiple_of((f % nk) * tk, tk)
        s = lax.dot_general(k_ref[pl.ds(koff, tk), :], q_ref[pl.ds(qoff, tq), :],
                            (((1,), (1,)), ((), ())), preferred_element_type=F32)
        return s.astype(BF16)

    def softmax_pv(s_ref, c, m):
        s = s_ref[...]
        part = jnp.max(s.reshape(tk // sub, sub, tq), axis=0).astype(F32)
        m_new = jnp.maximum(m, jnp.max(part, axis=0, keepdims=True))
        alpha = jnp.exp2(m - m_new)
        p = jnp.exp2(s - m_new.astype(BF16))
        koff = pl.multiple_of(c * tk, tk)
        pv = jnp.dot(vT_ref[:, pl.ds(koff, tk)], p, preferred_element_type=F32)
        acc_ref[...] = alpha * acc_ref[...] + pv
        return m_new

    def write_block(qb):
        qoff = pl.multiple_of(qb * tq, tq)
        out = acc_ref[0:dv, :] / acc_ref[dv:dv + 1, :]
        o_ref[pl.ds(qoff, tq), :] = out.T.astype(o_ref.dtype)

    def pair(i, m):
        f = 2 * i
        if nk == 1:
            sb_ref[...] = scores(f + 1)
            softmax_pv(sa_ref, 0, m)
            write_block(f)
            sa_ref[...] = scores(jnp.minimum(f + 2, total - 1))
            softmax_pv(sb_ref, 0, m)
            write_block(f + 1)
            return m
        c = f % nk
        m = jnp.where(c == 0, NEG_BIG, m)
        sb_ref[...] = scores(f + 1)
        m = softmax_pv(sa_ref, c, m)
        sa_ref[...] = scores(jnp.minimum(f + 2, total - 1))
        m = softmax_pv(sb_ref, c + 1, m)

        @pl.when(c + 2 == nk)
        def _():
            write_block(f // nk)

        return m

    acc_ref[...] = jnp.zeros_like(acc_ref)
    sa_ref[...] = scores(0)
    lax.fori_loop(0, total // 2, pair, jnp.full((1, tq), NEG_BIG, F32))


def flash_attention(q, k, vT, *, batch, seq_len, n_heads, dq, v_of_head, tq, tk, out_dtype):
    M = q.shape[0]
    dv = 128
    nk = seq_len // tk
    assert seq_len % tk == 0 and seq_len % tq == 0 and (nk % 2 == 0 or (nk == 1 and (seq_len // tq) % 2 == 0))
    once = pl.Buffered(1 if seq_len * dq * 2 > 2 * 1024 * 1024 else 2)
    return pl.pallas_call(
        functools.partial(_flash_kernel, tq=tq, tk=tk),
        grid=(batch, n_heads),
        in_specs=[pl.BlockSpec((seq_len, dq), lambda b, h: (b, h), pipeline_mode=once),
                  pl.BlockSpec((seq_len, dq), lambda b, h: (b, h), pipeline_mode=once),
                  pl.BlockSpec((VT_ROWS, seq_len), lambda b, h: (v_of_head(h), b), pipeline_mode=once)],
        out_specs=pl.BlockSpec((seq_len, dv), lambda b, h: (b, h)),
        out_shape=jax.ShapeDtypeStruct((M, n_heads * dv), out_dtype),
        scratch_shapes=[pltpu.VMEM((tk, tq), BF16), pltpu.VMEM((tk, tq), BF16), pltpu.VMEM((VT_ROWS, tq), F32)],
        compiler_params=_cparams(("parallel", "parallel")),
        name="flash_attention",
    )(q, k, vT)


NA_QROWS = 8
NA_KROWS = 2 * NA_WIN_ROWS


def _na_kernel(q_ref, k_ref, vT_ref, bias_ref, o_ref, *, rows):
    i = pl.program_id(2)
    ws = jnp.clip(i * NA_QROWS - NA_WIN_ROWS // 2, 0, rows - NA_KROWS)
    ks = pl.multiple_of(ws * GRID_W, (NA_WIN_ROWS // 2) * GRID_W)
    win = NA_KROWS * GRID_W
    dv = o_ref.shape[1]
    sub = 16
    s = lax.dot_general(k_ref[pl.ds(ks, win), :], q_ref[...], (((1,), (1,)), ((), ())),
                        preferred_element_type=F32)
    s = (s + bias_ref[...]).astype(BF16)
    part = jnp.max(s.reshape(win // sub, sub, s.shape[1]), axis=0).astype(F32)
    m = jnp.max(part, axis=0, keepdims=True)
    p = jnp.exp2(s - m.astype(BF16))
    acc = jnp.dot(vT_ref[:, pl.ds(ks, win)], p, preferred_element_type=F32)
    o_ref[...] = (acc[0:dv, :] / acc[dv:dv + 1, :]).T.astype(o_ref.dtype)


def na_attention(q, k, vT, bias, *, batch, seq_len):
    M = q.shape[0]
    rows = seq_len // GRID_W
    assert rows >= NA_KROWS and rows % NA_QROWS == 0
    nblk = rows // NA_QROWS
    tq = NA_QROWS * GRID_W
    pattern = lambda i: jnp.where(i == 0, 0, jnp.where(i == nblk - 1, 2, 1))
    return pl.pallas_call(
        functools.partial(_na_kernel, rows=rows),
        grid=(batch, NA_HEADS, nblk),
        in_specs=[pl.BlockSpec((tq, HEAD_DIM), lambda b, h, i: (b * nblk + i, h)),
                  pl.BlockSpec((seq_len, HEAD_DIM), lambda b, h, i: (b, h)),
                  pl.BlockSpec((VT_ROWS, seq_len), lambda b, h, i: (h, b)),
                  pl.BlockSpec((None, None, NA_KROWS * GRID_W, tq), lambda b, h, i: (h, pattern(i), 0, 0))],
        out_specs=pl.BlockSpec((tq, HEAD_DIM), lambda b, h, i: (b * nblk + i, h)),
        out_shape=jax.ShapeDtypeStruct((M, NA_W), BF16),
        compiler_params=_cparams(("parallel", "parallel", "arbitrary")),
        name="na_attention",
    )(q, k, vT, bias)


def _out_proj_kernel(x_ref, ona_ref, omla_ref, odf_ref, subln_ref, lq1, lk1, lq2, lk2,
                     w_na, w_mla, w_df, ffn_g, o_ref, h_ref, *, lam_init):
    lam = (jnp.exp(jnp.sum(lq1[...] * lk1[...], axis=-1, keepdims=True))
           - jnp.exp(jnp.sum(lq2[...] * lk2[...], axis=-1, keepdims=True)) + lam_init)
    heads = []
    for h in range(DIFF_HEADS):
        o1 = odf_ref[:, (2 * h) * 128:(2 * h + 1) * 128]
        o2 = odf_ref[:, (2 * h + 1) * 128:(2 * h + 2) * 128]
        o = _rms(o1 - lam * o2, subln_ref[...], DIFF_V_DIM) * (1.0 - lam_init)
        heads.append(o.astype(BF16))
    odf = jnp.concatenate(heads, axis=-1)
    y = jnp.dot(ona_ref[...], w_na[...], preferred_element_type=F32)
    y = y + jnp.dot(omla_ref[...], w_mla[...], preferred_element_type=F32)
    y = y + jnp.dot(odf, w_df[...], preferred_element_type=F32)
    x1 = x_ref[...] + y
    o_ref[...] = x1
    h_ref[...] = _rms(x1, ffn_g[...], x1.shape[-1]).astype(BF16)


def out_proj(x, o_na, o_mla, o_df, lw, *, lam_init, tm):
    M, D = x.shape
    row = lambda w: pl.BlockSpec((tm, w), lambda i: (i, 0))
    full = lambda a: pl.BlockSpec(a.shape, lambda i: (0,) * a.ndim)
    params = (lw["subln_g"], lw["lq1"], lw["lk1"], lw["lq2"], lw["lk2"], lw["w_out_na"], lw["w_out_mla"],
              lw["w_out_df"], lw["ffn_g"])
    return pl.pallas_call(
        functools.partial(_out_proj_kernel, lam_init=lam_init),
        grid=(M // tm,),
        in_specs=[row(D), row(NA_W), row(MLA_VW), row(2 * DIFF_HEADS * 128)] + [full(a) for a in params],
        out_specs=(row(D), row(D)),
        out_shape=(jax.ShapeDtypeStruct((M, D), F32), jax.ShapeDtypeStruct((M, D), BF16)),
        compiler_params=_cparams(("parallel",)),
        name="out_proj",
    )(x, o_na, o_mla, o_df, *params)


HALO = 16


def _ffn_kernel(x_ref, h_ref, hp_ref, hn_ref, wg_ref, wu_ref, cw_ref, cb_ref, wd_ref,
                p_ref, pg_ref, wpg_ref, wpp_ref, o_ref, xs_ref, acc_ref, *, seq_len):
    i = pl.program_id(0)
    f = pl.program_id(1)
    tm = x_ref.shape[0]

    @pl.when(f == 0)
    def _():
        row0 = i * tm
        at_start = (row0 % seq_len) == 0
        at_end = ((row0 + tm) % seq_len) == 0
        prev, nxt = hp_ref[...], hn_ref[...]
        xs_ref[0:HALO, :] = jnp.where(at_start, jnp.zeros_like(prev), prev)
        xs_ref[HALO:HALO + tm, :] = h_ref[...]
        xs_ref[HALO + tm:2 * HALO + tm, :] = jnp.where(at_end, jnp.zeros_like(nxt), nxt)
        acc_ref[...] = jnp.zeros_like(acc_ref)

    n_ext = tm + 2 * HALO
    gate = jnp.dot(xs_ref[...], wg_ref[...], preferred_element_type=F32)
    g_prev = pltpu.roll(gate, 1, 0)[HALO:HALO + tm, :]
    g_next = pltpu.roll(gate, n_ext - 1, 0)[HALO:HALO + tm, :]
    g_mid = gate[HALO:HALO + tm, :]
    g = g_prev * cw_ref[0:1, :] + g_mid * cw_ref[1:2, :] + g_next * cw_ref[2:3, :] + cb_ref[...]
    u = jnp.dot(xs_ref[HALO:HALO + tm, :], wu_ref[...], preferred_element_type=F32)
    a = (g * jax.nn.sigmoid(g) * u).astype(BF16)
    acc_ref[...] += jnp.dot(a, wd_ref[...], preferred_element_type=F32)

    @pl.when(f == pl.num_programs(1) - 1)
    def _():
        y = x_ref[...] + acc_ref[...]
        yn = _rms(y, pg_ref[...], y.shape[-1]).astype(BF16)
        gate = jax.nn.sigmoid(jnp.dot(yn, wpg_ref[...], preferred_element_type=F32))
        emb = jnp.dot(p_ref[...].astype(BF16), wpp_ref[...], preferred_element_type=F32)
        o_ref[...] = y + gate * emb


def ffn_ple(x, h, p, lw, *, seq_len, tm):
    M, D = x.shape
    tf = FF_TILE
    nf = lw["w_gate"].shape[1] // tf
    assert seq_len % tm == 0 and tm % HALO == 0
    hb = tm // HALO
    last = M // HALO - 1
    return pl.pallas_call(
        functools.partial(_ffn_kernel, seq_len=seq_len),
        grid=(M // tm, nf),
        in_specs=[pl.BlockSpec((tm, D), lambda i, f: (i, 0)),
                  pl.BlockSpec((tm, D), lambda i, f: (i, 0)),
                  pl.BlockSpec((HALO, D), lambda i, f: (jnp.maximum(i * hb - 1, 0), 0)),
                  pl.BlockSpec((HALO, D), lambda i, f: (jnp.minimum((i + 1) * hb, last), 0)),
                  pl.BlockSpec((D, tf), lambda i, f: (0, f)),
                  pl.BlockSpec((D, tf), lambda i, f: (0, f)),
                  pl.BlockSpec((3, tf), lambda i, f: (0, f)),
                  pl.BlockSpec((1, tf), lambda i, f: (0, f)),
                  pl.BlockSpec((tf, D), lambda i, f: (f, 0)),
                  pl.BlockSpec((tm, p.shape[1]), lambda i, f: (i, 0)),
                  pl.BlockSpec((1, D), lambda i, f: (0, 0)),
                  pl.BlockSpec((D, D), lambda i, f: (0, 0), pipeline_mode=pl.Buffered(1)),
                  pl.BlockSpec((p.shape[1], D), lambda i, f: (0, 0), pipeline_mode=pl.Buffered(1))],
        out_specs=pl.BlockSpec((tm, D), lambda i, f: (i, 0)),
        out_shape=jax.ShapeDtypeStruct((M, D), F32),
        scratch_shapes=[pltpu.VMEM((tm + 2 * HALO, D), BF16), pltpu.VMEM((tm, D), F32)],
        compiler_params=_cparams(("parallel", "arbitrary"), VMEM_LIMIT_FFN),
        name="ffn_ple",
    )(x, h, h, h, lw["w_gate"], lw["w_up"], lw["conv_w"], lw["conv_b"], lw["w_down"],
      p, lw["ple_g"], lw["w_ple_gate"], lw["w_ple_proj"])


def _rope_tables(seq_len):
    def angles(dim, theta):
        inv = 1.0 / (theta ** (jnp.arange(0, dim, 2, dtype=F32) / dim))
        ang = jnp.arange(seq_len, dtype=F32)[:, None] * inv[None, :]
        return jnp.cos(ang), jnp.sin(ang)

    z = lambda w: jnp.zeros((seq_len, w), F32)
    cos, sin = angles(MLA_ROPE_DIM, MLA_ROPE_THETA)
    half = MLA_ROPE_DIM // 2
    cm = jnp.concatenate([cos, cos, z(64)], axis=1)
    sam = jnp.concatenate([-sin, z(half), z(64)], axis=1)
    sbm = jnp.concatenate([z(half), sin, z(64)], axis=1)
    rope_m = jnp.concatenate([cm, sam, sbm], axis=1)

    cos, sin = angles(ROPE_PART_DIM, ROPE_THETA)
    half = ROPE_PART_DIM // 2
    rest = DIFF_QK_DIM - ROPE_PART_DIM
    c64 = jnp.concatenate([cos, cos, jnp.ones((seq_len, rest), F32)], axis=1)
    sa64 = jnp.concatenate([-sin, z(half), z(rest)], axis=1)
    sb64 = jnp.concatenate([z(half), sin, z(rest)], axis=1)
    rope_d = jnp.concatenate([c64, c64, sa64, sa64, sb64, sb64], axis=1)
    return rope_m, rope_d


def _na_bias_table(rpb):
    cols = jnp.arange(GRID_W)
    cs = jnp.clip(cols - NA_WIN_COLS // 2, 0, GRID_W - NA_WIN_COLS)
    kc = jnp.arange(GRID_W)
    valid = (kc[None, :] >= cs[:, None]) & (kc[None, :] < cs[:, None] + NA_WIN_COLS)
    nv = 2 * NA_WIN_COLS - 1
    onehot = (kc[None, None, :] - cols[None, :, None] + (NA_WIN_COLS - 1) == jnp.arange(nv)[:, None, None])
    toep = jnp.einsum("huv,vck->huck", rpb.astype(F32), onehot.astype(F32), precision=lax.Precision.HIGHEST)
    toep = jnp.where(valid[None, None], toep * LOG2E, NEG_BIG)
    n_h = rpb.shape[0]
    masked = 2 * NA_WIN_ROWS - 1
    toep = jnp.concatenate([toep, jnp.full((n_h, 1, GRID_W, GRID_W), NEG_BIG, F32)], axis=1)
    half = NA_WIN_ROWS // 2
    tile = np.full((3, NA_QROWS, NA_KROWS), masked, np.int32)
    for pat, delta in enumerate((0, -half, -NA_WIN_ROWS)):
        for a in range(NA_QROWS):
            first = (max(a - half, 0), a, min(a + half, NA_WIN_ROWS))[pat]
            for i in range(first, first + NA_WIN_ROWS):
                tile[pat, a, i] = delta + i - a + (NA_WIN_ROWS - 1)
    tab = jnp.take(toep, jnp.asarray(tile.reshape(-1)), axis=1)
    tab = tab.reshape(n_h, 3, NA_QROWS, NA_KROWS, GRID_W, GRID_W).transpose(0, 1, 3, 5, 2, 4)
    return tab.reshape(n_h, 3, NA_KROWS * GRID_W, NA_QROWS * GRID_W)


def _layer_params(i, norm_mix, w_in, na_q_norm, na_k_norm, na_rpb, mla_q_a_norm, mla_w_q_b, mla_kv_a_norm,
                  mla_w_kv_b, mla_q_nope_norm, mla_q_pe_norm, mla_k_nope_norm, mla_k_pe_norm,
                  diff_q_norm, diff_k_norm, diff_lambda_q1, diff_lambda_k1, diff_lambda_q2, diff_lambda_k2,
                  diff_subln, w_out, norm_ffn, w_gate, w_up, conv_w, conv_b, w_down,
                  ple_norm, w_ple_gate, w_ple_proj):
    r = lambda a: a[i].reshape(1, -1).astype(F32)
    pad_to = lambda a, n: jnp.pad(a, ((0, 0), (0, n - a.shape[1])))
    w = w_in[i]
    kpe0 = 3 * NA_W + MLA_Q_RANK + MLA_KV_RANK
    w_perm = jnp.concatenate([w[:, :kpe0], w[:, kpe0 + MLA_ROPE_DIM:], w[:, kpe0:kpe0 + MLA_ROPE_DIM],
                              jnp.zeros((w.shape[0], 128 - MLA_ROPE_DIM), w.dtype)], axis=1).astype(BF16)
    wqb = mla_w_q_b[i].reshape(MLA_Q_RANK, MLA_HEADS, MLA_QK_DIM)
    wqb = jnp.pad(wqb, ((0, 0), (0, 0), (0, MLA_SLOT - MLA_QK_DIM))).reshape(MLA_Q_RANK, MLA_HEADS * MLA_SLOT)
    wo = w_out[i].astype(BF16)
    return dict(
        mix_g=r(norm_mix), w_in=w_perm,
        naq_g=r(na_q_norm), nak_g=r(na_k_norm), na_bias=_na_bias_table(na_rpb[i]),
        qa_g=r(mla_q_a_norm), wqb=wqb.astype(BF16), kva_g=r(mla_kv_a_norm), wkvb=mla_w_kv_b[i].astype(BF16),
        qn_g=r(mla_q_nope_norm), qpe_g=pad_to(r(mla_q_pe_norm), 128),
        kn_g=r(mla_k_nope_norm), kpe_g=pad_to(r(mla_k_pe_norm), 128),
        dq_g=jnp.tile(r(diff_q_norm), (1, 2)), dk_g=jnp.tile(r(diff_k_norm), (1, 2)),
        lq1=r(diff_lambda_q1), lk1=r(diff_lambda_k1), lq2=r(diff_lambda_q2), lk2=r(diff_lambda_k2),
        subln_g=r(diff_subln),
        w_out_na=wo[:NA_W], w_out_mla=wo[NA_W:NA_W + MLA_VW], w_out_df=wo[NA_W + MLA_VW:],
        ffn_g=r(norm_ffn), w_gate=w_gate[i].astype(BF16), w_up=w_up[i].astype(BF16),
        conv_w=conv_w[i].astype(F32), conv_b=r(conv_b), w_down=w_down[i].astype(BF16),
        ple_g=r(ple_norm), w_ple_gate=w_ple_gate[i].astype(BF16), w_ple_proj=w_ple_proj[i].astype(BF16),
    )


def _tile(n, pref):
    t = min(pref, n)
    while n % t:
        t //= 2
    return t


def _encoder_layer(x, p_l, lw, rope_m, rope_d, *, layer_idx, batch, seq_len):
    tm = _tile(seq_len, 512)
    (na_q, na_k, na_vT, m_q, m_k, m_vT, d_q, d_k, d_vT) = proj_prep(
        x, rope_m, rope_d, lw, seq_len=seq_len, tm=_tile(seq_len, 256))
    o_na = na_attention(na_q, na_k, na_vT, lw["na_bias"], batch=batch, seq_len=seq_len)
    tq = _tile(seq_len, 512)
    tk = seq_len if seq_len <= 8192 else 8192
    o_mla = flash_attention(m_q, m_k, m_vT, batch=batch, seq_len=seq_len, n_heads=MLA_HEADS, dq=MLA_SLOT,
                            v_of_head=lambda h: h, tq=tq, tk=tk, out_dtype=BF16)
    o_df = flash_attention(d_q, d_k, d_vT, batch=batch, seq_len=seq_len, n_heads=2 * DIFF_HEADS, dq=128,
                           v_of_head=lambda h: h // 2, tq=tq, tk=tk, out_dtype=F32)
    lam_init = 0.8 - 0.6 * math.exp(-0.3 * layer_idx)
    x, h = out_proj(x, o_na, o_mla, o_df, lw, lam_init=lam_init, tm=tm)
    return ffn_ple(x, h, p_l, lw, seq_len=seq_len, tm=tm)


def kernel(x_prompt, x_sample, p_prompt, p_sample, norm_mix, w_in, na_q_norm, na_k_norm, na_rpb, mla_q_a_norm, mla_w_q_b, mla_kv_a_norm, mla_w_kv_b, mla_q_nope_norm, mla_q_pe_norm, mla_k_nope_norm, mla_k_pe_norm, diff_q_norm, diff_k_norm, diff_lambda_q1, diff_lambda_k1, diff_lambda_q2, diff_lambda_k2, diff_subln, w_out, norm_ffn, w_gate, w_up, conv_w, conv_b, w_down, ple_norm, w_ple_gate, w_ple_proj):
    weights = (norm_mix, w_in, na_q_norm, na_k_norm, na_rpb, mla_q_a_norm, mla_w_q_b, mla_kv_a_norm, mla_w_kv_b,
               mla_q_nope_norm, mla_q_pe_norm, mla_k_nope_norm, mla_k_pe_norm, diff_q_norm, diff_k_norm,
               diff_lambda_q1, diff_lambda_k1, diff_lambda_q2, diff_lambda_k2, diff_subln, w_out,
               norm_ffn, w_gate, w_up, conv_w, conv_b, w_down, ple_norm, w_ple_gate, w_ple_proj)
    depth = norm_mix.shape[0]
    groups = []
    for x, p in ((x_prompt, p_prompt), (x_sample, p_sample)):
        b, t, d = x.shape
        groups.append(dict(x=x.reshape(b * t, d), p=p.reshape(depth, b * t, p.shape[-1]), batch=b, seq_len=t,
                           rope=_rope_tables(t), shape=x.shape))
    for i in range(depth):
        lw = _layer_params(i, *weights)
        for g in groups:
            g["x"] = _encoder_layer(g["x"], g["p"][i], lw, *g["rope"], layer_idx=i,
                                    batch=g["batch"], seq_len=g["seq_len"])
    return tuple(g["x"].reshape(g["shape"]) for g in groups)
```

```python
import functools
import math

import jax
import jax.numpy as jnp
import numpy as np
from jax import lax
from jax.experimental import pallas as pl
from jax.experimental.pallas import tpu as pltpu

F32 = jnp.float32
BF16 = jnp.bfloat16

EPS = 1e-6
LOG2E = 1.4426950408889634
NEG_BIG = -1e30

D_MODEL = 2048
PLE_DIM = 256
GRID_W = 64
HEAD_DIM = 128
NA_HEADS = 6
NA_WIN_ROWS = 8
NA_WIN_COLS = 16
MLA_HEADS = 5
MLA_Q_RANK = 512
MLA_KV_RANK = 256
MLA_NOPE_DIM = 128
MLA_ROPE_DIM = 64
MLA_V_DIM = 128
MLA_ROPE_THETA = 10000.0
DIFF_HEADS = 5
DIFF_QK_DIM = 64
DIFF_V_DIM = 128
ROPE_THETA = 500000.0
ROPE_PART_DIM = DIFF_QK_DIM // 4
D_FF = 5632

NA_W = NA_HEADS * HEAD_DIM
MLA_QK_DIM = MLA_NOPE_DIM + MLA_ROPE_DIM
MLA_SLOT = 256
DIFF_W = DIFF_HEADS * 2 * DIFF_QK_DIM
DIFF_VW = DIFF_HEADS * DIFF_V_DIM
MLA_VW = MLA_HEADS * MLA_V_DIM
VT_ROWS = 128 + 16

C_NAQ = 0
C_NAK = C_NAQ + NA_W
C_NAV = C_NAK + NA_W
C_CQ = C_NAV + NA_W
C_CKV = C_CQ + MLA_Q_RANK
C_DQ = C_CKV + MLA_KV_RANK
C_DK = C_DQ + DIFF_W
C_DV = C_DK + DIFF_W
C_KPE = C_DV + DIFF_VW
IN_COLS_PAD = C_KPE + 128

LANES = 128
FF_TILE = 512
VMEM_LIMIT = 56 * 1024 * 1024
VMEM_LIMIT_FFN = 60 * 1024 * 1024


def _cparams(sem, limit=VMEM_LIMIT):
    return pltpu.CompilerParams(dimension_semantics=sem, vmem_limit_bytes=limit)


def _rms(x, g, n):
    ms = jnp.sum(x * x, axis=-1, keepdims=True) * (1.0 / n)
    return x * lax.rsqrt(ms + EPS) * g


def _proj_prep_kernel(x_ref, mix_g, w_ref, rope_m_ref, rope_d_ref, naq_g, nak_g, qa_g, wqb_ref, kva_g, wkvb_ref,
                      qn_g, qpe_g, kn_g, kpe_g, dq_g, dk_g,
                      naq_o, nak_o, navT_o, mq_o, mk_o, mvT_o, dq_o, dk_o, dvT_o):
    tm = x_ref.shape[0]
    xn = _rms(x_ref[...], mix_g[...], x_ref.shape[-1]).astype(BF16)

    def proj(col, width):
        return jnp.dot(xn, w_ref[:, col:col + width], preferred_element_type=F32)

    lane = lax.broadcasted_iota(jnp.int32, (tm, LANES), 1)
    lo = lane < 64
    ones_row = (lax.broadcasted_iota(jnp.int32, (VT_ROWS - 128, tm), 0) == 0).astype(BF16)

    na_scale = HEAD_DIM ** -0.5 * LOG2E
    na_q, na_k, na_v = proj(C_NAQ, NA_W), proj(C_NAK, NA_W), proj(C_NAV, NA_W)
    for h in range(NA_HEADS):
        sl = slice(h * HEAD_DIM, (h + 1) * HEAD_DIM)
        naq_o[:, sl] = (_rms(na_q[:, sl], naq_g[...], HEAD_DIM) * na_scale).astype(BF16)
        nak_o[:, sl] = _rms(na_k[:, sl], nak_g[...], HEAD_DIM).astype(BF16)
        navT_o[h * VT_ROWS:h * VT_ROWS + HEAD_DIM, :] = na_v[:, sl].T.astype(BF16)
        navT_o[h * VT_ROWS + HEAD_DIM:(h + 1) * VT_ROWS, :] = ones_row

    cm, sam, sbm = rope_m_ref[:, 0:128], rope_m_ref[:, 128:256], rope_m_ref[:, 256:384]

    def rope_m(y):
        return y * cm + pltpu.roll(y, 96, 1) * sam + pltpu.roll(y, 32, 1) * sbm

    mla_scale = MLA_QK_DIM ** -0.5 * LOG2E
    cq = _rms(proj(C_CQ, MLA_Q_RANK), qa_g[...], MLA_Q_RANK).astype(BF16)
    qm = jnp.dot(cq, wqb_ref[...], preferred_element_type=F32)
    for h in range(MLA_HEADS):
        b = h * MLA_SLOT
        nope = _rms(qm[:, b:b + 128], qn_g[...], MLA_NOPE_DIM)
        pe = rope_m(_rms(qm[:, b + 128:b + 256], qpe_g[...], MLA_ROPE_DIM))
        mq_o[:, b:b + 128] = (nope * mla_scale).astype(BF16)
        mq_o[:, b + 128:b + 256] = (pe * mla_scale).astype(BF16)
    ckv = _rms(proj(C_CKV, MLA_KV_RANK), kva_g[...], MLA_KV_RANK).astype(BF16)
    kv = jnp.dot(ckv, wkvb_ref[...], preferred_element_type=F32)
    kpe = rope_m(_rms(proj(C_KPE, 128), kpe_g[...], MLA_ROPE_DIM))
    kpe = kpe.astype(BF16)
    for h in range(MLA_HEADS):
        b = h * MLA_SLOT
        kn = _rms(kv[:, b:b + 128], kn_g[...], MLA_NOPE_DIM)
        mk_o[:, b:b + 128] = kn.astype(BF16)
        mk_o[:, b + 128:b + 256] = kpe
        mvT_o[h * VT_ROWS:h * VT_ROWS + MLA_V_DIM, :] = kv[:, b + 128:b + 256].T.astype(BF16)
        mvT_o[h * VT_ROWS + MLA_V_DIM:(h + 1) * VT_ROWS, :] = ones_row

    cd, sad, sbd = rope_d_ref[:, 0:128], rope_d_ref[:, 128:256], rope_d_ref[:, 256:384]

    def rope_d(y):
        return y * cd + pltpu.roll(y, 120, 1) * sad + pltpu.roll(y, 8, 1) * sbd

    def group_rms(x, g):
        x2 = x * x
        s_lo = jnp.sum(jnp.where(lo, x2, 0.0), axis=-1, keepdims=True)
        s_hi = jnp.sum(jnp.where(lo, 0.0, x2), axis=-1, keepdims=True)
        ms = jnp.where(lo, s_lo, s_hi) * (1.0 / DIFF_QK_DIM)
        return x * lax.rsqrt(ms + EPS) * g

    def split_components(y):
        return jnp.where(lo, y, 0.0), jnp.where(lo, pltpu.roll(y, 64, 1), 0.0)

    df_scale = DIFF_QK_DIM ** -0.5 * LOG2E
    df_q, df_k, df_v = proj(C_DQ, DIFF_W), proj(C_DK, DIFF_W), proj(C_DV, DIFF_VW)
    for h in range(DIFF_HEADS):
        q = rope_d(group_rms(df_q[:, h * 128:(h + 1) * 128], dq_g[...])) * df_scale
        q0, q1 = split_components(q)
        dq_o[:, (2 * h) * 128:(2 * h + 1) * 128] = q0.astype(BF16)
        dq_o[:, (2 * h + 1) * 128:(2 * h + 2) * 128] = q1.astype(BF16)
        k = rope_d(group_rms(df_k[:, h * 128:(h + 1) * 128], dk_g[...]))
        k0, k1 = split_components(k)
        dk_o[:, (2 * h) * 128:(2 * h + 1) * 128] = k0.astype(BF16)
        dk_o[:, (2 * h + 1) * 128:(2 * h + 2) * 128] = k1.astype(BF16)
        v = df_v[:, h * DIFF_V_DIM:(h + 1) * DIFF_V_DIM]
        dvT_o[h * VT_ROWS:h * VT_ROWS + DIFF_V_DIM, :] = v.T.astype(BF16)
        dvT_o[h * VT_ROWS + DIFF_V_DIM:(h + 1) * VT_ROWS, :] = ones_row


def proj_prep(x, rope_m, rope_d, lw, *, seq_len, tm):
    M, D = x.shape
    nt = seq_len // tm
    row = lambda w: pl.BlockSpec((tm, w), lambda i: (i, 0))
    colT = lambda h: pl.BlockSpec((h, tm), lambda i: (0, i))
    full = lambda a: pl.BlockSpec(a.shape, lambda i: (0,) * a.ndim)
    rope = pl.BlockSpec((tm, 384), lambda i: (i % nt, 0))
    w_in = pl.BlockSpec(lw["w_in"].shape, lambda i: (0, 0), pipeline_mode=pl.Buffered(1))
    params = (lw["naq_g"], lw["nak_g"], lw["qa_g"], lw["wqb"], lw["kva_g"], lw["wkvb"],
              lw["qn_g"], lw["qpe_g"], lw["kn_g"], lw["kpe_g"], lw["dq_g"], lw["dk_g"])
    out_shapes = (
        jax.ShapeDtypeStruct((M, NA_W), BF16), jax.ShapeDtypeStruct((M, NA_W), BF16),
        jax.ShapeDtypeStruct((NA_HEADS * VT_ROWS, M), BF16),
        jax.ShapeDtypeStruct((M, MLA_HEADS * MLA_SLOT), BF16),
        jax.ShapeDtypeStruct((M, MLA_HEADS * MLA_SLOT), BF16),
        jax.ShapeDtypeStruct((MLA_HEADS * VT_ROWS, M), BF16),
        jax.ShapeDtypeStruct((M, 2 * DIFF_HEADS * 128), BF16),
        jax.ShapeDtypeStruct((M, 2 * DIFF_HEADS * 128), BF16),
        jax.ShapeDtypeStruct((DIFF_HEADS * VT_ROWS, M), BF16),
    )
    out_specs = (row(NA_W), row(NA_W), colT(NA_HEADS * VT_ROWS), row(MLA_HEADS * MLA_SLOT), row(MLA_HEADS * MLA_SLOT),
                 colT(MLA_HEADS * VT_ROWS), row(2 * DIFF_HEADS * 128), row(2 * DIFF_HEADS * 128),
                 colT(DIFF_HEADS * VT_ROWS))
    return pl.pallas_call(
        _proj_prep_kernel,
        grid=(M // tm,),
        in_specs=[row(D), full(lw["mix_g"]), w_in, rope, rope] + [full(a) for a in params],
        out_specs=out_specs,
        out_shape=out_shapes,
        compiler_params=_cparams(("parallel",)),
        name="proj_prep",
    )(x, lw["mix_g"], lw["w_in"], rope_m, rope_d, *params)


def _flash_kernel(q_ref, k_ref, vT_ref, o_ref, sa_ref, sb_ref, acc_ref, *, tq, tk):
    seq_len = q_ref.shape[0]
    dv = o_ref.shape[1]
    nk = seq_len // tk
    total = (seq_len // tq) * nk
    sub = 16

    def scores(f):
        qoff = pl.multiple_of((f // nk) * tq, tq)
        koff = pl.multiple_of((f % nk) * tk, tk)
        s = lax.dot_general(k_ref[pl.ds(koff, tk), :], q_ref[pl.ds(qoff, tq), :],
                            (((1,), (1,)), ((), ())), preferred_element_type=F32)
        return s.astype(BF16)

    def softmax_pv(s_ref, c, m):
        s = s_ref[...]
        part = jnp.max(s.reshape(tk // sub, sub, tq), axis=0).astype(F32)
        m_new = jnp.maximum(m, jnp.max(part, axis=0, keepdims=True))
        alpha = jnp.exp2(m - m_new)
        p = jnp.exp2(s - m_new.astype(BF16))
        koff = pl.multiple_of(c * tk, tk)
        pv = jnp.dot(vT_ref[:, pl.ds(koff, tk)], p, preferred_element_type=F32)
        acc_ref[...] = alpha * acc_ref[...] + pv
        return m_new

    def write_block(qb):
        qoff = pl.multiple_of(qb * tq, tq)
        out = acc_ref[0:dv, :] / acc_ref[dv:dv + 1, :]
        o_ref[pl.ds(qoff, tq), :] = out.T.astype(o_ref.dtype)

    def pair(i, m):
        f = 2 * i
        if nk == 1:
            sb_ref[...] = scores(f + 1)
            softmax_pv(sa_ref, 0, m)
            write_block(f)
            sa_ref[...] = scores(jnp.minimum(f + 2, total - 1))
            softmax_pv(sb_ref, 0, m)
            write_block(f + 1)
            return m
        c = f % nk
        m = jnp.where(c == 0, NEG_BIG, m)
        sb_ref[...] = scores(f + 1)
        m = softmax_pv(sa_ref, c, m)
        sa_ref[...] = scores(jnp.minimum(f + 2, total - 1))
        m = softmax_pv(sb_ref, c + 1, m)

        @pl.when(c + 2 == nk)
        def _():
            write_block(f // nk)

        return m

    acc_ref[...] = jnp.zeros_like(acc_ref)
    sa_ref[...] = scores(0)
    lax.fori_loop(0, total // 2, pair, jnp.full((1, tq), NEG_BIG, F32))


def flash_attention(q, k, vT, *, batch, seq_len, n_heads, dq, v_of_head, tq, tk, out_dtype):
    M = q.shape[0]
    dv = 128
    nk = seq_len // tk
    assert seq_len % tk == 0 and seq_len % tq == 0 and (nk % 2 == 0 or (nk == 1 and (seq_len // tq) % 2 == 0))
    once = pl.Buffered(1 if seq_len * dq * 2 > 2 * 1024 * 1024 else 2)
    return pl.pallas_call(
        functools.partial(_flash_kernel, tq=tq, tk=tk),
        grid=(batch, n_heads),
        in_specs=[pl.BlockSpec((seq_len, dq), lambda b, h: (b, h), pipeline_mode=once),
                  pl.BlockSpec((seq_len, dq), lambda b, h: (b, h), pipeline_mode=once),
                  pl.BlockSpec((VT_ROWS, seq_len), lambda b, h: (v_of_head(h), b), pipeline_mode=once)],
        out_specs=pl.BlockSpec((seq_len, dv), lambda b, h: (b, h)),
        out_shape=jax.ShapeDtypeStruct((M, n_heads * dv), out_dtype),
        scratch_shapes=[pltpu.VMEM((tk, tq), BF16), pltpu.VMEM((tk, tq), BF16), pltpu.VMEM((VT_ROWS, tq), F32)],
        compiler_params=_cparams(("parallel", "parallel")),
        name="flash_attention",
    )(q, k, vT)


NA_QROWS = 8
NA_KROWS = 2 * NA_WIN_ROWS


NA_HEADS_PER_STEP = 3


def _na_kernel(q_ref, k_ref, vT_ref, bias_ref, o_ref, *, rows):
    i = pl.program_id(2)
    ws = jnp.clip(i * NA_QROWS - NA_WIN_ROWS // 2, 0, rows - NA_KROWS)
    ks = pl.multiple_of(ws * GRID_W, (NA_WIN_ROWS // 2) * GRID_W)
    win = NA_KROWS * GRID_W
    dv = HEAD_DIM
    sub = 16
    for j in range(NA_HEADS_PER_STEP):
        cols = slice(j * dv, (j + 1) * dv)
        s = lax.dot_general(k_ref[pl.ds(ks, win), cols], q_ref[:, cols], (((1,), (1,)), ((), ())),
                            preferred_element_type=F32)
        s = (s + bias_ref[j]).astype(BF16)
        part = jnp.max(s.reshape(win // sub, sub, s.shape[1]), axis=0).astype(F32)
        m = jnp.max(part, axis=0, keepdims=True)
        p = jnp.exp2(s - m.astype(BF16))
        acc = jnp.dot(vT_ref[j * VT_ROWS:(j + 1) * VT_ROWS, pl.ds(ks, win)], p,
                      preferred_element_type=F32)
        o_ref[:, cols] = (acc[0:dv, :] / acc[dv:dv + 1, :]).T.astype(o_ref.dtype)


def na_attention(q, k, vT, bias, *, batch, seq_len):
    M = q.shape[0]
    rows = seq_len // GRID_W
    assert rows >= NA_KROWS and rows % NA_QROWS == 0
    nblk = rows // NA_QROWS
    tq = NA_QROWS * GRID_W
    pattern = lambda i: jnp.where(i == 0, 0, jnp.where(i == nblk - 1, 2, 1))
    hs = NA_HEADS_PER_STEP
    once = pl.Buffered(1)
    return pl.pallas_call(
        functools.partial(_na_kernel, rows=rows),
        grid=(batch, NA_HEADS // hs, nblk),
        in_specs=[pl.BlockSpec((tq, hs * HEAD_DIM), lambda b, h, i: (b * nblk + i, h)),
                  pl.BlockSpec((seq_len, hs * HEAD_DIM), lambda b, h, i: (b, h), pipeline_mode=once),
                  pl.BlockSpec((hs * VT_ROWS, seq_len), lambda b, h, i: (h, b), pipeline_mode=once),
                  pl.BlockSpec((hs, None, NA_KROWS * GRID_W, tq), lambda b, h, i: (h, pattern(i), 0, 0))],
        out_specs=pl.BlockSpec((tq, hs * HEAD_DIM), lambda b, h, i: (b * nblk + i, h)),
        out_shape=jax.ShapeDtypeStruct((M, NA_W), BF16),
        compiler_params=_cparams(("parallel", "parallel", "arbitrary")),
        name="na_attention",
    )(q, k, vT, bias)


def _out_proj_kernel(x_ref, ona_ref, omla_ref, odf_ref, subln_ref, lq1, lk1, lq2, lk2,
                     w_na, w_mla, w_df, ffn_g, o_ref, h_ref, *, lam_init):
    lam = (jnp.exp(jnp.sum(lq1[...] * lk1[...], axis=-1, keepdims=True))
           - jnp.exp(jnp.sum(lq2[...] * lk2[...], axis=-1, keepdims=True)) + lam_init)
    heads = []
    for h in range(DIFF_HEADS):
        o1 = odf_ref[:, (2 * h) * 128:(2 * h + 1) * 128]
        o2 = odf_ref[:, (2 * h + 1) * 128:(2 * h + 2) * 128]
        o = _rms(o1 - lam * o2, subln_ref[...], DIFF_V_DIM) * (1.0 - lam_init)
        heads.append(o.astype(BF16))
    odf = jnp.concatenate(heads, axis=-1)
    y = jnp.dot(ona_ref[...], w_na[...], preferred_element_type=F32)
    y = y + jnp.dot(omla_ref[...], w_mla[...], preferred_element_type=F32)
    y = y + jnp.dot(odf, w_df[...], preferred_element_type=F32)
    x1 = x_ref[...] + y
    o_ref[...] = x1
    h_ref[...] = _rms(x1, ffn_g[...], x1.shape[-1]).astype(BF16)


def out_proj(x, o_na, o_mla, o_df, lw, *, lam_init, tm):
    M, D = x.shape
    row = lambda w: pl.BlockSpec((tm, w), lambda i: (i, 0))
    full = lambda a: pl.BlockSpec(a.shape, lambda i: (0,) * a.ndim)
    params = (lw["subln_g"], lw["lq1"], lw["lk1"], lw["lq2"], lw["lk2"], lw["w_out_na"], lw["w_out_mla"],
              lw["w_out_df"], lw["ffn_g"])
    return pl.pallas_call(
        functools.partial(_out_proj_kernel, lam_init=lam_init),
        grid=(M // tm,),
        in_specs=[row(D), row(NA_W), row(MLA_VW), row(2 * DIFF_HEADS * 128)] + [full(a) for a in params],
        out_specs=(row(D), row(D)),
        out_shape=(jax.ShapeDtypeStruct((M, D), F32), jax.ShapeDtypeStruct((M, D), BF16)),
        compiler_params=_cparams(("parallel",)),
        name="out_proj",
    )(x, o_na, o_mla, o_df, *params)


HALO = 16


def _ffn_kernel(x_ref, h_ref, hp_ref, hn_ref, wg_ref, wu_ref, cw_ref, cb_ref, wd_ref,
                p_ref, pg_ref, wpg_ref, wpp_ref, o_ref, xs_ref, acc_ref, *, seq_len):
    i = pl.program_id(0)
    f = pl.program_id(1)
    tm = x_ref.shape[0]

    @pl.when(f == 0)
    def _():
        row0 = i * tm
        at_start = (row0 % seq_len) == 0
        at_end = ((row0 + tm) % seq_len) == 0
        prev, nxt = hp_ref[...], hn_ref[...]
        xs_ref[0:HALO, :] = jnp.where(at_start, jnp.zeros_like(prev), prev)
        xs_ref[HALO:HALO + tm, :] = h_ref[...]
        xs_ref[HALO + tm:2 * HALO + tm, :] = jnp.where(at_end, jnp.zeros_like(nxt), nxt)
        acc_ref[...] = jnp.zeros_like(acc_ref)

    n_ext = tm + 2 * HALO
    gate = jnp.dot(xs_ref[...], wg_ref[...], preferred_element_type=F32)
    g_prev = pltpu.roll(gate, 1, 0)[HALO:HALO + tm, :]
    g_next = pltpu.roll(gate, n_ext - 1, 0)[HALO:HALO + tm, :]
    g_mid = gate[HALO:HALO + tm, :]
    g = g_prev * cw_ref[0:1, :] + g_mid * cw_ref[1:2, :] + g_next * cw_ref[2:3, :] + cb_ref[...]
    u = jnp.dot(xs_ref[HALO:HALO + tm, :], wu_ref[...], preferred_element_type=F32)
    a = (g * jax.nn.sigmoid(g) * u).astype(BF16)
    acc_ref[...] += jnp.dot(a, wd_ref[...], preferred_element_type=F32)

    @pl.when(f == pl.num_programs(1) - 1)
    def _():
        y = x_ref[...] + acc_ref[...]
        yn = _rms(y, pg_ref[...], y.shape[-1]).astype(BF16)
        gate = jax.nn.sigmoid(jnp.dot(yn, wpg_ref[...], preferred_element_type=F32))
        emb = jnp.dot(p_ref[...].astype(BF16), wpp_ref[...], preferred_element_type=F32)
        o_ref[...] = y + gate * emb


def ffn_ple(x, h, p, lw, *, seq_len, tm):
    M, D = x.shape
    tf = FF_TILE
    nf = lw["w_gate"].shape[1] // tf
    assert seq_len % tm == 0 and tm % HALO == 0
    hb = tm // HALO
    last = M // HALO - 1
    return pl.pallas_call(
        functools.partial(_ffn_kernel, seq_len=seq_len),
        grid=(M // tm, nf),
        in_specs=[pl.BlockSpec((tm, D), lambda i, f: (i, 0)),
                  pl.BlockSpec((tm, D), lambda i, f: (i, 0)),
                  pl.BlockSpec((HALO, D), lambda i, f: (jnp.maximum(i * hb - 1, 0), 0)),
                  pl.BlockSpec((HALO, D), lambda i, f: (jnp.minimum((i + 1) * hb, last), 0)),
                  pl.BlockSpec((D, tf), lambda i, f: (0, f)),
                  pl.BlockSpec((D, tf), lambda i, f: (0, f)),
                  pl.BlockSpec((3, tf), lambda i, f: (0, f)),
                  pl.BlockSpec((1, tf), lambda i, f: (0, f)),
                  pl.BlockSpec((tf, D), lambda i, f: (f, 0)),
                  pl.BlockSpec((tm, p.shape[1]), lambda i, f: (i, 0)),
                  pl.BlockSpec((1, D), lambda i, f: (0, 0)),
                  pl.BlockSpec((D, D), lambda i, f: (0, 0), pipeline_mode=pl.Buffered(1)),
                  pl.BlockSpec((p.shape[1], D), lambda i, f: (0, 0), pipeline_mode=pl.Buffered(1))],
        out_specs=pl.BlockSpec((tm, D), lambda i, f: (i, 0)),
        out_shape=jax.ShapeDtypeStruct((M, D), F32),
        scratch_shapes=[pltpu.VMEM((tm + 2 * HALO, D), BF16), pltpu.VMEM((tm, D), F32)],
        compiler_params=_cparams(("parallel", "arbitrary"), VMEM_LIMIT_FFN),
        name="ffn_ple",
    )(x, h, h, h, lw["w_gate"], lw["w_up"], lw["conv_w"], lw["conv_b"], lw["w_down"],
      p, lw["ple_g"], lw["w_ple_gate"], lw["w_ple_proj"])


def _rope_tables(seq_len):
    def angles(dim, theta):
        inv = 1.0 / (theta ** (jnp.arange(0, dim, 2, dtype=F32) / dim))
        ang = jnp.arange(seq_len, dtype=F32)[:, None] * inv[None, :]
        return jnp.cos(ang), jnp.sin(ang)

    z = lambda w: jnp.zeros((seq_len, w), F32)
    cos, sin = angles(MLA_ROPE_DIM, MLA_ROPE_THETA)
    half = MLA_ROPE_DIM // 2
    cm = jnp.concatenate([cos, cos, z(64)], axis=1)
    sam = jnp.concatenate([-sin, z(half), z(64)], axis=1)
    sbm = jnp.concatenate([z(half), sin, z(64)], axis=1)
    rope_m = jnp.concatenate([cm, sam, sbm], axis=1)

    cos, sin = angles(ROPE_PART_DIM, ROPE_THETA)
    half = ROPE_PART_DIM // 2
    rest = DIFF_QK_DIM - ROPE_PART_DIM
    c64 = jnp.concatenate([cos, cos, jnp.ones((seq_len, rest), F32)], axis=1)
    sa64 = jnp.concatenate([-sin, z(half), z(rest)], axis=1)
    sb64 = jnp.concatenate([z(half), sin, z(rest)], axis=1)
    rope_d = jnp.concatenate([c64, c64, sa64, sa64, sb64, sb64], axis=1)
    return rope_m, rope_d


def _na_bias_table(rpb):
    cols = jnp.arange(GRID_W)
    cs = jnp.clip(cols - NA_WIN_COLS // 2, 0, GRID_W - NA_WIN_COLS)
    kc = jnp.arange(GRID_W)
    valid = (kc[None, :] >= cs[:, None]) & (kc[None, :] < cs[:, None] + NA_WIN_COLS)
    nv = 2 * NA_WIN_COLS - 1
    onehot = (kc[None, None, :] - cols[None, :, None] + (NA_WIN_COLS - 1) == jnp.arange(nv)[:, None, None])
    toep = jnp.einsum("huv,vck->huck", rpb.astype(F32), onehot.astype(F32), precision=lax.Precision.HIGHEST)
    toep = jnp.where(valid[None, None], toep * LOG2E, NEG_BIG)
    n_h = rpb.shape[0]
    masked = 2 * NA_WIN_ROWS - 1
    toep = jnp.concatenate([toep, jnp.full((n_h, 1, GRID_W, GRID_W), NEG_BIG, F32)], axis=1)
    half = NA_WIN_ROWS // 2
    tile = np.full((3, NA_QROWS, NA_KROWS), masked, np.int32)
    for pat, delta in enumerate((0, -half, -NA_WIN_ROWS)):
        for a in range(NA_QROWS):
            first = (max(a - half, 0), a, min(a + half, NA_WIN_ROWS))[pat]
            for i in range(first, first + NA_WIN_ROWS):
                tile[pat, a, i] = delta + i - a + (NA_WIN_ROWS - 1)
    tab = jnp.take(toep, jnp.asarray(tile.reshape(-1)), axis=1)
    tab = tab.reshape(n_h, 3, NA_QROWS, NA_KROWS, GRID_W, GRID_W).transpose(0, 1, 3, 5, 2, 4)
    return tab.reshape(n_h, 3, NA_KROWS * GRID_W, NA_QROWS * GRID_W)


def _layer_params(i, norm_mix, w_in, na_q_norm, na_k_norm, na_rpb, mla_q_a_norm, mla_w_q_b, mla_kv_a_norm,
                  mla_w_kv_b, mla_q_nope_norm, mla_q_pe_norm, mla_k_nope_norm, mla_k_pe_norm,
                  diff_q_norm, diff_k_norm, diff_lambda_q1, diff_lambda_k1, diff_lambda_q2, diff_lambda_k2,
                  diff_subln, w_out, norm_ffn, w_gate, w_up, conv_w, conv_b, w_down,
                  ple_norm, w_ple_gate, w_ple_proj):
    r = lambda a: a[i].reshape(1, -1).astype(F32)
    pad_to = lambda a, n: jnp.pad(a, ((0, 0), (0, n - a.shape[1])))
    w = w_in[i]
    kpe0 = 3 * NA_W + MLA_Q_RANK + MLA_KV_RANK
    w_perm = jnp.concatenate([w[:, :kpe0], w[:, kpe0 + MLA_ROPE_DIM:], w[:, kpe0:kpe0 + MLA_ROPE_DIM],
                              jnp.zeros((w.shape[0], 128 - MLA_ROPE_DIM), w.dtype)], axis=1).astype(BF16)
    wqb = mla_w_q_b[i].reshape(MLA_Q_RANK, MLA_HEADS, MLA_QK_DIM)
    wqb = jnp.pad(wqb, ((0, 0), (0, 0), (0, MLA_SLOT - MLA_QK_DIM))).reshape(MLA_Q_RANK, MLA_HEADS * MLA_SLOT)
    wo = w_out[i].astype(BF16)
    return dict(
        mix_g=r(norm_mix), w_in=w_perm,
        naq_g=r(na_q_norm), nak_g=r(na_k_norm), na_bias=_na_bias_table(na_rpb[i]),
        qa_g=r(mla_q_a_norm), wqb=wqb.astype(BF16), kva_g=r(mla_kv_a_norm), wkvb=mla_w_kv_b[i].astype(BF16),
        qn_g=r(mla_q_nope_norm), qpe_g=pad_to(r(mla_q_pe_norm), 128),
        kn_g=r(mla_k_nope_norm), kpe_g=pad_to(r(mla_k_pe_norm), 128),
        dq_g=jnp.tile(r(diff_q_norm), (1, 2)), dk_g=jnp.tile(r(diff_k_norm), (1, 2)),
        lq1=r(diff_lambda_q1), lk1=r(diff_lambda_k1), lq2=r(diff_lambda_q2), lk2=r(diff_lambda_k2),
        subln_g=r(diff_subln),
        w_out_na=wo[:NA_W], w_out_mla=wo[NA_W:NA_W + MLA_VW], w_out_df=wo[NA_W + MLA_VW:],
        ffn_g=r(norm_ffn), w_gate=w_gate[i].astype(BF16), w_up=w_up[i].astype(BF16),
        conv_w=conv_w[i].astype(F32), conv_b=r(conv_b), w_down=w_down[i].astype(BF16),
        ple_g=r(ple_norm), w_ple_gate=w_ple_gate[i].astype(BF16), w_ple_proj=w_ple_proj[i].astype(BF16),
    )


def _tile(n, pref):
    t = min(pref, n)
    while n % t:
        t //= 2
    return t


def _encoder_layer(x, p_l, lw, rope_m, rope_d, *, layer_idx, batch, seq_len):
    tm = _tile(seq_len, 512)
    (na_q, na_k, na_vT, m_q, m_k, m_vT, d_q, d_k, d_vT) = proj_prep(
        x, rope_m, rope_d, lw, seq_len=seq_len, tm=_tile(seq_len, 256))
    o_na = na_attention(na_q, na_k, na_vT, lw["na_bias"], batch=batch, seq_len=seq_len)
    tq = _tile(seq_len, 512)
    tk = seq_len if seq_len <= 8192 else 8192
    o_mla = flash_attention(m_q, m_k, m_vT, batch=batch, seq_len=seq_len, n_heads=MLA_HEADS, dq=MLA_SLOT,
                            v_of_head=lambda h: h, tq=tq, tk=tk, out_dtype=BF16)
    o_df = flash_attention(d_q, d_k, d_vT, batch=batch, seq_len=seq_len, n_heads=2 * DIFF_HEADS, dq=128,
                           v_of_head=lambda h: h // 2, tq=tq, tk=tk, out_dtype=F32)
    lam_init = 0.8 - 0.6 * math.exp(-0.3 * layer_idx)
    x, h = out_proj(x, o_na, o_mla, o_df, lw, lam_init=lam_init, tm=tm)
    return ffn_ple(x, h, p_l, lw, seq_len=seq_len, tm=tm)


def kernel(x_prompt, x_sample, p_prompt, p_sample, norm_mix, w_in, na_q_norm, na_k_norm, na_rpb, mla_q_a_norm, mla_w_q_b, mla_kv_a_norm, mla_w_kv_b, mla_q_nope_norm, mla_q_pe_norm, mla_k_nope_norm, mla_k_pe_norm, diff_q_norm, diff_k_norm, diff_lambda_q1, diff_lambda_k1, diff_lambda_q2, diff_lambda_k2, diff_subln, w_out, norm_ffn, w_gate, w_up, conv_w, conv_b, w_down, ple_norm, w_ple_gate, w_ple_proj):
    weights = (norm_mix, w_in, na_q_norm, na_k_norm, na_rpb, mla_q_a_norm, mla_w_q_b, mla_kv_a_norm, mla_w_kv_b,
               mla_q_nope_norm, mla_q_pe_norm, mla_k_nope_norm, mla_k_pe_norm, diff_q_norm, diff_k_norm,
               diff_lambda_q1, diff_lambda_k1, diff_lambda_q2, diff_lambda_k2, diff_subln, w_out,
               norm_ffn, w_gate, w_up, conv_w, conv_b, w_down, ple_norm, w_ple_gate, w_ple_proj)
    depth = norm_mix.shape[0]
    groups = []
    for x, p in ((x_prompt, p_prompt), (x_sample, p_sample)):
        b, t, d = x.shape
        groups.append(dict(x=x.reshape(b * t, d), p=p.reshape(depth, b * t, p.shape[-1]), batch=b, seq_len=t,
                           rope=_rope_tables(t), shape=x.shape))
    for i in range(depth):
        lw = _layer_params(i, *weights)
        for g in groups:
            g["x"] = _encoder_layer(g["x"], g["p"][i], lw, *g["rope"], layer_idx=i,
                                    batch=g["batch"], seq_len=g["seq_len"])
    return tuple(g["x"].reshape(g["shape"]) for g in groups)
```

```python
import functools
import math

import jax
import jax.numpy as jnp
import numpy as np
from jax import lax
from jax.experimental import pallas as pl
from jax.experimental.pallas import tpu as pltpu

F32 = jnp.float32
BF16 = jnp.bfloat16

EPS = 1e-6
LOG2E = 1.4426950408889634
NEG_BIG = -1e30

D_MODEL = 2048
PLE_DIM = 256
GRID_W = 64
HEAD_DIM = 128
NA_HEADS = 6
NA_WIN_ROWS = 8
NA_WIN_COLS = 16
MLA_HEADS = 5
MLA_Q_RANK = 512
MLA_KV_RANK = 256
MLA_NOPE_DIM = 128
MLA_ROPE_DIM = 64
MLA_V_DIM = 128
MLA_ROPE_THETA = 10000.0
DIFF_HEADS = 5
DIFF_QK_DIM = 64
DIFF_V_DIM = 128
ROPE_THETA = 500000.0
ROPE_PART_DIM = DIFF_QK_DIM // 4
D_FF = 5632

NA_W = NA_HEADS * HEAD_DIM
MLA_QK_DIM = MLA_NOPE_DIM + MLA_ROPE_DIM
MLA_SLOT = 256
DIFF_W = DIFF_HEADS * 2 * DIFF_QK_DIM
DIFF_VW = DIFF_HEADS * DIFF_V_DIM
MLA_VW = MLA_HEADS * MLA_V_DIM
VT_ROWS = 128 + 16

C_NAQ = 0
C_NAK = C_NAQ + NA_W
C_NAV = C_NAK + NA_W
C_CQ = C_NAV + NA_W
C_CKV = C_CQ + MLA_Q_RANK
C_DQ = C_CKV + MLA_KV_RANK
C_DK = C_DQ + DIFF_W
C_DV = C_DK + DIFF_W
C_KPE = C_DV + DIFF_VW
IN_COLS_PAD = C_KPE + 128

LANES = 128
FF_TILE = 512
FLASH_MAX_KEYS = 8192
VMEM_LIMIT = 56 * 1024 * 1024
VMEM_LIMIT_FFN = 60 * 1024 * 1024


def _cparams(sem, limit=VMEM_LIMIT):
    return pltpu.CompilerParams(dimension_semantics=sem, vmem_limit_bytes=limit)


def _rms(x, g, n):
    ms = jnp.sum(x * x, axis=-1, keepdims=True) * (1.0 / n)
    return x * lax.rsqrt(ms + EPS) * g


def _proj_prep_kernel(x_ref, mix_g, w_ref, rope_m_ref, rope_d_ref, naq_g, nak_g, qa_g, wqb_ref, kva_g, wkvb_ref,
                      qn_g, qpe_g, kn_g, kpe_g, dq_g, dk_g,
                      naq_o, nak_o, navT_o, mq_o, mk_o, mvT_o, dq_o, dk_o, dvT_o):
    tm = x_ref.shape[0]
    xn = _rms(x_ref[...], mix_g[...], x_ref.shape[-1]).astype(BF16)

    def proj(col, width):
        return jnp.dot(xn, w_ref[:, col:col + width], preferred_element_type=F32)

    lane = lax.broadcasted_iota(jnp.int32, (tm, LANES), 1)
    lo = lane < 64
    ones_row = (lax.broadcasted_iota(jnp.int32, (VT_ROWS - 128, tm), 0) == 0).astype(BF16)

    na_scale = HEAD_DIM ** -0.5 * LOG2E
    na_q, na_k, na_v = proj(C_NAQ, NA_W), proj(C_NAK, NA_W), proj(C_NAV, NA_W)
    for h in range(NA_HEADS):
        sl = slice(h * HEAD_DIM, (h + 1) * HEAD_DIM)
        naq_o[:, sl] = (_rms(na_q[:, sl], naq_g[...], HEAD_DIM) * na_scale).astype(BF16)
        nak_o[:, sl] = _rms(na_k[:, sl], nak_g[...], HEAD_DIM).astype(BF16)
        navT_o[h * VT_ROWS:h * VT_ROWS + HEAD_DIM, :] = na_v[:, sl].T.astype(BF16)
        navT_o[h * VT_ROWS + HEAD_DIM:(h + 1) * VT_ROWS, :] = ones_row

    cm, sam, sbm = rope_m_ref[:, 0:128], rope_m_ref[:, 128:256], rope_m_ref[:, 256:384]

    def rope_m(y):
        return y * cm + pltpu.roll(y, 96, 1) * sam + pltpu.roll(y, 32, 1) * sbm

    mla_scale = MLA_QK_DIM ** -0.5 * LOG2E
    cq = _rms(proj(C_CQ, MLA_Q_RANK), qa_g[...], MLA_Q_RANK).astype(BF16)
    qm = jnp.dot(cq, wqb_ref[...], preferred_element_type=F32)
    for h in range(MLA_HEADS):
        b = h * MLA_SLOT
        nope = _rms(qm[:, b:b + 128], qn_g[...], MLA_NOPE_DIM)
        pe = rope_m(_rms(qm[:, b + 128:b + 256], qpe_g[...], MLA_ROPE_DIM))
        mq_o[:, b:b + 128] = (nope * mla_scale).astype(BF16)
        mq_o[:, b + 128:b + 256] = (pe * mla_scale).astype(BF16)
    ckv = _rms(proj(C_CKV, MLA_KV_RANK), kva_g[...], MLA_KV_RANK).astype(BF16)
    kv = jnp.dot(ckv, wkvb_ref[...], preferred_element_type=F32)
    kpe = rope_m(_rms(proj(C_KPE, 128), kpe_g[...], MLA_ROPE_DIM))
    kpe = kpe.astype(BF16)
    for h in range(MLA_HEADS):
        b = h * MLA_SLOT
        kn = _rms(kv[:, b:b + 128], kn_g[...], MLA_NOPE_DIM)
        mk_o[:, b:b + 128] = kn.astype(BF16)
        mk_o[:, b + 128:b + 256] = kpe
        mvT_o[h * VT_ROWS:h * VT_ROWS + MLA_V_DIM, :] = kv[:, b + 128:b + 256].T.astype(BF16)
        mvT_o[h * VT_ROWS + MLA_V_DIM:(h + 1) * VT_ROWS, :] = ones_row

    cd, sad, sbd = rope_d_ref[:, 0:128], rope_d_ref[:, 128:256], rope_d_ref[:, 256:384]

    def rope_d(y):
        return y * cd + pltpu.roll(y, 120, 1) * sad + pltpu.roll(y, 8, 1) * sbd

    def group_rms(x, g):
        x2 = x * x
        s_lo = jnp.sum(jnp.where(lo, x2, 0.0), axis=-1, keepdims=True)
        s_hi = jnp.sum(jnp.where(lo, 0.0, x2), axis=-1, keepdims=True)
        ms = jnp.where(lo, s_lo, s_hi) * (1.0 / DIFF_QK_DIM)
        return x * lax.rsqrt(ms + EPS) * g

    def split_components(y):
        return jnp.where(lo, y, 0.0), jnp.where(lo, pltpu.roll(y, 64, 1), 0.0)

    df_scale = DIFF_QK_DIM ** -0.5 * LOG2E
    df_q, df_k, df_v = proj(C_DQ, DIFF_W), proj(C_DK, DIFF_W), proj(C_DV, DIFF_VW)
    for h in range(DIFF_HEADS):
        q = rope_d(group_rms(df_q[:, h * 128:(h + 1) * 128], dq_g[...])) * df_scale
        q0, q1 = split_components(q)
        dq_o[:, (2 * h) * 128:(2 * h + 1) * 128] = q0.astype(BF16)
        dq_o[:, (2 * h + 1) * 128:(2 * h + 2) * 128] = q1.astype(BF16)
        k = rope_d(group_rms(df_k[:, h * 128:(h + 1) * 128], dk_g[...]))
        k0, k1 = split_components(k)
        dk_o[:, (2 * h) * 128:(2 * h + 1) * 128] = k0.astype(BF16)
        dk_o[:, (2 * h + 1) * 128:(2 * h + 2) * 128] = k1.astype(BF16)
        v = df_v[:, h * DIFF_V_DIM:(h + 1) * DIFF_V_DIM]
        dvT_o[h * VT_ROWS:h * VT_ROWS + DIFF_V_DIM, :] = v.T.astype(BF16)
        dvT_o[h * VT_ROWS + DIFF_V_DIM:(h + 1) * VT_ROWS, :] = ones_row


def proj_prep(x, rope_m, rope_d, lw, *, seq_len, tm):
    M, D = x.shape
    nt = seq_len // tm
    row = lambda w: pl.BlockSpec((tm, w), lambda i: (i, 0))
    colT = lambda h: pl.BlockSpec((h, tm), lambda i: (0, i))
    full = lambda a: pl.BlockSpec(a.shape, lambda i: (0,) * a.ndim)
    rope = pl.BlockSpec((tm, 384), lambda i: (i % nt, 0))
    w_in = pl.BlockSpec(lw["w_in"].shape, lambda i: (0, 0), pipeline_mode=pl.Buffered(1))
    params = (lw["naq_g"], lw["nak_g"], lw["qa_g"], lw["wqb"], lw["kva_g"], lw["wkvb"],
              lw["qn_g"], lw["qpe_g"], lw["kn_g"], lw["kpe_g"], lw["dq_g"], lw["dk_g"])
    out_shapes = (
        jax.ShapeDtypeStruct((M, NA_W), BF16), jax.ShapeDtypeStruct((M, NA_W), BF16),
        jax.ShapeDtypeStruct((NA_HEADS * VT_ROWS, M), BF16),
        jax.ShapeDtypeStruct((M, MLA_HEADS * MLA_SLOT), BF16),
        jax.ShapeDtypeStruct((M, MLA_HEADS * MLA_SLOT), BF16),
        jax.ShapeDtypeStruct((MLA_HEADS * VT_ROWS, M), BF16),
        jax.ShapeDtypeStruct((M, 2 * DIFF_HEADS * 128), BF16),
        jax.ShapeDtypeStruct((M, 2 * DIFF_HEADS * 128), BF16),
        jax.ShapeDtypeStruct((DIFF_HEADS * VT_ROWS, M), BF16),
    )
    out_specs = (row(NA_W), row(NA_W), colT(NA_HEADS * VT_ROWS), row(MLA_HEADS * MLA_SLOT), row(MLA_HEADS * MLA_SLOT),
                 colT(MLA_HEADS * VT_ROWS), row(2 * DIFF_HEADS * 128), row(2 * DIFF_HEADS * 128),
                 colT(DIFF_HEADS * VT_ROWS))
    return pl.pallas_call(
        _proj_prep_kernel,
        grid=(M // tm,),
        in_specs=[row(D), full(lw["mix_g"]), w_in, rope, rope] + [full(a) for a in params],
        out_specs=out_specs,
        out_shape=out_shapes,
        compiler_params=_cparams(("parallel",)),
        name="proj_prep",
    )(x, lw["mix_g"], lw["w_in"], rope_m, rope_d, *params)


def _flash_kernel(q_ref, k_ref, vT_ref, o_ref, sa_ref, sb_ref, acc_ref, *, tq, tk):
    seq_len = q_ref.shape[0]
    dv = o_ref.shape[1]
    nk = seq_len // tk
    total = (seq_len // tq) * nk
    sub = 16

    def scores(f):
        qoff = pl.multiple_of((f // nk) * tq, tq)
        koff = pl.multiple_of((f % nk) * tk, tk)
        s = lax.dot_general(k_ref[pl.ds(koff, tk), :], q_ref[pl.ds(qoff, tq), :],
                            (((1,), (1,)), ((), ())), preferred_element_type=F32)
        return s.astype(BF16)

    def softmax_pv(s_ref, c, m):
        s = s_ref[...]
        part = jnp.max(s.reshape(tk // sub, sub, tq), axis=0).astype(F32)
        m_new = jnp.maximum(m, jnp.max(part, axis=0, keepdims=True))
        alpha = jnp.exp2(m - m_new)
        p = jnp.exp2(s - m_new.astype(BF16))
        koff = pl.multiple_of(c * tk, tk)
        pv = jnp.dot(vT_ref[:, pl.ds(koff, tk)], p, preferred_element_type=F32)
        acc_ref[...] = alpha * acc_ref[...] + pv
        return m_new

    def write_block(qb):
        qoff = pl.multiple_of(qb * tq, tq)
        out = acc_ref[0:dv, :] / acc_ref[dv:dv + 1, :]
        o_ref[pl.ds(qoff, tq), :] = out.T.astype(o_ref.dtype)

    def pair(i, m):
        f = 2 * i
        if nk == 1:
            sb_ref[...] = scores(f + 1)
            softmax_pv(sa_ref, 0, m)
            write_block(f)
            sa_ref[...] = scores(jnp.minimum(f + 2, total - 1))
            softmax_pv(sb_ref, 0, m)
            write_block(f + 1)
            return m
        if nk == 2:
            sb_ref[...] = scores(f + 1)
            m_first = softmax_pv(sa_ref, 0, m)
            sa_ref[...] = scores(jnp.minimum(f + 2, total - 1))
            softmax_pv(sb_ref, 1, m_first)
            write_block(i)
            return m
        c = f % nk
        m = jnp.where(c == 0, NEG_BIG, m)
        sb_ref[...] = scores(f + 1)
        m = softmax_pv(sa_ref, c, m)
        sa_ref[...] = scores(jnp.minimum(f + 2, total - 1))
        m = softmax_pv(sb_ref, c + 1, m)

        @pl.when(c + 2 == nk)
        def _():
            write_block(f // nk)

        return m

    acc_ref[...] = jnp.zeros_like(acc_ref)
    sa_ref[...] = scores(0)
    lax.fori_loop(0, total // 2, pair, jnp.full((1, tq), NEG_BIG, F32))


def flash_attention(q, k, vT, *, batch, seq_len, n_heads, dq, v_of_head, tq, tk, out_dtype):
    M = q.shape[0]
    dv = 128
    nk = seq_len // tk
    assert seq_len % tk == 0 and seq_len % tq == 0 and (nk % 2 == 0 or (nk == 1 and (seq_len // tq) % 2 == 0))
    once = pl.Buffered(1 if seq_len * dq * 2 > 2 * 1024 * 1024 else 2)
    return pl.pallas_call(
        functools.partial(_flash_kernel, tq=tq, tk=tk),
        grid=(batch, n_heads),
        in_specs=[pl.BlockSpec((seq_len, dq), lambda b, h: (b, h), pipeline_mode=once),
                  pl.BlockSpec((seq_len, dq), lambda b, h: (b, h), pipeline_mode=once),
                  pl.BlockSpec((VT_ROWS, seq_len), lambda b, h: (v_of_head(h), b), pipeline_mode=once)],
        out_specs=pl.BlockSpec((seq_len, dv), lambda b, h: (b, h)),
        out_shape=jax.ShapeDtypeStruct((M, n_heads * dv), out_dtype),
        scratch_shapes=[pltpu.VMEM((tk, tq), BF16), pltpu.VMEM((tk, tq), BF16), pltpu.VMEM((VT_ROWS, tq), F32)],
        compiler_params=_cparams(("parallel", "parallel")),
        name="flash_attention",
    )(q, k, vT)


NA_QROWS = 8
NA_KROWS = 2 * NA_WIN_ROWS


NA_HEADS_PER_STEP = 3


def _na_kernel(q_ref, k_ref, vT_ref, bias_ref, o_ref, *, rows):
    i = pl.program_id(2)
    ws = jnp.clip(i * NA_QROWS - NA_WIN_ROWS // 2, 0, rows - NA_KROWS)
    ks = pl.multiple_of(ws * GRID_W, (NA_WIN_ROWS // 2) * GRID_W)
    win = NA_KROWS * GRID_W
    dv = HEAD_DIM
    sub = 16
    for j in range(NA_HEADS_PER_STEP):
        cols = slice(j * dv, (j + 1) * dv)
        s = lax.dot_general(k_ref[pl.ds(ks, win), cols], q_ref[:, cols], (((1,), (1,)), ((), ())),
                            preferred_element_type=F32)
        s = (s + bias_ref[j]).astype(BF16)
        part = jnp.max(s.reshape(win // sub, sub, s.shape[1]), axis=0).astype(F32)
        m = jnp.max(part, axis=0, keepdims=True)
        p = jnp.exp2(s - m.astype(BF16))
        acc = jnp.dot(vT_ref[j * VT_ROWS:(j + 1) * VT_ROWS, pl.ds(ks, win)], p,
                      preferred_element_type=F32)
        o_ref[:, cols] = (acc[0:dv, :] / acc[dv:dv + 1, :]).T.astype(o_ref.dtype)


def na_attention(q, k, vT, bias, *, batch, seq_len):
    M = q.shape[0]
    rows = seq_len // GRID_W
    assert rows >= NA_KROWS and rows % NA_QROWS == 0
    nblk = rows // NA_QROWS
    tq = NA_QROWS * GRID_W
    pattern = lambda i: jnp.where(i == 0, 0, jnp.where(i == nblk - 1, 2, 1))
    hs = NA_HEADS_PER_STEP
    once = pl.Buffered(1)
    return pl.pallas_call(
        functools.partial(_na_kernel, rows=rows),
        grid=(batch, NA_HEADS // hs, nblk),
        in_specs=[pl.BlockSpec((tq, hs * HEAD_DIM), lambda b, h, i: (b * nblk + i, h)),
                  pl.BlockSpec((seq_len, hs * HEAD_DIM), lambda b, h, i: (b, h), pipeline_mode=once),
                  pl.BlockSpec((hs * VT_ROWS, seq_len), lambda b, h, i: (h, b), pipeline_mode=once),
                  pl.BlockSpec((hs, None, NA_KROWS * GRID_W, tq), lambda b, h, i: (h, pattern(i), 0, 0))],
        out_specs=pl.BlockSpec((tq, hs * HEAD_DIM), lambda b, h, i: (b * nblk + i, h)),
        out_shape=jax.ShapeDtypeStruct((M, NA_W), BF16),
        compiler_params=_cparams(("parallel", "parallel", "arbitrary")),
        name="na_attention",
    )(q, k, vT, bias)


def _out_proj_kernel(x_ref, ona_ref, omla_ref, odf_ref, subln_ref, lq1, lk1, lq2, lk2,
                     w_na, w_mla, w_df, ffn_g, o_ref, h_ref, *, lam_init):
    lam = (jnp.exp(jnp.sum(lq1[...] * lk1[...], axis=-1, keepdims=True))
           - jnp.exp(jnp.sum(lq2[...] * lk2[...], axis=-1, keepdims=True)) + lam_init)
    heads = []
    for h in range(DIFF_HEADS):
        o1 = odf_ref[:, (2 * h) * 128:(2 * h + 1) * 128]
        o2 = odf_ref[:, (2 * h + 1) * 128:(2 * h + 2) * 128]
        o = _rms(o1 - lam * o2, subln_ref[...], DIFF_V_DIM) * (1.0 - lam_init)
        heads.append(o.astype(BF16))
    odf = jnp.concatenate(heads, axis=-1)
    y = jnp.dot(ona_ref[...], w_na[...], preferred_element_type=F32)
    y = y + jnp.dot(omla_ref[...], w_mla[...], preferred_element_type=F32)
    y = y + jnp.dot(odf, w_df[...], preferred_element_type=F32)
    x1 = x_ref[...] + y
    o_ref[...] = x1
    h_ref[...] = _rms(x1, ffn_g[...], x1.shape[-1]).astype(BF16)


def out_proj(x, o_na, o_mla, o_df, lw, *, lam_init, tm):
    M, D = x.shape
    row = lambda w: pl.BlockSpec((tm, w), lambda i: (i, 0))
    full = lambda a: pl.BlockSpec(a.shape, lambda i: (0,) * a.ndim)
    params = (lw["subln_g"], lw["lq1"], lw["lk1"], lw["lq2"], lw["lk2"], lw["w_out_na"], lw["w_out_mla"],
              lw["w_out_df"], lw["ffn_g"])
    return pl.pallas_call(
        functools.partial(_out_proj_kernel, lam_init=lam_init),
        grid=(M // tm,),
        in_specs=[row(D), row(NA_W), row(MLA_VW), row(2 * DIFF_HEADS * 128)] + [full(a) for a in params],
        out_specs=(row(D), row(D)),
        out_shape=(jax.ShapeDtypeStruct((M, D), F32), jax.ShapeDtypeStruct((M, D), BF16)),
        compiler_params=_cparams(("parallel",)),
        name="out_proj",
    )(x, o_na, o_mla, o_df, *params)


HALO = 16


def _ffn_kernel(x_ref, h_ref, hp_ref, hn_ref, wg_ref, wu_ref, cw_ref, cb_ref, wd_ref,
                p_ref, pg_ref, wpg_ref, wpp_ref, o_ref, xs_ref, acc_ref, *, seq_len):
    i = pl.program_id(0)
    f = pl.program_id(1)
    tm = x_ref.shape[0]

    @pl.when(f == 0)
    def _():
        row0 = i * tm
        at_start = (row0 % seq_len) == 0
        at_end = ((row0 + tm) % seq_len) == 0
        prev, nxt = hp_ref[...], hn_ref[...]
        xs_ref[0:HALO, :] = jnp.where(at_start, jnp.zeros_like(prev), prev)
        xs_ref[HALO:HALO + tm, :] = h_ref[...]
        xs_ref[HALO + tm:2 * HALO + tm, :] = jnp.where(at_end, jnp.zeros_like(nxt), nxt)
        acc_ref[...] = jnp.zeros_like(acc_ref)

    n_ext = tm + 2 * HALO
    gate = jnp.dot(xs_ref[...], wg_ref[...], preferred_element_type=F32)
    g_prev = pltpu.roll(gate, 1, 0)[HALO:HALO + tm, :]
    g_next = pltpu.roll(gate, n_ext - 1, 0)[HALO:HALO + tm, :]
    g_mid = gate[HALO:HALO + tm, :]
    g = g_prev * cw_ref[0:1, :] + g_mid * cw_ref[1:2, :] + g_next * cw_ref[2:3, :] + cb_ref[...]
    u = jnp.dot(xs_ref[HALO:HALO + tm, :], wu_ref[...], preferred_element_type=F32)
    a = (g * jax.nn.sigmoid(g) * u).astype(BF16)
    acc_ref[...] += jnp.dot(a, wd_ref[...], preferred_element_type=F32)

    @pl.when(f == pl.num_programs(1) - 1)
    def _():
        y = x_ref[...] + acc_ref[...]
        yn = _rms(y, pg_ref[...], y.shape[-1]).astype(BF16)
        gate = jax.nn.sigmoid(jnp.dot(yn, wpg_ref[...], preferred_element_type=F32))
        emb = jnp.dot(p_ref[...].astype(BF16), wpp_ref[...], preferred_element_type=F32)
        o_ref[...] = y + gate * emb


def ffn_ple(x, h, p, lw, *, seq_len, tm):
    M, D = x.shape
    tf = FF_TILE
    nf = lw["w_gate"].shape[1] // tf
    assert seq_len % tm == 0 and tm % HALO == 0
    hb = tm // HALO
    last = M // HALO - 1
    return pl.pallas_call(
        functools.partial(_ffn_kernel, seq_len=seq_len),
        grid=(M // tm, nf),
        in_specs=[pl.BlockSpec((tm, D), lambda i, f: (i, 0)),
                  pl.BlockSpec((tm, D), lambda i, f: (i, 0)),
                  pl.BlockSpec((HALO, D), lambda i, f: (jnp.maximum(i * hb - 1, 0), 0)),
                  pl.BlockSpec((HALO, D), lambda i, f: (jnp.minimum((i + 1) * hb, last), 0)),
                  pl.BlockSpec((D, tf), lambda i, f: (0, f)),
                  pl.BlockSpec((D, tf), lambda i, f: (0, f)),
                  pl.BlockSpec((3, tf), lambda i, f: (0, f)),
                  pl.BlockSpec((1, tf), lambda i, f: (0, f)),
                  pl.BlockSpec((tf, D), lambda i, f: (f, 0)),
                  pl.BlockSpec((tm, p.shape[1]), lambda i, f: (i, 0)),
                  pl.BlockSpec((1, D), lambda i, f: (0, 0)),
                  pl.BlockSpec((D, D), lambda i, f: (0, 0), pipeline_mode=pl.Buffered(1)),
                  pl.BlockSpec((p.shape[1], D), lambda i, f: (0, 0), pipeline_mode=pl.Buffered(1))],
        out_specs=pl.BlockSpec((tm, D), lambda i, f: (i, 0)),
        out_shape=jax.ShapeDtypeStruct((M, D), F32),
        scratch_shapes=[pltpu.VMEM((tm + 2 * HALO, D), BF16), pltpu.VMEM((tm, D), F32)],
        compiler_params=_cparams(("parallel", "arbitrary"), VMEM_LIMIT_FFN),
        name="ffn_ple",
    )(x, h, h, h, lw["w_gate"], lw["w_up"], lw["conv_w"], lw["conv_b"], lw["w_down"],
      p, lw["ple_g"], lw["w_ple_gate"], lw["w_ple_proj"])


def _rope_tables(seq_len):
    def angles(dim, theta):
        inv = 1.0 / (theta ** (jnp.arange(0, dim, 2, dtype=F32) / dim))
        ang = jnp.arange(seq_len, dtype=F32)[:, None] * inv[None, :]
        return jnp.cos(ang), jnp.sin(ang)

    z = lambda w: jnp.zeros((seq_len, w), F32)
    cos, sin = angles(MLA_ROPE_DIM, MLA_ROPE_THETA)
    half = MLA_ROPE_DIM // 2
    cm = jnp.concatenate([cos, cos, z(64)], axis=1)
    sam = jnp.concatenate([-sin, z(half), z(64)], axis=1)
    sbm = jnp.concatenate([z(half), sin, z(64)], axis=1)
    rope_m = jnp.concatenate([cm, sam, sbm], axis=1)

    cos, sin = angles(ROPE_PART_DIM, ROPE_THETA)
    half = ROPE_PART_DIM // 2
    rest = DIFF_QK_DIM - ROPE_PART_DIM
    c64 = jnp.concatenate([cos, cos, jnp.ones((seq_len, rest), F32)], axis=1)
    sa64 = jnp.concatenate([-sin, z(half), z(rest)], axis=1)
    sb64 = jnp.concatenate([z(half), sin, z(rest)], axis=1)
    rope_d = jnp.concatenate([c64, c64, sa64, sa64, sb64, sb64], axis=1)
    return rope_m, rope_d


def _na_bias_table(rpb):
    cols = jnp.arange(GRID_W)
    cs = jnp.clip(cols - NA_WIN_COLS // 2, 0, GRID_W - NA_WIN_COLS)
    kc = jnp.arange(GRID_W)
    valid = (kc[None, :] >= cs[:, None]) & (kc[None, :] < cs[:, None] + NA_WIN_COLS)
    nv = 2 * NA_WIN_COLS - 1
    onehot = (kc[None, None, :] - cols[None, :, None] + (NA_WIN_COLS - 1) == jnp.arange(nv)[:, None, None])
    toep = jnp.einsum("huv,vck->huck", rpb.astype(F32), onehot.astype(F32), precision=lax.Precision.HIGHEST)
    toep = jnp.where(valid[None, None], toep * LOG2E, NEG_BIG)
    n_h = rpb.shape[0]
    masked = 2 * NA_WIN_ROWS - 1
    toep = jnp.concatenate([toep, jnp.full((n_h, 1, GRID_W, GRID_W), NEG_BIG, F32)], axis=1)
    half = NA_WIN_ROWS // 2
    tile = np.full((3, NA_QROWS, NA_KROWS), masked, np.int32)
    for pat, delta in enumerate((0, -half, -NA_WIN_ROWS)):
        for a in range(NA_QROWS):
            first = (max(a - half, 0), a, min(a + half, NA_WIN_ROWS))[pat]
            for i in range(first, first + NA_WIN_ROWS):
                tile[pat, a, i] = delta + i - a + (NA_WIN_ROWS - 1)
    tab = jnp.take(toep, jnp.asarray(tile.reshape(-1)), axis=1)
    tab = tab.reshape(n_h, 3, NA_QROWS, NA_KROWS, GRID_W, GRID_W).transpose(0, 1, 3, 5, 2, 4)
    return tab.reshape(n_h, 3, NA_KROWS * GRID_W, NA_QROWS * GRID_W)


def _layer_params(i, norm_mix, w_in, na_q_norm, na_k_norm, na_rpb, mla_q_a_norm, mla_w_q_b, mla_kv_a_norm,
                  mla_w_kv_b, mla_q_nope_norm, mla_q_pe_norm, mla_k_nope_norm, mla_k_pe_norm,
                  diff_q_norm, diff_k_norm, diff_lambda_q1, diff_lambda_k1, diff_lambda_q2, diff_lambda_k2,
                  diff_subln, w_out, norm_ffn, w_gate, w_up, conv_w, conv_b, w_down,
                  ple_norm, w_ple_gate, w_ple_proj):
    r = lambda a: a[i].reshape(1, -1).astype(F32)
    pad_to = lambda a, n: jnp.pad(a, ((0, 0), (0, n - a.shape[1])))
    w = w_in[i]
    kpe0 = 3 * NA_W + MLA_Q_RANK + MLA_KV_RANK
    w_perm = jnp.concatenate([w[:, :kpe0], w[:, kpe0 + MLA_ROPE_DIM:], w[:, kpe0:kpe0 + MLA_ROPE_DIM],
                              jnp.zeros((w.shape[0], 128 - MLA_ROPE_DIM), w.dtype)], axis=1).astype(BF16)
    wqb = mla_w_q_b[i].reshape(MLA_Q_RANK, MLA_HEADS, MLA_QK_DIM)
    wqb = jnp.pad(wqb, ((0, 0), (0, 0), (0, MLA_SLOT - MLA_QK_DIM))).reshape(MLA_Q_RANK, MLA_HEADS * MLA_SLOT)
    wo = w_out[i].astype(BF16)
    return dict(
        mix_g=r(norm_mix), w_in=w_perm,
        naq_g=r(na_q_norm), nak_g=r(na_k_norm), na_bias=_na_bias_table(na_rpb[i]),
        qa_g=r(mla_q_a_norm), wqb=wqb.astype(BF16), kva_g=r(mla_kv_a_norm), wkvb=mla_w_kv_b[i].astype(BF16),
        qn_g=r(mla_q_nope_norm), qpe_g=pad_to(r(mla_q_pe_norm), 128),
        kn_g=r(mla_k_nope_norm), kpe_g=pad_to(r(mla_k_pe_norm), 128),
        dq_g=jnp.tile(r(diff_q_norm), (1, 2)), dk_g=jnp.tile(r(diff_k_norm), (1, 2)),
        lq1=r(diff_lambda_q1), lk1=r(diff_lambda_k1), lq2=r(diff_lambda_q2), lk2=r(diff_lambda_k2),
        subln_g=r(diff_subln),
        w_out_na=wo[:NA_W], w_out_mla=wo[NA_W:NA_W + MLA_VW], w_out_df=wo[NA_W + MLA_VW:],
        ffn_g=r(norm_ffn), w_gate=w_gate[i].astype(BF16), w_up=w_up[i].astype(BF16),
        conv_w=conv_w[i].astype(F32), conv_b=r(conv_b), w_down=w_down[i].astype(BF16),
        ple_g=r(ple_norm), w_ple_gate=w_ple_gate[i].astype(BF16), w_ple_proj=w_ple_proj[i].astype(BF16),
    )


def _tile(n, pref):
    t = min(pref, n)
    while n % t:
        t //= 2
    return t


def _encoder_layer(x, p_l, lw, rope_m, rope_d, *, layer_idx, batch, seq_len):
    tm = _tile(seq_len, 512)
    (na_q, na_k, na_vT, m_q, m_k, m_vT, d_q, d_k, d_vT) = proj_prep(
        x, rope_m, rope_d, lw, seq_len=seq_len, tm=_tile(seq_len, 256))
    o_na = na_attention(na_q, na_k, na_vT, lw["na_bias"], batch=batch, seq_len=seq_len)
    tq = _tile(seq_len, 512)
    tk = min(seq_len, FLASH_MAX_KEYS)
    o_mla = flash_attention(m_q, m_k, m_vT, batch=batch, seq_len=seq_len, n_heads=MLA_HEADS, dq=MLA_SLOT,
                            v_of_head=lambda h: h, tq=tq, tk=tk, out_dtype=BF16)
    o_df = flash_attention(d_q, d_k, d_vT, batch=batch, seq_len=seq_len, n_heads=2 * DIFF_HEADS, dq=128,
                           v_of_head=lambda h: h // 2, tq=tq, tk=tk, out_dtype=F32)
    lam_init = 0.8 - 0.6 * math.exp(-0.3 * layer_idx)
    x, h = out_proj(x, o_na, o_mla, o_df, lw, lam_init=lam_init, tm=tm)
    return ffn_ple(x, h, p_l, lw, seq_len=seq_len, tm=tm)


def kernel(x_prompt, x_sample, p_prompt, p_sample, norm_mix, w_in, na_q_norm, na_k_norm, na_rpb, mla_q_a_norm, mla_w_q_b, mla_kv_a_norm, mla_w_kv_b, mla_q_nope_norm, mla_q_pe_norm, mla_k_nope_norm, mla_k_pe_norm, diff_q_norm, diff_k_norm, diff_lambda_q1, diff_lambda_k1, diff_lambda_q2, diff_lambda_k2, diff_subln, w_out, norm_ffn, w_gate, w_up, conv_w, conv_b, w_down, ple_norm, w_ple_gate, w_ple_proj):
    weights = (norm_mix, w_in, na_q_norm, na_k_norm, na_rpb, mla_q_a_norm, mla_w_q_b, mla_kv_a_norm, mla_w_kv_b,
               mla_q_nope_norm, mla_q_pe_norm, mla_k_nope_norm, mla_k_pe_norm, diff_q_norm, diff_k_norm,
               diff_lambda_q1, diff_lambda_k1, diff_lambda_q2, diff_lambda_k2, diff_subln, w_out,
               norm_ffn, w_gate, w_up, conv_w, conv_b, w_down, ple_norm, w_ple_gate, w_ple_proj)
    depth = norm_mix.shape[0]
    groups = []
    for x, p in ((x_prompt, p_prompt), (x_sample, p_sample)):
        b, t, d = x.shape
        groups.append(dict(x=x.reshape(b * t, d), p=p.reshape(depth, b * t, p.shape[-1]), batch=b, seq_len=t,
                           rope=_rope_tables(t), shape=x.shape))
    for i in range(depth):
        lw = _layer_params(i, *weights)
        for g in groups:
            g["x"] = _encoder_layer(g["x"], g["p"][i], lw, *g["rope"], layer_idx=i,
                                    batch=g["batch"], seq_len=g["seq_len"])
    return tuple(g["x"].reshape(g["shape"]) for g in groups)
```

```python
import functools
import math

import jax
import jax.numpy as jnp
import numpy as np
from jax import lax
from jax.experimental import pallas as pl
from jax.experimental.pallas import tpu as pltpu

F32 = jnp.float32
BF16 = jnp.bfloat16

EPS = 1e-6
LOG2E = 1.4426950408889634
NEG_BIG = -1e30

D_MODEL = 2048
PLE_DIM = 256
GRID_W = 64
HEAD_DIM = 128
NA_HEADS = 6
NA_WIN_ROWS = 8
NA_WIN_COLS = 16
MLA_HEADS = 5
MLA_Q_RANK = 512
MLA_KV_RANK = 256
MLA_NOPE_DIM = 128
MLA_ROPE_DIM = 64
MLA_V_DIM = 128
MLA_ROPE_THETA = 10000.0
DIFF_HEADS = 5
DIFF_QK_DIM = 64
DIFF_V_DIM = 128
ROPE_THETA = 500000.0
ROPE_PART_DIM = DIFF_QK_DIM // 4
D_FF = 5632

NA_W = NA_HEADS * HEAD_DIM
MLA_QK_DIM = MLA_NOPE_DIM + MLA_ROPE_DIM
MLA_SLOT = 256
DIFF_W = DIFF_HEADS * 2 * DIFF_QK_DIM
DIFF_VW = DIFF_HEADS * DIFF_V_DIM
MLA_VW = MLA_HEADS * MLA_V_DIM
VT_ROWS = 128 + 16

C_NAQ = 0
C_NAK = C_NAQ + NA_W
C_NAV = C_NAK + NA_W
C_CQ = C_NAV + NA_W
C_CKV = C_CQ + MLA_Q_RANK
C_DQ = C_CKV + MLA_KV_RANK
C_DK = C_DQ + DIFF_W
C_DV = C_DK + DIFF_W
C_KPE = C_DV + DIFF_VW
IN_COLS_PAD = C_KPE + 128

LANES = 128
FF_TILE = 512
FLASH_MAX_KEYS = 8192
FLASH_DOUBLE_BUFFER_BYTES = 2 * 1024 * 1024
VMEM_LIMIT = 56 * 1024 * 1024
VMEM_LIMIT_FFN = 60 * 1024 * 1024


def _cparams(sem, limit=VMEM_LIMIT):
    return pltpu.CompilerParams(dimension_semantics=sem, vmem_limit_bytes=limit)


def _rms(x, g, n):
    ms = jnp.sum(x * x, axis=-1, keepdims=True) * (1.0 / n)
    return x * lax.rsqrt(ms + EPS) * g


def _proj_prep_kernel(x_ref, mix_g, w_ref, rope_m_ref, rope_d_ref, naq_g, nak_g, qa_g, wqb_ref, kva_g, wkvb_ref,
                      qn_g, qpe_g, kn_g, kpe_g, dq_g, dk_g,
                      naq_o, nak_o, navT_o, mq_o, mk_o, mvT_o, dq_o, dk_o, dvT_o):
    tm = x_ref.shape[0]
    xn = _rms(x_ref[...], mix_g[...], x_ref.shape[-1]).astype(BF16)

    def proj(col, width):
        return jnp.dot(xn, w_ref[:, col:col + width], preferred_element_type=F32)

    lane = lax.broadcasted_iota(jnp.int32, (tm, LANES), 1)
    lo = lane < 64
    ones_row = (lax.broadcasted_iota(jnp.int32, (VT_ROWS - 128, tm), 0) == 0).astype(BF16)

    na_scale = HEAD_DIM ** -0.5 * LOG2E
    na_q, na_k, na_v = proj(C_NAQ, NA_W), proj(C_NAK, NA_W), proj(C_NAV, NA_W)
    for h in range(NA_HEADS):
        sl = slice(h * HEAD_DIM, (h + 1) * HEAD_DIM)
        naq_o[:, sl] = (_rms(na_q[:, sl], naq_g[...], HEAD_DIM) * na_scale).astype(BF16)
        nak_o[:, sl] = _rms(na_k[:, sl], nak_g[...], HEAD_DIM).astype(BF16)
        navT_o[h * VT_ROWS:h * VT_ROWS + HEAD_DIM, :] = na_v[:, sl].T.astype(BF16)
        navT_o[h * VT_ROWS + HEAD_DIM:(h + 1) * VT_ROWS, :] = ones_row

    cm, sam, sbm = rope_m_ref[:, 0:128], rope_m_ref[:, 128:256], rope_m_ref[:, 256:384]

    def rope_m(y):
        return y * cm + pltpu.roll(y, 96, 1) * sam + pltpu.roll(y, 32, 1) * sbm

    mla_scale = MLA_QK_DIM ** -0.5 * LOG2E
    cq = _rms(proj(C_CQ, MLA_Q_RANK), qa_g[...], MLA_Q_RANK).astype(BF16)
    qm = jnp.dot(cq, wqb_ref[...], preferred_element_type=F32)
    for h in range(MLA_HEADS):
        b = h * MLA_SLOT
        nope = _rms(qm[:, b:b + 128], qn_g[...], MLA_NOPE_DIM)
        pe = rope_m(_rms(qm[:, b + 128:b + 256], qpe_g[...], MLA_ROPE_DIM))
        mq_o[:, b:b + 128] = (nope * mla_scale).astype(BF16)
        mq_o[:, b + 128:b + 256] = (pe * mla_scale).astype(BF16)
    ckv = _rms(proj(C_CKV, MLA_KV_RANK), kva_g[...], MLA_KV_RANK).astype(BF16)
    kv = jnp.dot(ckv, wkvb_ref[...], preferred_element_type=F32)
    kpe = rope_m(_rms(proj(C_KPE, 128), kpe_g[...], MLA_ROPE_DIM))
    kpe = kpe.astype(BF16)
    for h in range(MLA_HEADS):
        b = h * MLA_SLOT
        kn = _rms(kv[:, b:b + 128], kn_g[...], MLA_NOPE_DIM)
        mk_o[:, b:b + 128] = kn.astype(BF16)
        mk_o[:, b + 128:b + 256] = kpe
        mvT_o[h * VT_ROWS:h * VT_ROWS + MLA_V_DIM, :] = kv[:, b + 128:b + 256].T.astype(BF16)
        mvT_o[h * VT_ROWS + MLA_V_DIM:(h + 1) * VT_ROWS, :] = ones_row

    cd, sad, sbd = rope_d_ref[:, 0:128], rope_d_ref[:, 128:256], rope_d_ref[:, 256:384]

    def rope_d(y):
        return y * cd + pltpu.roll(y, 120, 1) * sad + pltpu.roll(y, 8, 1) * sbd

    def group_rms(x, g):
        x2 = x * x
        s_lo = jnp.sum(jnp.where(lo, x2, 0.0), axis=-1, keepdims=True)
        s_hi = jnp.sum(jnp.where(lo, 0.0, x2), axis=-1, keepdims=True)
        ms = jnp.where(lo, s_lo, s_hi) * (1.0 / DIFF_QK_DIM)
        return x * lax.rsqrt(ms + EPS) * g

    def split_components(y):
        return jnp.where(lo, y, 0.0), jnp.where(lo, pltpu.roll(y, 64, 1), 0.0)

    df_scale = DIFF_QK_DIM ** -0.5 * LOG2E
    df_q, df_k, df_v = proj(C_DQ, DIFF_W), proj(C_DK, DIFF_W), proj(C_DV, DIFF_VW)
    for h in range(DIFF_HEADS):
        q = rope_d(group_rms(df_q[:, h * 128:(h + 1) * 128], dq_g[...])) * df_scale
        q0, q1 = split_components(q)
        dq_o[:, (2 * h) * 128:(2 * h + 1) * 128] = q0.astype(BF16)
        dq_o[:, (2 * h + 1) * 128:(2 * h + 2) * 128] = q1.astype(BF16)
        k = rope_d(group_rms(df_k[:, h * 128:(h + 1) * 128], dk_g[...]))
        k0, k1 = split_components(k)
        dk_o[:, (2 * h) * 128:(2 * h + 1) * 128] = k0.astype(BF16)
        dk_o[:, (2 * h + 1) * 128:(2 * h + 2) * 128] = k1.astype(BF16)
        v = df_v[:, h * DIFF_V_DIM:(h + 1) * DIFF_V_DIM]
        dvT_o[h * VT_ROWS:h * VT_ROWS + DIFF_V_DIM, :] = v.T.astype(BF16)
        dvT_o[h * VT_ROWS + DIFF_V_DIM:(h + 1) * VT_ROWS, :] = ones_row


def proj_prep(x, rope_m, rope_d, lw, *, seq_len, tm):
    M, D = x.shape
    nt = seq_len // tm
    row = lambda w: pl.BlockSpec((tm, w), lambda i: (i, 0))
    colT = lambda h: pl.BlockSpec((h, tm), lambda i: (0, i))
    full = lambda a: pl.BlockSpec(a.shape, lambda i: (0,) * a.ndim)
    rope = pl.BlockSpec((tm, 384), lambda i: (i % nt, 0))
    w_in = pl.BlockSpec(lw["w_in"].shape, lambda i: (0, 0), pipeline_mode=pl.Buffered(1))
    params = (lw["naq_g"], lw["nak_g"], lw["qa_g"], lw["wqb"], lw["kva_g"], lw["wkvb"],
              lw["qn_g"], lw["qpe_g"], lw["kn_g"], lw["kpe_g"], lw["dq_g"], lw["dk_g"])
    out_shapes = (
        jax.ShapeDtypeStruct((M, NA_W), BF16), jax.ShapeDtypeStruct((M, NA_W), BF16),
        jax.ShapeDtypeStruct((NA_HEADS * VT_ROWS, M), BF16),
        jax.ShapeDtypeStruct((M, MLA_HEADS * MLA_SLOT), BF16),
        jax.ShapeDtypeStruct((M, MLA_HEADS * MLA_SLOT), BF16),
        jax.ShapeDtypeStruct((MLA_HEADS * VT_ROWS, M), BF16),
        jax.ShapeDtypeStruct((M, 2 * DIFF_HEADS * 128), BF16),
        jax.ShapeDtypeStruct((M, 2 * DIFF_HEADS * 128), BF16),
        jax.ShapeDtypeStruct((DIFF_HEADS * VT_ROWS, M), BF16),
    )
    out_specs = (row(NA_W), row(NA_W), colT(NA_HEADS * VT_ROWS), row(MLA_HEADS * MLA_SLOT), row(MLA_HEADS * MLA_SLOT),
                 colT(MLA_HEADS * VT_ROWS), row(2 * DIFF_HEADS * 128), row(2 * DIFF_HEADS * 128),
                 colT(DIFF_HEADS * VT_ROWS))
    return pl.pallas_call(
        _proj_prep_kernel,
        grid=(M // tm,),
        in_specs=[row(D), full(lw["mix_g"]), w_in, rope, rope] + [full(a) for a in params],
        out_specs=out_specs,
        out_shape=out_shapes,
        compiler_params=_cparams(("parallel",)),
        name="proj_prep",
    )(x, lw["mix_g"], lw["w_in"], rope_m, rope_d, *params)


def _flash_kernel(q_ref, k_ref, vT_ref, o_ref, sa_ref, sb_ref, acc_ref, *, tq, tk):
    seq_len = q_ref.shape[0]
    dv = o_ref.shape[1]
    nk = seq_len // tk
    total = (seq_len // tq) * nk
    sub = 16

    def scores(f):
        qoff = pl.multiple_of((f // nk) * tq, tq)
        koff = pl.multiple_of((f % nk) * tk, tk)
        s = lax.dot_general(k_ref[pl.ds(koff, tk), :], q_ref[pl.ds(qoff, tq), :],
                            (((1,), (1,)), ((), ())), preferred_element_type=F32)
        return s.astype(BF16)

    def softmax_pv(s_ref, c, m):
        s = s_ref[...]
        part = jnp.max(s.reshape(tk // sub, sub, tq), axis=0).astype(F32)
        m_new = jnp.maximum(m, jnp.max(part, axis=0, keepdims=True))
        alpha = jnp.exp2(m - m_new)
        p = jnp.exp2(s - m_new.astype(BF16))
        koff = pl.multiple_of(c * tk, tk)
        pv = jnp.dot(vT_ref[:, pl.ds(koff, tk)], p, preferred_element_type=F32)
        acc_ref[...] = alpha * acc_ref[...] + pv
        return m_new

    def write_block(qb):
        qoff = pl.multiple_of(qb * tq, tq)
        out = acc_ref[0:dv, :] / acc_ref[dv:dv + 1, :]
        o_ref[pl.ds(qoff, tq), :] = out.T.astype(o_ref.dtype)

    def pair(i, m):
        f = 2 * i
        if nk == 1:
            sb_ref[...] = scores(f + 1)
            softmax_pv(sa_ref, 0, m)
            write_block(f)
            sa_ref[...] = scores(jnp.minimum(f + 2, total - 1))
            softmax_pv(sb_ref, 0, m)
            write_block(f + 1)
            return m
        if nk == 2:
            sb_ref[...] = scores(f + 1)
            m_first = softmax_pv(sa_ref, 0, m)
            sa_ref[...] = scores(jnp.minimum(f + 2, total - 1))
            softmax_pv(sb_ref, 1, m_first)
            write_block(i)
            return m
        c = f % nk
        m = jnp.where(c == 0, NEG_BIG, m)
        sb_ref[...] = scores(f + 1)
        m = softmax_pv(sa_ref, c, m)
        sa_ref[...] = scores(jnp.minimum(f + 2, total - 1))
        m = softmax_pv(sb_ref, c + 1, m)

        @pl.when(c + 2 == nk)
        def _():
            write_block(f // nk)

        return m

    acc_ref[...] = jnp.zeros_like(acc_ref)
    sa_ref[...] = scores(0)
    lax.fori_loop(0, total // 2, pair, jnp.full((1, tq), NEG_BIG, F32))


def flash_attention(q, k, vT, *, batch, seq_len, n_heads, dq, v_of_head, tq, tk, out_dtype):
    M = q.shape[0]
    dv = 128
    nk = seq_len // tk
    assert seq_len % tk == 0 and seq_len % tq == 0 and (nk % 2 == 0 or (nk == 1 and (seq_len // tq) % 2 == 0))
    once = pl.Buffered(1 if seq_len * dq * 2 > FLASH_DOUBLE_BUFFER_BYTES else 2)
    return pl.pallas_call(
        functools.partial(_flash_kernel, tq=tq, tk=tk),
        grid=(batch, n_heads),
        in_specs=[pl.BlockSpec((seq_len, dq), lambda b, h: (b, h), pipeline_mode=once),
                  pl.BlockSpec((seq_len, dq), lambda b, h: (b, h), pipeline_mode=once),
                  pl.BlockSpec((VT_ROWS, seq_len), lambda b, h: (v_of_head(h), b), pipeline_mode=once)],
        out_specs=pl.BlockSpec((seq_len, dv), lambda b, h: (b, h)),
        out_shape=jax.ShapeDtypeStruct((M, n_heads * dv), out_dtype),
        scratch_shapes=[pltpu.VMEM((tk, tq), BF16), pltpu.VMEM((tk, tq), BF16), pltpu.VMEM((VT_ROWS, tq), F32)],
        compiler_params=_cparams(("parallel", "parallel")),
        name="flash_attention",
    )(q, k, vT)


NA_QROWS = 8
NA_KROWS = 2 * NA_WIN_ROWS


NA_HEADS_PER_STEP = 3
NA_ALL_HEADS_MAX_SEQ = 4096


def _na_kernel(q_ref, k_ref, vT_ref, bias_ref, o_ref, *, rows):
    i = pl.program_id(2)
    ws = jnp.clip(i * NA_QROWS - NA_WIN_ROWS // 2, 0, rows - NA_KROWS)
    ks = pl.multiple_of(ws * GRID_W, (NA_WIN_ROWS // 2) * GRID_W)
    win = NA_KROWS * GRID_W
    dv = HEAD_DIM
    sub = 16
    for j in range(q_ref.shape[1] // dv):
        cols = slice(j * dv, (j + 1) * dv)
        s = lax.dot_general(k_ref[pl.ds(ks, win), cols], q_ref[:, cols], (((1,), (1,)), ((), ())),
                            preferred_element_type=F32)
        s = (s + bias_ref[j]).astype(BF16)
        part = jnp.max(s.reshape(win // sub, sub, s.shape[1]), axis=0).astype(F32)
        m = jnp.max(part, axis=0, keepdims=True)
        p = jnp.exp2(s - m.astype(BF16))
        acc = jnp.dot(vT_ref[j * VT_ROWS:(j + 1) * VT_ROWS, pl.ds(ks, win)], p,
                      preferred_element_type=F32)
        o_ref[:, cols] = (acc[0:dv, :] / acc[dv:dv + 1, :]).T.astype(o_ref.dtype)


def na_attention(q, k, vT, bias, *, batch, seq_len):
    M = q.shape[0]
    rows = seq_len // GRID_W
    assert rows >= NA_KROWS and rows % NA_QROWS == 0
    nblk = rows // NA_QROWS
    tq = NA_QROWS * GRID_W
    pattern = lambda i: jnp.where(i == 0, 0, jnp.where(i == nblk - 1, 2, 1))
    hs = NA_HEADS if seq_len <= NA_ALL_HEADS_MAX_SEQ else NA_HEADS_PER_STEP
    once = pl.Buffered(1)
    return pl.pallas_call(
        functools.partial(_na_kernel, rows=rows),
        grid=(batch, NA_HEADS // hs, nblk),
        in_specs=[pl.BlockSpec((tq, hs * HEAD_DIM), lambda b, h, i: (b * nblk + i, h)),
                  pl.BlockSpec((seq_len, hs * HEAD_DIM), lambda b, h, i: (b, h), pipeline_mode=once),
                  pl.BlockSpec((hs * VT_ROWS, seq_len), lambda b, h, i: (h, b), pipeline_mode=once),
                  pl.BlockSpec((hs, None, NA_KROWS * GRID_W, tq), lambda b, h, i: (h, pattern(i), 0, 0))],
        out_specs=pl.BlockSpec((tq, hs * HEAD_DIM), lambda b, h, i: (b * nblk + i, h)),
        out_shape=jax.ShapeDtypeStruct((M, NA_W), BF16),
        compiler_params=_cparams(("parallel", "parallel", "arbitrary")),
        name="na_attention",
    )(q, k, vT, bias)


def _out_proj_kernel(x_ref, ona_ref, omla_ref, odf_ref, subln_ref, lq1, lk1, lq2, lk2,
                     w_na, w_mla, w_df, ffn_g, o_ref, h_ref, *, lam_init):
    lam = (jnp.exp(jnp.sum(lq1[...] * lk1[...], axis=-1, keepdims=True))
           - jnp.exp(jnp.sum(lq2[...] * lk2[...], axis=-1, keepdims=True)) + lam_init)
    heads = []
    for h in range(DIFF_HEADS):
        o1 = odf_ref[:, (2 * h) * 128:(2 * h + 1) * 128]
        o2 = odf_ref[:, (2 * h + 1) * 128:(2 * h + 2) * 128]
        o = _rms(o1 - lam * o2, subln_ref[...], DIFF_V_DIM) * (1.0 - lam_init)
        heads.append(o.astype(BF16))
    odf = jnp.concatenate(heads, axis=-1)
    y = jnp.dot(ona_ref[...], w_na[...], preferred_element_type=F32)
    y = y + jnp.dot(omla_ref[...], w_mla[...], preferred_element_type=F32)
    y = y + jnp.dot(odf, w_df[...], preferred_element_type=F32)
    x1 = x_ref[...] + y
    o_ref[...] = x1
    h_ref[...] = _rms(x1, ffn_g[...], x1.shape[-1]).astype(BF16)


def out_proj(x, o_na, o_mla, o_df, lw, *, lam_init, tm):
    M, D = x.shape
    row = lambda w: pl.BlockSpec((tm, w), lambda i: (i, 0))
    full = lambda a: pl.BlockSpec(a.shape, lambda i: (0,) * a.ndim)
    params = (lw["subln_g"], lw["lq1"], lw["lk1"], lw["lq2"], lw["lk2"], lw["w_out_na"], lw["w_out_mla"],
              lw["w_out_df"], lw["ffn_g"])
    return pl.pallas_call(
        functools.partial(_out_proj_kernel, lam_init=lam_init),
        grid=(M // tm,),
        in_specs=[row(D), row(NA_W), row(MLA_VW), row(2 * DIFF_HEADS * 128)] + [full(a) for a in params],
        out_specs=(row(D), row(D)),
        out_shape=(jax.ShapeDtypeStruct((M, D), F32), jax.ShapeDtypeStruct((M, D), BF16)),
        compiler_params=_cparams(("parallel",)),
        name="out_proj",
    )(x, o_na, o_mla, o_df, *params)


HALO = 16


def _ffn_kernel(x_ref, h_ref, hp_ref, hn_ref, wg_ref, wu_ref, cw_ref, cb_ref, wd_ref,
                p_ref, pg_ref, wpg_ref, wpp_ref, o_ref, xs_ref, acc_ref, *, seq_len):
    i = pl.program_id(0)
    f = pl.program_id(1)
    tm = x_ref.shape[0]

    @pl.when(f == 0)
    def _():
        row0 = i * tm
        at_start = (row0 % seq_len) == 0
        at_end = ((row0 + tm) % seq_len) == 0
        prev, nxt = hp_ref[...], hn_ref[...]
        xs_ref[0:HALO, :] = jnp.where(at_start, jnp.zeros_like(prev), prev)
        xs_ref[HALO:HALO + tm, :] = h_ref[...]
        xs_ref[HALO + tm:2 * HALO + tm, :] = jnp.where(at_end, jnp.zeros_like(nxt), nxt)
        acc_ref[...] = jnp.zeros_like(acc_ref)

    n_ext = tm + 2 * HALO
    gate = jnp.dot(xs_ref[...], wg_ref[...], preferred_element_type=F32)
    g_prev = pltpu.roll(gate, 1, 0)[HALO:HALO + tm, :]
    g_next = pltpu.roll(gate, n_ext - 1, 0)[HALO:HALO + tm, :]
    g_mid = gate[HALO:HALO + tm, :]
    g = g_prev * cw_ref[0:1, :] + g_mid * cw_ref[1:2, :] + g_next * cw_ref[2:3, :] + cb_ref[...]
    u = jnp.dot(xs_ref[HALO:HALO + tm, :], wu_ref[...], preferred_element_type=F32)
    a = (g * jax.nn.sigmoid(g) * u).astype(BF16)
    acc_ref[...] += jnp.dot(a, wd_ref[...], preferred_element_type=F32)

    @pl.when(f == pl.num_programs(1) - 1)
    def _():
        y = x_ref[...] + acc_ref[...]
        yn = _rms(y, pg_ref[...], y.shape[-1]).astype(BF16)
        gate = jax.nn.sigmoid(jnp.dot(yn, wpg_ref[...], preferred_element_type=F32))
        emb = jnp.dot(p_ref[...].astype(BF16), wpp_ref[...], preferred_element_type=F32)
        o_ref[...] = y + gate * emb


def ffn_ple(x, h, p, lw, *, seq_len, tm):
    M, D = x.shape
    tf = FF_TILE
    nf = lw["w_gate"].shape[1] // tf
    assert seq_len % tm == 0 and tm % HALO == 0
    hb = tm // HALO
    last = M // HALO - 1
    return pl.pallas_call(
        functools.partial(_ffn_kernel, seq_len=seq_len),
        grid=(M // tm, nf),
        in_specs=[pl.BlockSpec((tm, D), lambda i, f: (i, 0)),
                  pl.BlockSpec((tm, D), lambda i, f: (i, 0)),
                  pl.BlockSpec((HALO, D), lambda i, f: (jnp.maximum(i * hb - 1, 0), 0)),
                  pl.BlockSpec((HALO, D), lambda i, f: (jnp.minimum((i + 1) * hb, last), 0)),
                  pl.BlockSpec((D, tf), lambda i, f: (0, f)),
                  pl.BlockSpec((D, tf), lambda i, f: (0, f)),
                  pl.BlockSpec((3, tf), lambda i, f: (0, f)),
                  pl.BlockSpec((1, tf), lambda i, f: (0, f)),
                  pl.BlockSpec((tf, D), lambda i, f: (f, 0)),
                  pl.BlockSpec((tm, p.shape[1]), lambda i, f: (i, 0)),
                  pl.BlockSpec((1, D), lambda i, f: (0, 0)),
                  pl.BlockSpec((D, D), lambda i, f: (0, 0), pipeline_mode=pl.Buffered(1)),
                  pl.BlockSpec((p.shape[1], D), lambda i, f: (0, 0), pipeline_mode=pl.Buffered(1))],
        out_specs=pl.BlockSpec((tm, D), lambda i, f: (i, 0)),
        out_shape=jax.ShapeDtypeStruct((M, D), F32),
        scratch_shapes=[pltpu.VMEM((tm + 2 * HALO, D), BF16), pltpu.VMEM((tm, D), F32)],
        compiler_params=_cparams(("parallel", "arbitrary"), VMEM_LIMIT_FFN),
        name="ffn_ple",
    )(x, h, h, h, lw["w_gate"], lw["w_up"], lw["conv_w"], lw["conv_b"], lw["w_down"],
      p, lw["ple_g"], lw["w_ple_gate"], lw["w_ple_proj"])


def _rope_tables(seq_len):
    def angles(dim, theta):
        inv = 1.0 / (theta ** (jnp.arange(0, dim, 2, dtype=F32) / dim))
        ang = jnp.arange(seq_len, dtype=F32)[:, None] * inv[None, :]
        return jnp.cos(ang), jnp.sin(ang)

    z = lambda w: jnp.zeros((seq_len, w), F32)
    cos, sin = angles(MLA_ROPE_DIM, MLA_ROPE_THETA)
    half = MLA_ROPE_DIM // 2
    cm = jnp.concatenate([cos, cos, z(64)], axis=1)
    sam = jnp.concatenate([-sin, z(half), z(64)], axis=1)
    sbm = jnp.concatenate([z(half), sin, z(64)], axis=1)
    rope_m = jnp.concatenate([cm, sam, sbm], axis=1)

    cos, sin = angles(ROPE_PART_DIM, ROPE_THETA)
    half = ROPE_PART_DIM // 2
    rest = DIFF_QK_DIM - ROPE_PART_DIM
    c64 = jnp.concatenate([cos, cos, jnp.ones((seq_len, rest), F32)], axis=1)
    sa64 = jnp.concatenate([-sin, z(half), z(rest)], axis=1)
    sb64 = jnp.concatenate([z(half), sin, z(rest)], axis=1)
    rope_d = jnp.concatenate([c64, c64, sa64, sa64, sb64, sb64], axis=1)
    return rope_m, rope_d


def _na_bias_table(rpb):
    cols = jnp.arange(GRID_W)
    cs = jnp.clip(cols - NA_WIN_COLS // 2, 0, GRID_W - NA_WIN_COLS)
    kc = jnp.arange(GRID_W)
    valid = (kc[None, :] >= cs[:, None]) & (kc[None, :] < cs[:, None] + NA_WIN_COLS)
    nv = 2 * NA_WIN_COLS - 1
    onehot = (kc[None, None, :] - cols[None, :, None] + (NA_WIN_COLS - 1) == jnp.arange(nv)[:, None, None])
    toep = jnp.einsum("huv,vck->huck", rpb.astype(F32), onehot.astype(F32), precision=lax.Precision.HIGHEST)
    toep = jnp.where(valid[None, None], toep * LOG2E, NEG_BIG)
    n_h = rpb.shape[0]
    masked = 2 * NA_WIN_ROWS - 1
    toep = jnp.concatenate([toep, jnp.full((n_h, 1, GRID_W, GRID_W), NEG_BIG, F32)], axis=1)
    half = NA_WIN_ROWS // 2
    tile = np.full((3, NA_QROWS, NA_KROWS), masked, np.int32)
    for pat, delta in enumerate((0, -half, -NA_WIN_ROWS)):
        for a in range(NA_QROWS):
            first = (max(a - half, 0), a, min(a + half, NA_WIN_ROWS))[pat]
            for i in range(first, first + NA_WIN_ROWS):
                tile[pat, a, i] = delta + i - a + (NA_WIN_ROWS - 1)
    tab = jnp.take(toep, jnp.asarray(tile.reshape(-1)), axis=1)
    tab = tab.reshape(n_h, 3, NA_QROWS, NA_KROWS, GRID_W, GRID_W).transpose(0, 1, 3, 5, 2, 4)
    return tab.reshape(n_h, 3, NA_KROWS * GRID_W, NA_QROWS * GRID_W)


def _layer_params(i, norm_mix, w_in, na_q_norm, na_k_norm, na_rpb, mla_q_a_norm, mla_w_q_b, mla_kv_a_norm,
                  mla_w_kv_b, mla_q_nope_norm, mla_q_pe_norm, mla_k_nope_norm, mla_k_pe_norm,
                  diff_q_norm, diff_k_norm, diff_lambda_q1, diff_lambda_k1, diff_lambda_q2, diff_lambda_k2,
                  diff_subln, w_out, norm_ffn, w_gate, w_up, conv_w, conv_b, w_down,
                  ple_norm, w_ple_gate, w_ple_proj):
    r = lambda a: a[i].reshape(1, -1).astype(F32)
    pad_to = lambda a, n: jnp.pad(a, ((0, 0), (0, n - a.shape[1])))
    w = w_in[i]
    kpe0 = 3 * NA_W + MLA_Q_RANK + MLA_KV_RANK
    w_perm = jnp.concatenate([w[:, :kpe0], w[:, kpe0 + MLA_ROPE_DIM:], w[:, kpe0:kpe0 + MLA_ROPE_DIM],
                              jnp.zeros((w.shape[0], 128 - MLA_ROPE_DIM), w.dtype)], axis=1).astype(BF16)
    wqb = mla_w_q_b[i].reshape(MLA_Q_RANK, MLA_HEADS, MLA_QK_DIM)
    wqb = jnp.pad(wqb, ((0, 0), (0, 0), (0, MLA_SLOT - MLA_QK_DIM))).reshape(MLA_Q_RANK, MLA_HEADS * MLA_SLOT)
    wo = w_out[i].astype(BF16)
    return dict(
        mix_g=r(norm_mix), w_in=w_perm,
        naq_g=r(na_q_norm), nak_g=r(na_k_norm), na_bias=_na_bias_table(na_rpb[i]),
        qa_g=r(mla_q_a_norm), wqb=wqb.astype(BF16), kva_g=r(mla_kv_a_norm), wkvb=mla_w_kv_b[i].astype(BF16),
        qn_g=r(mla_q_nope_norm), qpe_g=pad_to(r(mla_q_pe_norm), 128),
        kn_g=r(mla_k_nope_norm), kpe_g=pad_to(r(mla_k_pe_norm), 128),
        dq_g=jnp.tile(r(diff_q_norm), (1, 2)), dk_g=jnp.tile(r(diff_k_norm), (1, 2)),
        lq1=r(diff_lambda_q1), lk1=r(diff_lambda_k1), lq2=r(diff_lambda_q2), lk2=r(diff_lambda_k2),
        subln_g=r(diff_subln),
        w_out_na=wo[:NA_W], w_out_mla=wo[NA_W:NA_W + MLA_VW], w_out_df=wo[NA_W + MLA_VW:],
        ffn_g=r(norm_ffn), w_gate=w_gate[i].astype(BF16), w_up=w_up[i].astype(BF16),
        conv_w=conv_w[i].astype(F32), conv_b=r(conv_b), w_down=w_down[i].astype(BF16),
        ple_g=r(ple_norm), w_ple_gate=w_ple_gate[i].astype(BF16), w_ple_proj=w_ple_proj[i].astype(BF16),
    )


def _tile(n, pref):
    t = min(pref, n)
    while n % t:
        t //= 2
    return t


def _encoder_layer(x, p_l, lw, rope_m, rope_d, *, layer_idx, batch, seq_len):
    tm = _tile(seq_len, 512)
    (na_q, na_k, na_vT, m_q, m_k, m_vT, d_q, d_k, d_vT) = proj_prep(
        x, rope_m, rope_d, lw, seq_len=seq_len, tm=_tile(seq_len, 256))
    o_na = na_attention(na_q, na_k, na_vT, lw["na_bias"], batch=batch, seq_len=seq_len)
    tq = _tile(seq_len, 512)
    tk = min(seq_len, FLASH_MAX_KEYS)
    o_mla = flash_attention(m_q, m_k, m_vT, batch=batch, seq_len=seq_len, n_heads=MLA_HEADS, dq=MLA_SLOT,
                            v_of_head=lambda h: h, tq=tq, tk=tk, out_dtype=BF16)
    o_df = flash_attention(d_q, d_k, d_vT, batch=batch, seq_len=seq_len, n_heads=2 * DIFF_HEADS, dq=128,
                           v_of_head=lambda h: h // 2, tq=tq, tk=tk, out_dtype=F32)
    lam_init = 0.8 - 0.6 * math.exp(-0.3 * layer_idx)
    x, h = out_proj(x, o_na, o_mla, o_df, lw, lam_init=lam_init, tm=tm)
    return ffn_ple(x, h, p_l, lw, seq_len=seq_len, tm=tm)


def kernel(x_prompt, x_sample, p_prompt, p_sample, norm_mix, w_in, na_q_norm, na_k_norm, na_rpb, mla_q_a_norm, mla_w_q_b, mla_kv_a_norm, mla_w_kv_b, mla_q_nope_norm, mla_q_pe_norm, mla_k_nope_norm, mla_k_pe_norm, diff_q_norm, diff_k_norm, diff_lambda_q1, diff_lambda_k1, diff_lambda_q2, diff_lambda_k2, diff_subln, w_out, norm_ffn, w_gate, w_up, conv_w, conv_b, w_down, ple_norm, w_ple_gate, w_ple_proj):
    weights = (norm_mix, w_in, na_q_norm, na_k_norm, na_rpb, mla_q_a_norm, mla_w_q_b, mla_kv_a_norm, mla_w_kv_b,
               mla_q_nope_norm, mla_q_pe_norm, mla_k_nope_norm, mla_k_pe_norm, diff_q_norm, diff_k_norm,
               diff_lambda_q1, diff_lambda_k1, diff_lambda_q2, diff_lambda_k2, diff_subln, w_out,
               norm_ffn, w_gate, w_up, conv_w, conv_b, w_down, ple_norm, w_ple_gate, w_ple_proj)
    depth = norm_mix.shape[0]
    groups = []
    for x, p in ((x_prompt, p_prompt), (x_sample, p_sample)):
        b, t, d = x.shape
        groups.append(dict(x=x.reshape(b * t, d), p=p.reshape(depth, b * t, p.shape[-1]), batch=b, seq_len=t,
                           rope=_rope_tables(t), shape=x.shape))
    for i in range(depth):
        lw = _layer_params(i, *weights)
        for g in groups:
            g["x"] = _encoder_layer(g["x"], g["p"][i], lw, *g["rope"], layer_idx=i,
                                    batch=g["batch"], seq_len=g["seq_len"])
    return tuple(g["x"].reshape(g["shape"]) for g in groups)
```

```python
import functools
import math

import jax
import jax.numpy as jnp
import numpy as np
from jax import lax
from jax.experimental import pallas as pl
from jax.experimental.pallas import tpu as pltpu

F32 = jnp.float32
BF16 = jnp.bfloat16

EPS = 1e-6
LOG2E = 1.4426950408889634
NEG_BIG = -1e30

D_MODEL = 2048
PLE_DIM = 256
GRID_W = 64
HEAD_DIM = 128
NA_HEADS = 6
NA_WIN_ROWS = 8
NA_WIN_COLS = 16
MLA_HEADS = 5
MLA_Q_RANK = 512
MLA_KV_RANK = 256
MLA_NOPE_DIM = 128
MLA_ROPE_DIM = 64
MLA_V_DIM = 128
MLA_ROPE_THETA = 10000.0
DIFF_HEADS = 5
DIFF_QK_DIM = 64
DIFF_V_DIM = 128
ROPE_THETA = 500000.0
ROPE_PART_DIM = DIFF_QK_DIM // 4
D_FF = 5632

NA_W = NA_HEADS * HEAD_DIM
MLA_QK_DIM = MLA_NOPE_DIM + MLA_ROPE_DIM
MLA_SLOT = 256
DIFF_W = DIFF_HEADS * 2 * DIFF_QK_DIM
DIFF_VW = DIFF_HEADS * DIFF_V_DIM
MLA_VW = MLA_HEADS * MLA_V_DIM
VT_ROWS = 128 + 16

C_NAQ = 0
C_NAK = C_NAQ + NA_W
C_NAV = C_NAK + NA_W
C_CQ = C_NAV + NA_W
C_CKV = C_CQ + MLA_Q_RANK
C_DQ = C_CKV + MLA_KV_RANK
C_DK = C_DQ + DIFF_W
C_DV = C_DK + DIFF_W
C_KPE = C_DV + DIFF_VW
IN_COLS_PAD = C_KPE + 128

LANES = 128
FF_TILE = 512
FLASH_MAX_KEYS = 8192
FLASH_DOUBLE_BUFFER_BYTES = 2 * 1024 * 1024
VMEM_LIMIT = 56 * 1024 * 1024
VMEM_LIMIT_FFN = 60 * 1024 * 1024


def _cparams(sem, limit=VMEM_LIMIT):
    return pltpu.CompilerParams(dimension_semantics=sem, vmem_limit_bytes=limit)


def _rms(x, g, n):
    ms = jnp.sum(x * x, axis=-1, keepdims=True) * (1.0 / n)
    return x * lax.rsqrt(ms + EPS) * g


def _proj_prep_kernel(x_ref, mix_g, w_ref, rope_m_ref, rope_d_ref, naq_g, nak_g, qa_g, wqb_ref, kva_g, wkvb_ref,
                      qn_g, qpe_g, kn_g, kpe_g, dq_g, dk_g,
                      naq_o, nak_o, navT_o, mq_o, mk_o, mvT_o, dq_o, dk_o, dvT_o):
    tm = x_ref.shape[0]
    xn = _rms(x_ref[...], mix_g[...], x_ref.shape[-1]).astype(BF16)

    def proj(col, width):
        return jnp.dot(xn, w_ref[:, col:col + width], preferred_element_type=F32)

    lane = lax.broadcasted_iota(jnp.int32, (tm, LANES), 1)
    lo = lane < 64
    ones_row = (lax.broadcasted_iota(jnp.int32, (VT_ROWS - 128, tm), 0) == 0).astype(BF16)

    na_scale = HEAD_DIM ** -0.5 * LOG2E
    p_cq, p_ckv, p_kpe = proj(C_CQ, MLA_Q_RANK), proj(C_CKV, MLA_KV_RANK), proj(C_KPE, 128)
    na_q, na_k, na_v = proj(C_NAQ, NA_W), proj(C_NAK, NA_W), proj(C_NAV, NA_W)
    for h in range(NA_HEADS):
        sl = slice(h * HEAD_DIM, (h + 1) * HEAD_DIM)
        naq_o[:, sl] = (_rms(na_q[:, sl], naq_g[...], HEAD_DIM) * na_scale).astype(BF16)
        nak_o[:, sl] = _rms(na_k[:, sl], nak_g[...], HEAD_DIM).astype(BF16)
        navT_o[h * VT_ROWS:h * VT_ROWS + HEAD_DIM, :] = na_v[:, sl].T.astype(BF16)
        navT_o[h * VT_ROWS + HEAD_DIM:(h + 1) * VT_ROWS, :] = ones_row

    cm, sam, sbm = rope_m_ref[:, 0:128], rope_m_ref[:, 128:256], rope_m_ref[:, 256:384]

    def rope_m(y):
        return y * cm + pltpu.roll(y, 96, 1) * sam + pltpu.roll(y, 32, 1) * sbm

    mla_scale = MLA_QK_DIM ** -0.5 * LOG2E
    cq = _rms(p_cq, qa_g[...], MLA_Q_RANK).astype(BF16)
    qm = jnp.dot(cq, wqb_ref[...], preferred_element_type=F32)
    for h in range(MLA_HEADS):
        b = h * MLA_SLOT
        nope = _rms(qm[:, b:b + 128], qn_g[...], MLA_NOPE_DIM)
        pe = rope_m(_rms(qm[:, b + 128:b + 256], qpe_g[...], MLA_ROPE_DIM))
        mq_o[:, b:b + 128] = (nope * mla_scale).astype(BF16)
        mq_o[:, b + 128:b + 256] = (pe * mla_scale).astype(BF16)
    ckv = _rms(p_ckv, kva_g[...], MLA_KV_RANK).astype(BF16)
    kv = jnp.dot(ckv, wkvb_ref[...], preferred_element_type=F32)
    kpe = rope_m(_rms(p_kpe, kpe_g[...], MLA_ROPE_DIM))
    kpe = kpe.astype(BF16)
    for h in range(MLA_HEADS):
        b = h * MLA_SLOT
        kn = _rms(kv[:, b:b + 128], kn_g[...], MLA_NOPE_DIM)
        mk_o[:, b:b + 128] = kn.astype(BF16)
        mk_o[:, b + 128:b + 256] = kpe
        mvT_o[h * VT_ROWS:h * VT_ROWS + MLA_V_DIM, :] = kv[:, b + 128:b + 256].T.astype(BF16)
        mvT_o[h * VT_ROWS + MLA_V_DIM:(h + 1) * VT_ROWS, :] = ones_row

    cd, sad, sbd = rope_d_ref[:, 0:128], rope_d_ref[:, 128:256], rope_d_ref[:, 256:384]

    def rope_d(y):
        return y * cd + pltpu.roll(y, 120, 1) * sad + pltpu.roll(y, 8, 1) * sbd

    def group_rms(x, g):
        x2 = x * x
        s_lo = jnp.sum(jnp.where(lo, x2, 0.0), axis=-1, keepdims=True)
        s_hi = jnp.sum(jnp.where(lo, 0.0, x2), axis=-1, keepdims=True)
        ms = jnp.where(lo, s_lo, s_hi) * (1.0 / DIFF_QK_DIM)
        return x * lax.rsqrt(ms + EPS) * g

    def split_components(y):
        return jnp.where(lo, y, 0.0), jnp.where(lo, pltpu.roll(y, 64, 1), 0.0)

    df_scale = DIFF_QK_DIM ** -0.5 * LOG2E
    df_q, df_k, df_v = proj(C_DQ, DIFF_W), proj(C_DK, DIFF_W), proj(C_DV, DIFF_VW)
    for h in range(DIFF_HEADS):
        q = rope_d(group_rms(df_q[:, h * 128:(h + 1) * 128], dq_g[...])) * df_scale
        q0, q1 = split_components(q)
        dq_o[:, (2 * h) * 128:(2 * h + 1) * 128] = q0.astype(BF16)
        dq_o[:, (2 * h + 1) * 128:(2 * h + 2) * 128] = q1.astype(BF16)
        k = rope_d(group_rms(df_k[:, h * 128:(h + 1) * 128], dk_g[...]))
        k0, k1 = split_components(k)
        dk_o[:, (2 * h) * 128:(2 * h + 1) * 128] = k0.astype(BF16)
        dk_o[:, (2 * h + 1) * 128:(2 * h + 2) * 128] = k1.astype(BF16)
        v = df_v[:, h * DIFF_V_DIM:(h + 1) * DIFF_V_DIM]
        dvT_o[h * VT_ROWS:h * VT_ROWS + DIFF_V_DIM, :] = v.T.astype(BF16)
        dvT_o[h * VT_ROWS + DIFF_V_DIM:(h + 1) * VT_ROWS, :] = ones_row


def proj_prep(x, rope_m, rope_d, lw, *, seq_len, tm):
    M, D = x.shape
    nt = seq_len // tm
    row = lambda w: pl.BlockSpec((tm, w), lambda i: (i, 0))
    colT = lambda h: pl.BlockSpec((h, tm), lambda i: (0, i))
    full = lambda a: pl.BlockSpec(a.shape, lambda i: (0,) * a.ndim)
    rope = pl.BlockSpec((tm, 384), lambda i: (i % nt, 0))
    w_in = pl.BlockSpec(lw["w_in"].shape, lambda i: (0, 0), pipeline_mode=pl.Buffered(1))
    params = (lw["naq_g"], lw["nak_g"], lw["qa_g"], lw["wqb"], lw["kva_g"], lw["wkvb"],
              lw["qn_g"], lw["qpe_g"], lw["kn_g"], lw["kpe_g"], lw["dq_g"], lw["dk_g"])
    out_shapes = (
        jax.ShapeDtypeStruct((M, NA_W), BF16), jax.ShapeDtypeStruct((M, NA_W), BF16),
        jax.ShapeDtypeStruct((NA_HEADS * VT_ROWS, M), BF16),
        jax.ShapeDtypeStruct((M, MLA_HEADS * MLA_SLOT), BF16),
        jax.ShapeDtypeStruct((M, MLA_HEADS * MLA_SLOT), BF16),
        jax.ShapeDtypeStruct((MLA_HEADS * VT_ROWS, M), BF16),
        jax.ShapeDtypeStruct((M, 2 * DIFF_HEADS * 128), BF16),
        jax.ShapeDtypeStruct((M, 2 * DIFF_HEADS * 128), BF16),
        jax.ShapeDtypeStruct((DIFF_HEADS * VT_ROWS, M), BF16),
    )
    out_specs = (row(NA_W), row(NA_W), colT(NA_HEADS * VT_ROWS), row(MLA_HEADS * MLA_SLOT), row(MLA_HEADS * MLA_SLOT),
                 colT(MLA_HEADS * VT_ROWS), row(2 * DIFF_HEADS * 128), row(2 * DIFF_HEADS * 128),
                 colT(DIFF_HEADS * VT_ROWS))
    return pl.pallas_call(
        _proj_prep_kernel,
        grid=(M // tm,),
        in_specs=[row(D), full(lw["mix_g"]), w_in, rope, rope] + [full(a) for a in params],
        out_specs=out_specs,
        out_shape=out_shapes,
        compiler_params=_cparams(("parallel",)),
        name="proj_prep",
    )(x, lw["mix_g"], lw["w_in"], rope_m, rope_d, *params)


def _flash_kernel(q_ref, k_ref, vT_ref, o_ref, sa_ref, sb_ref, acc_ref, *, tq, tk):
    seq_len = q_ref.shape[0]
    dv = o_ref.shape[1]
    nk = seq_len // tk
    total = (seq_len // tq) * nk
    sub = 16

    def scores(f):
        qoff = pl.multiple_of((f // nk) * tq, tq)
        koff = pl.multiple_of((f % nk) * tk, tk)
        s = lax.dot_general(k_ref[pl.ds(koff, tk), :], q_ref[pl.ds(qoff, tq), :],
                            (((1,), (1,)), ((), ())), preferred_element_type=F32)
        return s.astype(BF16)

    def softmax_pv(s_ref, c, m):
        s = s_ref[...]
        part = jnp.max(s.reshape(tk // sub, sub, tq), axis=0).astype(F32)
        m_new = jnp.maximum(m, jnp.max(part, axis=0, keepdims=True))
        alpha = jnp.exp2(m - m_new)
        p = jnp.exp2(s - m_new.astype(BF16))
        koff = pl.multiple_of(c * tk, tk)
        pv = jnp.dot(vT_ref[:, pl.ds(koff, tk)], p, preferred_element_type=F32)
        acc_ref[...] = alpha * acc_ref[...] + pv
        return m_new

    def write_block(qb):
        qoff = pl.multiple_of(qb * tq, tq)
        out = acc_ref[0:dv, :] / acc_ref[dv:dv + 1, :]
        o_ref[pl.ds(qoff, tq), :] = out.T.astype(o_ref.dtype)

    def pair(i, m):
        f = 2 * i
        if nk == 1:
            sb_ref[...] = scores(f + 1)
            softmax_pv(sa_ref, 0, m)
            write_block(f)
            sa_ref[...] = scores(jnp.minimum(f + 2, total - 1))
            softmax_pv(sb_ref, 0, m)
            write_block(f + 1)
            return m
        if nk == 2:
            sb_ref[...] = scores(f + 1)
            m_first = softmax_pv(sa_ref, 0, m)
            sa_ref[...] = scores(jnp.minimum(f + 2, total - 1))
            softmax_pv(sb_ref, 1, m_first)
            write_block(i)
            return m
        c = f % nk
        m = jnp.where(c == 0, NEG_BIG, m)
        sb_ref[...] = scores(f + 1)
        m = softmax_pv(sa_ref, c, m)
        sa_ref[...] = scores(jnp.minimum(f + 2, total - 1))
        m = softmax_pv(sb_ref, c + 1, m)

        @pl.when(c + 2 == nk)
        def _():
            write_block(f // nk)

        return m

    acc_ref[...] = jnp.zeros_like(acc_ref)
    sa_ref[...] = scores(0)
    lax.fori_loop(0, total // 2, pair, jnp.full((1, tq), NEG_BIG, F32))


def flash_attention(q, k, vT, *, batch, seq_len, n_heads, dq, v_of_head, tq, tk, out_dtype):
    M = q.shape[0]
    dv = 128
    nk = seq_len // tk
    assert seq_len % tk == 0 and seq_len % tq == 0 and (nk % 2 == 0 or (nk == 1 and (seq_len // tq) % 2 == 0))
    once = pl.Buffered(1 if seq_len * dq * 2 > FLASH_DOUBLE_BUFFER_BYTES else 2)
    return pl.pallas_call(
        functools.partial(_flash_kernel, tq=tq, tk=tk),
        grid=(batch, n_heads),
        in_specs=[pl.BlockSpec((seq_len, dq), lambda b, h: (b, h), pipeline_mode=once),
                  pl.BlockSpec((seq_len, dq), lambda b, h: (b, h), pipeline_mode=once),
                  pl.BlockSpec((VT_ROWS, seq_len), lambda b, h: (v_of_head(h), b), pipeline_mode=once)],
        out_specs=pl.BlockSpec((seq_len, dv), lambda b, h: (b, h)),
        out_shape=jax.ShapeDtypeStruct((M, n_heads * dv), out_dtype),
        scratch_shapes=[pltpu.VMEM((tk, tq), BF16), pltpu.VMEM((tk, tq), BF16), pltpu.VMEM((VT_ROWS, tq), F32)],
        compiler_params=_cparams(("parallel", "parallel")),
        name="flash_attention",
    )(q, k, vT)


NA_QROWS = 8
NA_KROWS = 2 * NA_WIN_ROWS


NA_HEADS_PER_STEP = 3


def _na_kernel(q_ref, k_ref, vT_ref, bias_ref, o_ref, *, rows):
    i = pl.program_id(2)
    ws = jnp.clip(i * NA_QROWS - NA_WIN_ROWS // 2, 0, rows - NA_KROWS)
    ks = pl.multiple_of(ws * GRID_W, (NA_WIN_ROWS // 2) * GRID_W)
    win = NA_KROWS * GRID_W
    dv = HEAD_DIM
    sub = 16
    for j in range(NA_HEADS_PER_STEP):
        cols = slice(j * dv, (j + 1) * dv)
        s = lax.dot_general(k_ref[pl.ds(ks, win), cols], q_ref[:, cols], (((1,), (1,)), ((), ())),
                            preferred_element_type=F32)
        s = (s + bias_ref[j]).astype(BF16)
        part = jnp.max(s.reshape(win // sub, sub, s.shape[1]), axis=0).astype(F32)
        m = jnp.max(part, axis=0, keepdims=True)
        p = jnp.exp2(s - m.astype(BF16))
        acc = jnp.dot(vT_ref[j * VT_ROWS:(j + 1) * VT_ROWS, pl.ds(ks, win)], p,
                      preferred_element_type=F32)
        o_ref[:, cols] = (acc[0:dv, :] / acc[dv:dv + 1, :]).T.astype(o_ref.dtype)


def na_attention(q, k, vT, bias, *, batch, seq_len):
    M = q.shape[0]
    rows = seq_len // GRID_W
    assert rows >= NA_KROWS and rows % NA_QROWS == 0
    nblk = rows // NA_QROWS
    tq = NA_QROWS * GRID_W
    pattern = lambda i: jnp.where(i == 0, 0, jnp.where(i == nblk - 1, 2, 1))
    hs = NA_HEADS_PER_STEP
    once = pl.Buffered(1)
    return pl.pallas_call(
        functools.partial(_na_kernel, rows=rows),
        grid=(batch, NA_HEADS // hs, nblk),
        in_specs=[pl.BlockSpec((tq, hs * HEAD_DIM), lambda b, h, i: (b * nblk + i, h)),
                  pl.BlockSpec((seq_len, hs * HEAD_DIM), lambda b, h, i: (b, h), pipeline_mode=once),
                  pl.BlockSpec((hs * VT_ROWS, seq_len), lambda b, h, i: (h, b), pipeline_mode=once),
                  pl.BlockSpec((hs, None, NA_KROWS * GRID_W, tq), lambda b, h, i: (h, pattern(i), 0, 0))],
        out_specs=pl.BlockSpec((tq, hs * HEAD_DIM), lambda b, h, i: (b * nblk + i, h)),
        out_shape=jax.ShapeDtypeStruct((M, NA_W), BF16),
        compiler_params=_cparams(("parallel", "parallel", "arbitrary")),
        name="na_attention",
    )(q, k, vT, bias)


def _out_proj_kernel(x_ref, ona_ref, omla_ref, odf_ref, subln_ref, lq1, lk1, lq2, lk2,
                     w_na, w_mla, w_df, ffn_g, o_ref, h_ref, *, lam_init):
    lam = (jnp.exp(jnp.sum(lq1[...] * lk1[...], axis=-1, keepdims=True))
           - jnp.exp(jnp.sum(lq2[...] * lk2[...], axis=-1, keepdims=True)) + lam_init)
    heads = []
    for h in range(DIFF_HEADS):
        o1 = odf_ref[:, (2 * h) * 128:(2 * h + 1) * 128]
        o2 = odf_ref[:, (2 * h + 1) * 128:(2 * h + 2) * 128]
        o = _rms(o1 - lam * o2, subln_ref[...], DIFF_V_DIM) * (1.0 - lam_init)
        heads.append(o.astype(BF16))
    odf = jnp.concatenate(heads, axis=-1)
    y = jnp.dot(ona_ref[...], w_na[...], preferred_element_type=F32)
    y = y + jnp.dot(omla_ref[...], w_mla[...], preferred_element_type=F32)
    y = y + jnp.dot(odf, w_df[...], preferred_element_type=F32)
    x1 = x_ref[...] + y
    o_ref[...] = x1
    h_ref[...] = _rms(x1, ffn_g[...], x1.shape[-1]).astype(BF16)


def out_proj(x, o_na, o_mla, o_df, lw, *, lam_init, tm):
    M, D = x.shape
    row = lambda w: pl.BlockSpec((tm, w), lambda i: (i, 0))
    full = lambda a: pl.BlockSpec(a.shape, lambda i: (0,) * a.ndim)
    params = (lw["subln_g"], lw["lq1"], lw["lk1"], lw["lq2"], lw["lk2"], lw["w_out_na"], lw["w_out_mla"],
              lw["w_out_df"], lw["ffn_g"])
    return pl.pallas_call(
        functools.partial(_out_proj_kernel, lam_init=lam_init),
        grid=(M // tm,),
        in_specs=[row(D), row(NA_W), row(MLA_VW), row(2 * DIFF_HEADS * 128)] + [full(a) for a in params],
        out_specs=(row(D), row(D)),
        out_shape=(jax.ShapeDtypeStruct((M, D), F32), jax.ShapeDtypeStruct((M, D), BF16)),
        compiler_params=_cparams(("parallel",)),
        name="out_proj",
    )(x, o_na, o_mla, o_df, *params)


HALO = 16


def _ffn_kernel(x_ref, h_ref, hp_ref, hn_ref, wg_ref, wu_ref, cw_ref, cb_ref, wd_ref,
                p_ref, pg_ref, wpg_ref, wpp_ref, o_ref, xs_ref, acc_ref, *, seq_len):
    i = pl.program_id(0)
    f = pl.program_id(1)
    tm = x_ref.shape[0]

    @pl.when(f == 0)
    def _():
        row0 = i * tm
        at_start = (row0 % seq_len) == 0
        at_end = ((row0 + tm) % seq_len) == 0
        prev, nxt = hp_ref[...], hn_ref[...]
        xs_ref[0:HALO, :] = jnp.where(at_start, jnp.zeros_like(prev), prev)
        xs_ref[HALO:HALO + tm, :] = h_ref[...]
        xs_ref[HALO + tm:2 * HALO + tm, :] = jnp.where(at_end, jnp.zeros_like(nxt), nxt)
        acc_ref[...] = jnp.zeros_like(acc_ref)

    n_ext = tm + 2 * HALO
    gate = jnp.dot(xs_ref[...], wg_ref[...], preferred_element_type=F32)
    g_prev = pltpu.roll(gate, 1, 0)[HALO:HALO + tm, :]
    g_next = pltpu.roll(gate, n_ext - 1, 0)[HALO:HALO + tm, :]
    g_mid = gate[HALO:HALO + tm, :]
    g = g_prev * cw_ref[0:1, :] + g_mid * cw_ref[1:2, :] + g_next * cw_ref[2:3, :] + cb_ref[...]
    u = jnp.dot(xs_ref[HALO:HALO + tm, :], wu_ref[...], preferred_element_type=F32)
    a = (g * jax.nn.sigmoid(g) * u).astype(BF16)
    acc_ref[...] += jnp.dot(a, wd_ref[...], preferred_element_type=F32)

    @pl.when(f == pl.num_programs(1) - 1)
    def _():
        y = x_ref[...] + acc_ref[...]
        yn = _rms(y, pg_ref[...], y.shape[-1]).astype(BF16)
        gate = jax.nn.sigmoid(jnp.dot(yn, wpg_ref[...], preferred_element_type=F32))
        emb = jnp.dot(p_ref[...].astype(BF16), wpp_ref[...], preferred_element_type=F32)
        o_ref[...] = y + gate * emb


def ffn_ple(x, h, p, lw, *, seq_len, tm):
    M, D = x.shape
    tf = FF_TILE
    nf = lw["w_gate"].shape[1] // tf
    assert seq_len % tm == 0 and tm % HALO == 0
    hb = tm // HALO
    last = M // HALO - 1
    return pl.pallas_call(
        functools.partial(_ffn_kernel, seq_len=seq_len),
        grid=(M // tm, nf),
        in_specs=[pl.BlockSpec((tm, D), lambda i, f: (i, 0)),
                  pl.BlockSpec((tm, D), lambda i, f: (i, 0)),
                  pl.BlockSpec((HALO, D), lambda i, f: (jnp.maximum(i * hb - 1, 0), 0)),
                  pl.BlockSpec((HALO, D), lambda i, f: (jnp.minimum((i + 1) * hb, last), 0)),
                  pl.BlockSpec((D, tf), lambda i, f: (0, f)),
                  pl.BlockSpec((D, tf), lambda i, f: (0, f)),
                  pl.BlockSpec((3, tf), lambda i, f: (0, f)),
                  pl.BlockSpec((1, tf), lambda i, f: (0, f)),
                  pl.BlockSpec((tf, D), lambda i, f: (f, 0)),
                  pl.BlockSpec((tm, p.shape[1]), lambda i, f: (i, 0)),
                  pl.BlockSpec((1, D), lambda i, f: (0, 0)),
                  pl.BlockSpec((D, D), lambda i, f: (0, 0), pipeline_mode=pl.Buffered(1)),
                  pl.BlockSpec((p.shape[1], D), lambda i, f: (0, 0), pipeline_mode=pl.Buffered(1))],
        out_specs=pl.BlockSpec((tm, D), lambda i, f: (i, 0)),
        out_shape=jax.ShapeDtypeStruct((M, D), F32),
        scratch_shapes=[pltpu.VMEM((tm + 2 * HALO, D), BF16), pltpu.VMEM((tm, D), F32)],
        compiler_params=_cparams(("parallel", "arbitrary"), VMEM_LIMIT_FFN),
        name="ffn_ple",
    )(x, h, h, h, lw["w_gate"], lw["w_up"], lw["conv_w"], lw["conv_b"], lw["w_down"],
      p, lw["ple_g"], lw["w_ple_gate"], lw["w_ple_proj"])


def _rope_tables(seq_len):
    def angles(dim, theta):
        inv = 1.0 / (theta ** (jnp.arange(0, dim, 2, dtype=F32) / dim))
        ang = jnp.arange(seq_len, dtype=F32)[:, None] * inv[None, :]
        return jnp.cos(ang), jnp.sin(ang)

    z = lambda w: jnp.zeros((seq_len, w), F32)
    cos, sin = angles(MLA_ROPE_DIM, MLA_ROPE_THETA)
    half = MLA_ROPE_DIM // 2
    cm = jnp.concatenate([cos, cos, z(64)], axis=1)
    sam = jnp.concatenate([-sin, z(half), z(64)], axis=1)
    sbm = jnp.concatenate([z(half), sin, z(64)], axis=1)
    rope_m = jnp.concatenate([cm, sam, sbm], axis=1)

    cos, sin = angles(ROPE_PART_DIM, ROPE_THETA)
    half = ROPE_PART_DIM // 2
    rest = DIFF_QK_DIM - ROPE_PART_DIM
    c64 = jnp.concatenate([cos, cos, jnp.ones((seq_len, rest), F32)], axis=1)
    sa64 = jnp.concatenate([-sin, z(half), z(rest)], axis=1)
    sb64 = jnp.concatenate([z(half), sin, z(rest)], axis=1)
    rope_d = jnp.concatenate([c64, c64, sa64, sa64, sb64, sb64], axis=1)
    return rope_m, rope_d


def _na_bias_table(rpb):
    cols = jnp.arange(GRID_W)
    cs = jnp.clip(cols - NA_WIN_COLS // 2, 0, GRID_W - NA_WIN_COLS)
    kc = jnp.arange(GRID_W)
    valid = (kc[None, :] >= cs[:, None]) & (kc[None, :] < cs[:, None] + NA_WIN_COLS)
    nv = 2 * NA_WIN_COLS - 1
    onehot = (kc[None, None, :] - cols[None, :, None] + (NA_WIN_COLS - 1) == jnp.arange(nv)[:, None, None])
    toep = jnp.einsum("huv,vck->huck", rpb.astype(F32), onehot.astype(F32), precision=lax.Precision.HIGHEST)
    toep = jnp.where(valid[None, None], toep * LOG2E, NEG_BIG)
    n_h = rpb.shape[0]
    masked = 2 * NA_WIN_ROWS - 1
    toep = jnp.concatenate([toep, jnp.full((n_h, 1, GRID_W, GRID_W), NEG_BIG, F32)], axis=1)
    half = NA_WIN_ROWS // 2
    tile = np.full((3, NA_QROWS, NA_KROWS), masked, np.int32)
    for pat, delta in enumerate((0, -half, -NA_WIN_ROWS)):
        for a in range(NA_QROWS):
            first = (max(a - half, 0), a, min(a + half, NA_WIN_ROWS))[pat]
            for i in range(first, first + NA_WIN_ROWS):
                tile[pat, a, i] = delta + i - a + (NA_WIN_ROWS - 1)
    tab = jnp.take(toep, jnp.asarray(tile.reshape(-1)), axis=1)
    tab = tab.reshape(n_h, 3, NA_QROWS, NA_KROWS, GRID_W, GRID_W).transpose(0, 1, 3, 5, 2, 4)
    return tab.reshape(n_h, 3, NA_KROWS * GRID_W, NA_QROWS * GRID_W)


def _layer_params(i, norm_mix, w_in, na_q_norm, na_k_norm, na_rpb, mla_q_a_norm, mla_w_q_b, mla_kv_a_norm,
                  mla_w_kv_b, mla_q_nope_norm, mla_q_pe_norm, mla_k_nope_norm, mla_k_pe_norm,
                  diff_q_norm, diff_k_norm, diff_lambda_q1, diff_lambda_k1, diff_lambda_q2, diff_lambda_k2,
                  diff_subln, w_out, norm_ffn, w_gate, w_up, conv_w, conv_b, w_down,
                  ple_norm, w_ple_gate, w_ple_proj):
    r = lambda a: a[i].reshape(1, -1).astype(F32)
    pad_to = lambda a, n: jnp.pad(a, ((0, 0), (0, n - a.shape[1])))
    w = w_in[i]
    kpe0 = 3 * NA_W + MLA_Q_RANK + MLA_KV_RANK
    w_perm = jnp.concatenate([w[:, :kpe0], w[:, kpe0 + MLA_ROPE_DIM:], w[:, kpe0:kpe0 + MLA_ROPE_DIM],
                              jnp.zeros((w.shape[0], 128 - MLA_ROPE_DIM), w.dtype)], axis=1).astype(BF16)
    wqb = mla_w_q_b[i].reshape(MLA_Q_RANK, MLA_HEADS, MLA_QK_DIM)
    wqb = jnp.pad(wqb, ((0, 0), (0, 0), (0, MLA_SLOT - MLA_QK_DIM))).reshape(MLA_Q_RANK, MLA_HEADS * MLA_SLOT)
    wo = w_out[i].astype(BF16)
    return dict(
        mix_g=r(norm_mix), w_in=w_perm,
        naq_g=r(na_q_norm), nak_g=r(na_k_norm), na_bias=_na_bias_table(na_rpb[i]),
        qa_g=r(mla_q_a_norm), wqb=wqb.astype(BF16), kva_g=r(mla_kv_a_norm), wkvb=mla_w_kv_b[i].astype(BF16),
        qn_g=r(mla_q_nope_norm), qpe_g=pad_to(r(mla_q_pe_norm), 128),
        kn_g=r(mla_k_nope_norm), kpe_g=pad_to(r(mla_k_pe_norm), 128),
        dq_g=jnp.tile(r(diff_q_norm), (1, 2)), dk_g=jnp.tile(r(diff_k_norm), (1, 2)),
        lq1=r(diff_lambda_q1), lk1=r(diff_lambda_k1), lq2=r(diff_lambda_q2), lk2=r(diff_lambda_k2),
        subln_g=r(diff_subln),
        w_out_na=wo[:NA_W], w_out_mla=wo[NA_W:NA_W + MLA_VW], w_out_df=wo[NA_W + MLA_VW:],
        ffn_g=r(norm_ffn), w_gate=w_gate[i].astype(BF16), w_up=w_up[i].astype(BF16),
        conv_w=conv_w[i].astype(F32), conv_b=r(conv_b), w_down=w_down[i].astype(BF16),
        ple_g=r(ple_norm), w_ple_gate=w_ple_gate[i].astype(BF16), w_ple_proj=w_ple_proj[i].astype(BF16),
    )


def _tile(n, pref):
    t = min(pref, n)
    while n % t:
        t //= 2
    return t


def _encoder_layer(x, p_l, lw, rope_m, rope_d, *, layer_idx, batch, seq_len):
    tm = _tile(seq_len, 512)
    (na_q, na_k, na_vT, m_q, m_k, m_vT, d_q, d_k, d_vT) = proj_prep(
        x, rope_m, rope_d, lw, seq_len=seq_len, tm=_tile(seq_len, 256))
    o_na = na_attention(na_q, na_k, na_vT, lw["na_bias"], batch=batch, seq_len=seq_len)
    tq = _tile(seq_len, 512)
    tk = min(seq_len, FLASH_MAX_KEYS)
    o_mla = flash_attention(m_q, m_k, m_vT, batch=batch, seq_len=seq_len, n_heads=MLA_HEADS, dq=MLA_SLOT,
                            v_of_head=lambda h: h, tq=tq, tk=tk, out_dtype=BF16)
    o_df = flash_attention(d_q, d_k, d_vT, batch=batch, seq_len=seq_len, n_heads=2 * DIFF_HEADS, dq=128,
                           v_of_head=lambda h: h // 2, tq=tq, tk=tk, out_dtype=F32)
    lam_init = 0.8 - 0.6 * math.exp(-0.3 * layer_idx)
    x, h = out_proj(x, o_na, o_mla, o_df, lw, lam_init=lam_init, tm=tm)
    return ffn_ple(x, h, p_l, lw, seq_len=seq_len, tm=tm)


def kernel(x_prompt, x_sample, p_prompt, p_sample, norm_mix, w_in, na_q_norm, na_k_norm, na_rpb, mla_q_a_norm, mla_w_q_b, mla_kv_a_norm, mla_w_kv_b, mla_q_nope_norm, mla_q_pe_norm, mla_k_nope_norm, mla_k_pe_norm, diff_q_norm, diff_k_norm, diff_lambda_q1, diff_lambda_k1, diff_lambda_q2, diff_lambda_k2, diff_subln, w_out, norm_ffn, w_gate, w_up, conv_w, conv_b, w_down, ple_norm, w_ple_gate, w_ple_proj):
    weights = (norm_mix, w_in, na_q_norm, na_k_norm, na_rpb, mla_q_a_norm, mla_w_q_b, mla_kv_a_norm, mla_w_kv_b,
               mla_q_nope_norm, mla_q_pe_norm, mla_k_nope_norm, mla_k_pe_norm, diff_q_norm, diff_k_norm,
               diff_lambda_q1, diff_lambda_k1, diff_lambda_q2, diff_lambda_k2, diff_subln, w_out,
               norm_ffn, w_gate, w_up, conv_w, conv_b, w_down, ple_norm, w_ple_gate, w_ple_proj)
    depth = norm_mix.shape[0]
    groups = []
    for x, p in ((x_prompt, p_prompt), (x_sample, p_sample)):
        b, t, d = x.shape
        groups.append(dict(x=x.reshape(b * t, d), p=p.reshape(depth, b * t, p.shape[-1]), batch=b, seq_len=t,
                           rope=_rope_tables(t), shape=x.shape))
    for i in range(depth):
        lw = _layer_params(i, *weights)
        for g in groups:
            g["x"] = _encoder_layer(g["x"], g["p"][i], lw, *g["rope"], layer_idx=i,
                                    batch=g["batch"], seq_len=g["seq_len"])
    return tuple(g["x"].reshape(g["shape"]) for g in groups)
```
